```python
import jax
import jax.numpy as jnp
from jax import lax
import numpy as np

D_MODEL = 4096
BATCH = 8
SEQ = 2048
DEPTH = 2

GRID_W = 64
CTX_LEN = 256
N_MIXERS = 2
N_ATTN_LAYERS = (DEPTH + N_MIXERS - 1) // N_MIXERS
N_SGU_LAYERS = DEPTH // N_MIXERS
NORM_EPS = 1e-6

NA_HEAD_DIM = 128
NA_HEADS = D_MODEL // NA_HEAD_DIM
NA_KH_MAX = 8
NA_KW = 16
NA_QB = 16
NA_BW = NA_KW + NA_QB
NA_NCB = GRID_W // NA_QB
ROPE_BASE = 10000.0
ROPE_FREQS = NA_HEAD_DIM // 4

SGU_CHUNK = 128
SGU_WIDTH = D_MODEL
SGU_GROUPS = 16
SGU_GROUP_DIM = SGU_WIDTH // SGU_GROUPS

N_EXPERTS = 16
N_GROUPS = 4
EXPERTS_PER_GROUP = N_EXPERTS // N_GROUPS
TOP_K = 2
D_EXPERT = 1024

kernel_name = "hybrid_natten_sgu_groupmoe_diffusion"


def rms_norm(x, g):
    xf = x.astype(jnp.float32)
    y = xf * lax.rsqrt(jnp.mean(xf * xf, axis=-1, keepdims=True) + NORM_EPS)
    return (y * g.astype(jnp.float32)).astype(x.dtype)


def layer_norm(x, g, b):
    xf = x.astype(jnp.float32)
    mu = jnp.mean(xf, axis=-1, keepdims=True)
    var = jnp.mean(jnp.square(xf - mu), axis=-1, keepdims=True)
    y = (xf - mu) * lax.rsqrt(var + NORM_EPS) * g.astype(jnp.float32) + b.astype(jnp.float32)
    return y.astype(x.dtype)


def modulate(x, shift, scale):
    return x * (1 + scale) + shift


def ada_modulation(cond_act, w, b, n_chunks):
    m = cond_act @ w[:, : n_chunks * D_MODEL] + b[: n_chunks * D_MODEL]
    return jnp.split(m, n_chunks, axis=-1)


def axial_rope_tables(n_tokens):
    t = jnp.arange(n_tokens, dtype=jnp.int32)
    pos = jnp.stack([t // GRID_W, t % GRID_W], axis=-1).astype(jnp.float32)
    inv_freq = ROPE_BASE ** (-jnp.arange(ROPE_FREQS, dtype=jnp.float32) / ROPE_FREQS)
    ang = pos[:, :, None] * inv_freq
    return jnp.cos(ang), jnp.sin(ang)


def apply_axial_rope(x, cos, sin):
    b, s, h, d = x.shape
    xs = x.reshape(b, s, h, 2, 2, ROPE_FREQS).astype(jnp.float32)
    x1, x2 = xs[..., 0, :], xs[..., 1, :]
    cs, sn = cos[None, :, None], sin[None, :, None]
    out = jnp.stack([x1 * cs - x2 * sn, x2 * cs + x1 * sn], axis=-2)
    return out.reshape(b, s, h, d).astype(x.dtype)


def neighbourhood_attention(h, hc, w_qkv, q_g, k_g, rpb, w_o, cos, sin, with_ctx_out):
    B, S, _ = h.shape
    L = hc.shape[1]
    rows = S // GRID_W
    kh = min(NA_KH_MAX, rows)
    H, hd = NA_HEADS, NA_HEAD_DIM
    scale = hd ** -0.5

    qkv = (h @ w_qkv).reshape(B, S, 3, H, hd)
    q = rms_norm(qkv[:, :, 0], q_g)
    k = rms_norm(qkv[:, :, 1], k_g)
    v = qkv[:, :, 2]
    q_rot = apply_axial_rope(q, cos, sin)
    k_rot = apply_axial_rope(k, cos, sin)

    kv_c = (hc @ w_qkv[:, D_MODEL:]).reshape(B, L, 2, H, hd)
    kc = rms_norm(kv_c[:, :, 0], k_g).transpose(0, 2, 1, 3)
    vc = kv_c[:, :, 1].transpose(0, 2, 1, 3)

    to_grid = lambda t: t.transpose(0, 2, 1, 3).reshape(B, H, rows, GRID_W, hd)
    qg, qpg, kg, vg = to_grid(q_rot), to_grid(q), to_grid(k_rot), to_grid(v)
    z = jnp.zeros((), jnp.int32)
    nk = kh * NA_BW

    def block(n):
        r = n // NA_NCB
        c0 = (n % NA_NCB) * NA_QB
        rs = jnp.clip(r - kh // 2, 0, rows - kh)
        bs = jnp.clip(c0 - NA_KW // 2, 0, GRID_W - NA_BW)
        qb = lax.dynamic_slice(qg, (z, z, r, c0, z), (B, H, 1, NA_QB, hd)).reshape(B, H, NA_QB, hd)
        qpb = lax.dynamic_slice(qpg, (z, z, r, c0, z), (B, H, 1, NA_QB, hd)).reshape(B, H, NA_QB, hd)
        kb = lax.dynamic_slice(kg, (z, z, rs, bs, z), (B, H, kh, NA_BW, hd)).reshape(B, H, nk, hd)
        vb = lax.dynamic_slice(vg, (z, z, rs, bs, z), (B, H, kh, NA_BW, hd)).reshape(B, H, nk, hd)

        q_cols = c0 + jnp.arange(NA_QB, dtype=jnp.int32)
        k_cols = bs + jnp.arange(NA_BW, dtype=jnp.int32)
        k_rows = rs + jnp.arange(kh, dtype=jnp.int32)
        cs = jnp.clip(q_cols - NA_KW // 2, 0, GRID_W - NA_KW)
        in_win = (k_cols[None, :] >= cs[:, None]) & (k_cols[None, :] < cs[:, None] + NA_KW)
        mask = jnp.broadcast_to(in_win[:, None, :], (NA_QB, kh, NA_BW)).reshape(NA_QB, nk)
        dr = k_rows - r + (NA_KH_MAX - 1)
        dc = jnp.clip(k_cols[None, :] - q_cols[:, None] + NA_KW - 1, 0, 2 * NA_KW - 2)
        bias = rpb[:, dr[None, :, None], dc[:, None, :]].reshape(H, NA_QB, nk)

        s_loc = jnp.einsum('bhqd,bhkd->bhqk', qb, kb).astype(jnp.float32) * scale
        s_loc = jnp.where(mask, s_loc + bias.astype(jnp.float32), -jnp.inf)
        s_ctx = jnp.einsum('bhqd,bhkd->bhqk', qpb, kc).astype(jnp.float32) * scale
        p = jax.nn.softmax(jnp.concatenate([s_loc, s_ctx], axis=-1), axis=-1).astype(vb.dtype)
        return (jnp.einsum('bhqk,bhkd->bhqd', p[..., :nk], vb)
                + jnp.einsum('bhqk,bhkd->bhqd', p[..., nk:], vc))

    o = lax.map(block, jnp.arange(rows * NA_NCB, dtype=jnp.int32))
    o = o.reshape(rows, NA_NCB, B, H, NA_QB, hd).transpose(2, 0, 1, 4, 3, 5).reshape(B, S, H * hd)
    y = o @ w_o

    yc = None
    if with_ctx_out:
        qc = rms_norm((hc @ w_qkv[:, :D_MODEL]).reshape(B, L, H, hd), q_g).transpose(0, 2, 1, 3)
        sc = jnp.einsum('bhqd,bhkd->bhqk', qc, kc).astype(jnp.float32) * scale
        pc = jax.nn.softmax(sc, axis=-1).astype(vc.dtype)
        yc = jnp.einsum('bhqk,bhkd->bqhd', pc, vc).reshape(B, L, H * hd) @ w_o
    return y, yc


def spatial_gating_unit(h, w_uv, b_uv, ln_g, ln_b, w_s, b_s, w_out, b_out):
    B, S, _ = h.shape
    zt = jax.nn.gelu(h @ w_uv + b_uv, approximate=False)
    u, v = jnp.split(zt, 2, axis=-1)
    v = layer_norm(v, ln_g, ln_b).reshape(B, S // SGU_CHUNK, SGU_CHUNK, SGU_GROUPS, SGU_GROUP_DIM)
    s = jnp.einsum('gpq,bnqgc->bnpgc', w_s, v) + b_s.T[None, None, :, :, None]
    return (u * s.reshape(B, S, SGU_WIDTH)) @ w_out + b_out


def group_limited_router(t, w_router, router_bias):
    T = t.shape[0]
    scores = jax.nn.sigmoid((t @ w_router).astype(jnp.float32))
    sel = (scores + router_bias.astype(jnp.float32)).reshape(T, N_GROUPS, EXPERTS_PER_GROUP)
    grp_score = jnp.sum(lax.top_k(sel, 2)[0], axis=-1)
    g_idx = jnp.argmax(grp_score, axis=-1)
    in_grp = sel[jnp.arange(T), g_idx]
    _, local = lax.top_k(in_grp, TOP_K)
    expert_idx = g_idx[:, None] * EXPERTS_PER_GROUP + local
    w = jnp.take_along_axis(scores, expert_idx, axis=-1)
    w = w / jnp.sum(w, axis=-1, keepdims=True)
    return jnp.sum(jax.nn.one_hot(expert_idx, N_EXPERTS, dtype=jnp.float32) * w[..., None], axis=1)


def moe_ffn(h, w_router, router_bias, w_gate, w_up, w_down):
    shp = h.shape
    t = h.reshape(-1, D_MODEL)
    comb = group_limited_router(t, w_router, router_bias).astype(h.dtype)
    out = jnp.zeros_like(t)
    for e in range(N_EXPERTS):
        a = jax.nn.silu(t @ w_gate[e]) * (t @ w_up[e])
        out = out + comb[:, e:e + 1] * (a @ w_down[e])
    return out.reshape(shp)


def setup_inputs(seed: int = 0) -> dict:
    key = jax.random.key(seed)
    ks = jax.random.split(key, 26)
    f32 = jnp.float32
    nrm = lambda k, shape, s: jax.random.normal(k, shape, f32) * s
    D, W, E, F = D_MODEL, SGU_WIDTH, N_EXPERTS, D_EXPERT
    return {
        "x": nrm(ks[0], (BATCH, SEQ, D), 1.0),
        "c": nrm(ks[1], (BATCH, D), 1.0),
        "ctx": nrm(ks[2], (BATCH, CTX_LEN, D), 1.0),
        "c_ctx": nrm(ks[3], (D,), 1.0),
        "ada_w": nrm(ks[4], (DEPTH, D, 6 * D), 0.5 * D ** -0.5),
        "ada_b": nrm(ks[5], (DEPTH, 6 * D), 0.01),
        "norm1_g": 1.0 + nrm(ks[6], (DEPTH, D), 0.1),
        "norm2_g": 1.0 + nrm(ks[7], (DEPTH, D), 0.1),
        "na_w_qkv": nrm(ks[8], (N_ATTN_LAYERS, D, 3 * D), D ** -0.5),
        "na_q_g": 1.0 + nrm(ks[9], (N_ATTN_LAYERS, NA_HEAD_DIM), 0.1),
        "na_k_g": 1.0 + nrm(ks[10], (N_ATTN_LAYERS, NA_HEAD_DIM), 0.1),
        "na_rpb": nrm(ks[11], (N_ATTN_LAYERS, NA_HEADS, 2 * NA_KH_MAX - 1, 2 * NA_KW - 1), 0.5),
        "na_w_o": nrm(ks[12], (N_ATTN_LAYERS, D, D), D ** -0.5),
        "sgu_w_uv": nrm(ks[13], (N_SGU_LAYERS, D, 2 * W), D ** -0.5),
        "sgu_b_uv": nrm(ks[14], (N_SGU_LAYERS, 2 * W), 0.01),
        "sgu_ln_g": 1.0 + nrm(ks[15], (N_SGU_LAYERS, W), 0.1),
        "sgu_ln_b": nrm(ks[16], (N_SGU_LAYERS, W), 0.01),
        "sgu_w_s": nrm(ks[17], (N_SGU_LAYERS, SGU_GROUPS, SGU_CHUNK, SGU_CHUNK), SGU_CHUNK ** -0.5),
        "sgu_b_s": 1.0 + nrm(ks[18], (N_SGU_LAYERS, SGU_GROUPS, SGU_CHUNK), 0.1),
        "sgu_w_out": nrm(ks[19], (N_SGU_LAYERS, W, D), W ** -0.5),
        "sgu_b_out": nrm(ks[20], (N_SGU_LAYERS, D), 0.01),
        "router_w": nrm(ks[21], (D, E), D ** -0.5),
        "router_b": nrm(ks[22], (E,), 0.01),
        "moe_w_gate": nrm(ks[23], (DEPTH, E, D, F), D ** -0.5),
        "moe_w_up": nrm(ks[24], (DEPTH, E, D, F), D ** -0.5),
        "moe_w_down": nrm(ks[25], (DEPTH, E, F, D), F ** -0.5),
    }


def reference(x, c, ctx, c_ctx, ada_w, ada_b, norm1_g, norm2_g,
              na_w_qkv, na_q_g, na_k_g, na_rpb, na_w_o,
              sgu_w_uv, sgu_b_uv, sgu_ln_g, sgu_ln_b, sgu_w_s, sgu_b_s, sgu_w_out, sgu_b_out,
              router_w, router_b, moe_w_gate, moe_w_up, moe_w_down):
    S = x.shape[1]
    cos, sin = axial_rope_tables(S)
    c_act = jax.nn.silu(c)
    cc_act = jax.nn.silu(c_ctx)
    for i in range(DEPTH):
        kind = i % N_MIXERS
        slot = i // N_MIXERS
        ctx_live = any(j % N_MIXERS == 0 for j in range(i + 1, DEPTH))
        sh1, sc1, g1, sh2, sc2, g2 = [m[:, None, :] for m in ada_modulation(c_act, ada_w[i], ada_b[i], 6)]
        h = modulate(rms_norm(x, norm1_g[i]), sh1, sc1)

        hc = None
        if ctx_live:
            csh1, csc1, cg1, csh2, csc2, cg2 = ada_modulation(cc_act, ada_w[i], ada_b[i], 6)
            hc = modulate(rms_norm(ctx, norm1_g[i]), csh1, csc1)
        elif kind == 0:
            csh1, csc1 = ada_modulation(cc_act, ada_w[i], ada_b[i], 2)
            hc = modulate(rms_norm(ctx, norm1_g[i]), csh1, csc1)

        if kind == 0:
            y, yc = neighbourhood_attention(h, hc, na_w_qkv[slot], na_q_g[slot], na_k_g[slot],
                                            na_rpb[slot], na_w_o[slot], cos, sin, ctx_live)
        else:
            sgu_args = (sgu_w_uv[slot], sgu_b_uv[slot], sgu_ln_g[slot], sgu_ln_b[slot],
                        sgu_w_s[slot], sgu_b_s[slot], sgu_w_out[slot], sgu_b_out[slot])
            y = spatial_gating_unit(h, *sgu_args)
            yc = spatial_gating_unit(hc, *sgu_args) if ctx_live else None

        x = x + g1 * y
        x = x + g2 * moe_ffn(modulate(rms_norm(x, norm2_g[i]), sh2, sc2),
                             router_w, router_b, moe_w_gate[i], moe_w_up[i], moe_w_down[i])
        if ctx_live:
            ctx = ctx + cg1 * yc
            ctx = ctx + cg2 * moe_ffn(modulate(rms_norm(ctx, norm2_g[i]), csh2, csc2),
                                      router_w, router_b, moe_w_gate[i], moe_w_up[i], moe_w_down[i])
    return x
```

```python
import functools

import jax
import jax.numpy as jnp
from jax import lax
from jax.experimental import pallas as pl
from jax.experimental.pallas import tpu as pltpu

F32, BF16, I32, U32 = jnp.float32, jnp.bfloat16, jnp.int32, jnp.uint32

GRID_W = 64
NORM_EPS = 1e-6
NA_HEAD_DIM = 128
NA_KH = 8
NA_KW = 16
ROPE_BASE = 10000.0
ROPE_FREQS = NA_HEAD_DIM // 4
SGU_CHUNK = 128
SGU_GROUPS = 16
N_EXPERTS = 16
N_GROUPS = 4
EXPERTS_PER_GROUP = N_EXPERTS // N_GROUPS
TOP_K = 2
MASK_VALUE = -1e30

LANES = 128
VMEM_LIMIT_BYTES = 56 * 1024 * 1024
MM_TM = 1024
MM_TN = 512
MM_RC = 256
ROW_TILE = 512
MOE_TM = 512
MOE_A_TN = 256
MOE_B_TN = 2048
GATHER_ROWS = 512
COMBINE_ROWS = 256


def _cparams(*sem):
    return pltpu.CompilerParams(dimension_semantics=sem, vmem_limit_bytes=VMEM_LIMIT_BYTES)


def _dot(a, b):
    return jnp.dot(a, b, preferred_element_type=F32)


def _dot_nt(a, b, precision=None):
    return lax.dot_general(a, b, (((1,), (1,)), ((), ())), precision=precision,
                           preferred_element_type=F32)


def _rms(x, g):
    return x * lax.rsqrt(jnp.mean(x * x, axis=-1, keepdims=True) + NORM_EPS) * g


def _pack_bf16_pair(lo, hi):
    lo_b = lax.bitcast_convert_type(lo.astype(BF16).astype(F32), U32) >> 16
    hi_b = lax.bitcast_convert_type(hi.astype(BF16).astype(F32), U32) & jnp.uint32(0xFFFF0000)
    return lo_b | hi_b


def _unpack_lo(w):
    return lax.bitcast_convert_type(w << 16, F32)


def _unpack_hi(w):
    return lax.bitcast_convert_type(w & jnp.uint32(0xFFFF0000), F32)


def _ada_kernel(c_ref, w_ref, b_ref, o_ref):
    c = c_ref[...]
    a = (c * jax.nn.sigmoid(c)).astype(BF16)
    o_ref[...] = _dot(a, w_ref[...].astype(BF16)) + b_ref[...]


def _ada_mods(cpad, ada_w, ada_b):
    n_layers, d, n = ada_w.shape
    rows = cpad.shape[0]
    tn = MM_TN
    return pl.pallas_call(
        _ada_kernel,
        grid=(n_layers, n // tn),
        in_specs=[pl.BlockSpec((rows, d), lambda l, j: (0, 0)),
                  pl.BlockSpec((None, d, tn), lambda l, j: (l, 0, j)),
                  pl.BlockSpec((None, 1, tn), lambda l, j: (l, 0, j))],
        out_specs=pl.BlockSpec((None, rows, tn), lambda l, j: (l, 0, j)),
        out_shape=jax.ShapeDtypeStruct((n_layers, rows, n), F32),
        compiler_params=_cparams("arbitrary", "arbitrary"),
        name="ada_mods",
    )(cpad, ada_w, ada_b.reshape(n_layers, 1, n))


def _mod_spec(layer, chunk, row_fn):
    def idx(*g):
        return (layer, row_fn(*g), chunk, 0, 0)
    return idx


def _prenorm0_kernel(x_ref, ctx_ref, g_ref, sh_ref, sc_ref, o_ref, *, n_lat):
    i = pl.program_id(0)

    def emit(v):
        o_ref[...] = (_rms(v, g_ref[...]) * (1.0 + sc_ref[...]) + sh_ref[...]).astype(BF16)

    @pl.when(i < n_lat)
    def _():
        emit(x_ref[...])

    @pl.when(i >= n_lat)
    def _():
        emit(ctx_ref[...])


def _prenorm0(x2, ctx2, g, mods, layer, seq, ctx_row):
    t, d = x2.shape
    tc = ctx2.shape[0]
    tm = ROW_TILE
    n_lat, n_ctx = t // tm, tc // tm
    row = lambda i: jnp.where(i < n_lat, (i * tm) // seq, ctx_row)
    mspec = lambda chunk: pl.BlockSpec((None, None, None, 1, d), _mod_spec(layer, chunk, row))
    return pl.pallas_call(
        functools.partial(_prenorm0_kernel, n_lat=n_lat),
        grid=(n_lat + n_ctx,),
        in_specs=[pl.BlockSpec((tm, d), lambda i: (jnp.minimum(i, n_lat - 1), 0)),
                  pl.BlockSpec((tm, d), lambda i: (jnp.maximum(i - n_lat, 0), 0)),
                  pl.BlockSpec((1, d), lambda i: (0, 0)),
                  mspec(0), mspec(1)],
        out_specs=pl.BlockSpec((tm, d), lambda i: (i, 0)),
        out_shape=jax.ShapeDtypeStruct((t + tc, d), BF16),
        compiler_params=_cparams("arbitrary"),
        name="prenorm0",
    )(x2, ctx2, g.reshape(1, d), mods, mods)


def _ws_body(x_ref, w_ref, *refs, n_extra, n_out, epilogue, tm, rc):
    extra, outs, wb_ref = refs[:n_extra], refs[n_extra:n_extra + n_out], refs[-1]

    @pl.when(pl.program_id(1) == 0)
    def _():
        wb_ref[...] = w_ref[...].astype(BF16)

    for c in range(tm // rc):
        rows = slice(c * rc, (c + 1) * rc)
        epilogue(_dot(x_ref[rows, :], wb_ref[...]), rows, extra, outs)


def _ws_matmul(x, w, col0, n_cols, epilogue, extras, extra_specs, out_shapes, out_specs, name,
               tm=MM_TM, tn=MM_TN, m=None):
    k = x.shape[1]
    m = x.shape[0] if m is None else m
    assert m % tm == 0 and n_cols % tn == 0 and col0 % tn == 0
    jb = col0 // tn
    body = functools.partial(_ws_body, n_extra=len(extras), n_out=len(out_shapes),
                             epilogue=epilogue, tm=tm, rc=MM_RC)
    return pl.pallas_call(
        body,
        grid=(n_cols // tn, m // tm),
        in_specs=[pl.BlockSpec((tm, k), lambda j, i: (i, 0)),
                  pl.BlockSpec((k, tn), lambda j, i: (0, j + jb))] + list(extra_specs),
        out_specs=out_specs,
        out_shape=out_shapes,
        scratch_shapes=[pltpu.VMEM((k, tn), BF16)],
        compiler_params=_cparams("arbitrary", "arbitrary"),
        name=name,
    )(x, w, *extras)


def _swap32(y):
    lane = lax.broadcasted_iota(I32, y.shape, 1)
    return jnp.where((lane & 32) != 0, pltpu.roll(y, 32, 1), pltpu.roll(y, 96, 1))


def _ep_q(acc, rows, extra, outs):
    g_ref, cos_ref, sin_ref = extra
    qrot_ref, qpl_ref = outs
    cos, sin = cos_ref[rows, :], sin_ref[rows, :]
    for h in range(acc.shape[1] // NA_HEAD_DIM):
        cols = slice(h * NA_HEAD_DIM, (h + 1) * NA_HEAD_DIM)
        y = _rms(acc[:, cols], g_ref[...])
        qpl_ref[rows, cols] = y.astype(BF16)
        qrot_ref[rows, cols] = (y * cos + _swap32(y) * sin).astype(BF16)


def _ep_k(acc, rows, extra, outs):
    g_ref, cos_ref, sin_ref = extra
    (krot_ref,) = outs
    cos, sin = cos_ref[rows, :], sin_ref[rows, :]
    for h in range(acc.shape[1] // NA_HEAD_DIM):
        cols = slice(h * NA_HEAD_DIM, (h + 1) * NA_HEAD_DIM)
        y = _rms(acc[:, cols], g_ref[...])
        krot_ref[rows, cols] = (y * cos + _swap32(y) * sin).astype(BF16)


def _ep_cast(acc, rows, extra, outs):
    outs[0][rows, :] = acc.astype(BF16)


def _ep_residual(acc, rows, extra, outs):
    x_ref, gate_ref = extra
    outs[0][rows, :] = x_ref[rows, :] + gate_ref[...] * acc


def _ep_bias_residual(acc, rows, extra, outs):
    x_ref, gate_ref, b_ref = extra
    outs[0][rows, :] = x_ref[rows, :] + gate_ref[...] * (acc + b_ref[...])


def _ep_bias_gelu(acc, rows, extra, outs):
    (b_ref,) = extra
    a = acc + b_ref[...]
    outs[0][rows, :] = (0.5 * a * (1.0 + lax.erf(a * (2.0 ** -0.5)))).astype(BF16)


def _rope_tables(seq, extra_rows):
    t = jnp.arange(seq, dtype=I32)
    pos = jnp.stack([t // GRID_W, t % GRID_W], axis=-1).astype(F32)
    inv_freq = ROPE_BASE ** (-jnp.arange(ROPE_FREQS, dtype=F32) / ROPE_FREQS)
    ang = pos[:, :, None] * inv_freq
    cos, sin = jnp.cos(ang), jnp.sin(ang)
    cos = jnp.stack([cos, cos], axis=2).reshape(seq, NA_HEAD_DIM)
    sin = jnp.stack([-sin, sin], axis=2).reshape(seq, NA_HEAD_DIM)
    cos = jnp.concatenate([cos, jnp.ones((extra_rows, NA_HEAD_DIM), F32)], axis=0)
    sin = jnp.concatenate([sin, jnp.zeros((extra_rows, NA_HEAD_DIM), F32)], axis=0)
    return cos, sin


def _qkv(h_all, w_qkv, q_g, k_g, t, seq, d):
    tm, tn = MM_TM, MM_TN
    m_all = h_all.shape[0]
    n_lat, per_seq = t // tm, seq // tm
    cos, sin = _rope_tables(seq, tm)
    g_spec = pl.BlockSpec((1, NA_HEAD_DIM), lambda j, i: (0, 0))
    tab_idx = lambda j, i: (jnp.where(i < n_lat, i % per_seq, per_seq), 0)
    tab_spec = pl.BlockSpec((tm, NA_HEAD_DIM), tab_idx)
    out_spec = pl.BlockSpec((tm, tn), lambda j, i: (i, j))
    q_rot, q_pl = _ws_matmul(
        h_all, w_qkv, 0, d, _ep_q, [q_g.reshape(1, -1), cos, sin], [g_spec, tab_spec, tab_spec],
        [jax.ShapeDtypeStruct((t, d), BF16)] * 2, [out_spec, out_spec], "qkv_q", m=t)
    (k_rot,) = _ws_matmul(
        h_all, w_qkv, d, d, _ep_k, [k_g.reshape(1, -1), cos, sin], [g_spec, tab_spec, tab_spec],
        [jax.ShapeDtypeStruct((m_all, d), BF16)], [out_spec], "qkv_k")
    (v,) = _ws_matmul(
        h_all, w_qkv, 2 * d, d, _ep_cast, [], [],
        [jax.ShapeDtypeStruct((m_all, d), BF16)], [out_spec], "qkv_v")
    return q_rot, q_pl, k_rot, v


def _bias_tables(rpb):
    n_heads, n_dr, _ = rpb.shape
    q = jnp.arange(GRID_W, dtype=I32)[:, None]
    kc = jnp.arange(GRID_W, dtype=I32)[None, :]
    dc = jnp.clip(kc - q + NA_KW - 1, 0, 2 * NA_KW - 2)
    cs = jnp.clip(q - NA_KW // 2, 0, GRID_W - NA_KW)
    in_win = (kc >= cs) & (kc < cs + NA_KW)
    c = jnp.where(in_win[None, None], rpb[:, :, dc], MASK_VALUE)
    c = c.transpose(0, 2, 1, 3).reshape(n_heads, GRID_W, n_dr * GRID_W).astype(F32)
    width = (n_dr + 2) * GRID_W
    c = jnp.pad(c, ((0, 0), (0, 0), (0, width - n_dr * GRID_W)))
    return c[:, :, :width - GRID_W], c[:, :, GRID_W:]


def _attn_kernel(q_ref, qp_ref, k_ref, v_ref, kc_ref, vc_ref, c0_ref, c1_ref, o_ref, *, rows, scale):
    win = NA_KH * GRID_W
    half = NA_KH // 2

    def row_step(r, variant):
        if isinstance(r, int):
            q0, k0 = r * GRID_W, min(max(r - half, 0), rows - NA_KH) * GRID_W
        else:
            q0 = pl.multiple_of(r * GRID_W, GRID_W)
            k0 = pl.multiple_of(jnp.clip(r - half, 0, rows - NA_KH) * GRID_W, GRID_W)
        off = (NA_KH - 1 - variant) * GRID_W
        if off % LANES == 0:
            bias = c0_ref[:, off:off + win]
        else:
            bias = c1_ref[:, off - GRID_W:off - GRID_W + win]
        s_loc = _dot_nt(q_ref[pl.ds(q0, GRID_W), :], k_ref[pl.ds(k0, win), :]) * scale + bias
        s_ctx = _dot_nt(qp_ref[pl.ds(q0, GRID_W), :], kc_ref[...]) * scale
        m = jnp.maximum(jnp.max(s_loc, axis=-1, keepdims=True), jnp.max(s_ctx, axis=-1, keepdims=True))
        p_loc = jnp.exp(s_loc - m)
        p_ctx = jnp.exp(s_ctx - m)
        denom = jnp.sum(p_loc, axis=-1, keepdims=True) + jnp.sum(p_ctx, axis=-1, keepdims=True)
        o = _dot(p_loc.astype(BF16), v_ref[pl.ds(k0, win), :]) + _dot(p_ctx.astype(BF16), vc_ref[...])
        o_ref[pl.ds(q0, GRID_W), :] = (o / denom).astype(BF16)

    for r in range(half):
        row_step(r, r)
    for r in range(rows - half + 1, rows):
        row_step(r, r - (rows - NA_KH))

    def mid(r, carry):
        row_step(r, half)
        return carry

    lax.fori_loop(half, rows - half + 1, mid, 0)


def _attention(q_rot, q_pl, k_rot, v, rpb, batch, seq, ctx_len):
    t, d = q_rot.shape
    n_heads = d // NA_HEAD_DIM
    rows = seq // GRID_W
    c0, c1 = _bias_tables(rpb)
    ctx_blk0 = t // ctx_len
    lat = pl.BlockSpec((seq, NA_HEAD_DIM), lambda h, b: (b, h))
    cx = pl.BlockSpec((ctx_len, NA_HEAD_DIM), lambda h, b: (ctx_blk0 + b, h))
    tab = pl.BlockSpec((None, GRID_W, c0.shape[2]), lambda h, b: (h, 0, 0))
    return pl.pallas_call(
        functools.partial(_attn_kernel, rows=rows, scale=NA_HEAD_DIM ** -0.5),
        grid=(n_heads, batch),
        in_specs=[lat, lat, lat, lat, cx, cx, tab, tab],
        out_specs=lat,
        out_shape=jax.ShapeDtypeStruct((t, d), BF16),
        compiler_params=_cparams("arbitrary", "arbitrary"),
        name="attention",
    )(q_rot, q_pl, k_rot, v, k_rot, v, c0, c1)


def _route(h, wrt_ref, rb_ref, carry_ref, eidx_ref, ew_ref, rank_ref, cnt_ref):
    tm = h.shape[0]
    logits = _dot_nt(wrt_ref[...], h, precision=lax.Precision.HIGHEST)
    scores = jax.nn.sigmoid(logits)
    sel = scores + rb_ref[...]

    def top2(vals):
        def first_max(vs):
            m = functools.reduce(jnp.maximum, vs)
            idx = jnp.full(m.shape, len(vs) - 1, I32)
            for k in range(len(vs) - 2, -1, -1):
                idx = jnp.where(vs[k] == m, k, idx)
            return m, idx
        m1, i1 = first_max(vals)
        m2, i2 = first_max([jnp.where(i1 == k, -jnp.inf, v) for k, v in enumerate(vals)])
        return m1, i1, m2, i2

    grp = []
    for g in range(N_GROUPS):
        vals = [sel[g * EXPERTS_PER_GROUP + k:g * EXPERTS_PER_GROUP + k + 1, :]
                for k in range(EXPERTS_PER_GROUP)]
        grp.append(top2(vals))
    gsum = [m1 + m2 for m1, _, m2, _ in grp]
    gmax = functools.reduce(jnp.maximum, gsum)
    g_idx = jnp.full(gmax.shape, N_GROUPS - 1, I32)
    for g in range(N_GROUPS - 2, -1, -1):
        g_idx = jnp.where(gsum[g] == gmax, g, g_idx)
    i1 = grp[N_GROUPS - 1][1]
    i2 = grp[N_GROUPS - 1][3]
    for g in range(N_GROUPS - 2, -1, -1):
        i1 = jnp.where(g_idx == g, grp[g][1], i1)
        i2 = jnp.where(g_idx == g, grp[g][3], i2)
    e1 = g_idx * EXPERTS_PER_GROUP + i1
    e2 = g_idx * EXPERTS_PER_GROUP + i2
    e_iota = lax.broadcasted_iota(I32, (N_EXPERTS, tm), 0)
    hit1, hit2 = e_iota == e1, e_iota == e2
    s1 = jnp.sum(jnp.where(hit1, scores, 0.0), axis=0, keepdims=True)
    s2 = jnp.sum(jnp.where(hit2, scores, 0.0), axis=0, keepdims=True)
    tot = s1 + s2
    eidx_ref[0:1, :] = e1
    eidx_ref[1:2, :] = e2
    ew_ref[0:1, :] = s1 / tot
    ew_ref[1:2, :] = s2 / tot
    onehot = (hit1 | hit2).astype(BF16)
    upper = (lax.broadcasted_iota(I32, (tm, tm), 0) < lax.broadcasted_iota(I32, (tm, tm), 1)).astype(BF16)
    before = _dot(onehot, upper) + carry_ref[...]
    rank_ref[0:1, :] = jnp.sum(jnp.where(hit1, before, 0.0), axis=0, keepdims=True).astype(I32)
    rank_ref[1:2, :] = jnp.sum(jnp.where(hit2, before, 0.0), axis=0, keepdims=True).astype(I32)
    carry_ref[...] += jnp.sum(onehot.astype(F32), axis=1, keepdims=True)
    cnt_ref[...] = jnp.broadcast_to(carry_ref[...], cnt_ref.shape)


def _prenorm_router_kernel(x_ref, g_ref, sh_ref, sc_ref, wrt_ref, rb_ref,
                           hp_ref, eidx_ref, ew_ref, rank_ref, cnt_ref, carry_ref):
    @pl.when(pl.program_id(0) == 0)
    def _():
        carry_ref[...] = jnp.zeros_like(carry_ref)

    h = _rms(x_ref[...], g_ref[...]) * (1.0 + sc_ref[...]) + sh_ref[...]
    half = h.shape[1] // 2
    hp_ref[...] = _pack_bf16_pair(h[:, :half], h[:, half:])
    _route(h, wrt_ref, rb_ref, carry_ref, eidx_ref, ew_ref, rank_ref, cnt_ref)


def _router_out(t, d, tm):
    shapes = [jax.ShapeDtypeStruct((t, d // 2), U32),
              jax.ShapeDtypeStruct((TOP_K, t), I32),
              jax.ShapeDtypeStruct((TOP_K, t), F32),
              jax.ShapeDtypeStruct((TOP_K, t), I32),
              jax.ShapeDtypeStruct((N_EXPERTS, LANES), F32)]
    specs = [pl.BlockSpec((tm, d // 2), lambda i: (i, 0)),
             pl.BlockSpec((TOP_K, tm), lambda i: (0, i)),
             pl.BlockSpec((TOP_K, tm), lambda i: (0, i)),
             pl.BlockSpec((TOP_K, tm), lambda i: (0, i)),
             pl.BlockSpec((N_EXPERTS, LANES), lambda i: (0, 0))]
    return shapes, specs


def _prenorm_router(x2, g, mods, layer, seq, router_w, router_b):
    t, d = x2.shape
    tm = ROW_TILE
    row = lambda i: (i * tm) // seq
    mspec = lambda chunk: pl.BlockSpec((None, None, None, 1, d), _mod_spec(layer, chunk, row))
    shapes, specs = _router_out(t, d, tm)
    return pl.pallas_call(
        _prenorm_router_kernel,
        grid=(t // tm,),
        in_specs=[pl.BlockSpec((tm, d), lambda i: (i, 0)),
                  pl.BlockSpec((1, d), lambda i: (0, 0)),
                  mspec(3), mspec(4),
                  pl.BlockSpec((N_EXPERTS, d), lambda i: (0, 0)),
                  pl.BlockSpec((N_EXPERTS, 1), lambda i: (0, 0))],
        out_specs=specs,
        out_shape=shapes,
        scratch_shapes=[pltpu.VMEM((N_EXPERTS, 1), F32)],
        compiler_params=_cparams("arbitrary"),
        name="prenorm_router",
    )(x2, g.reshape(1, d), mods, mods, router_w.T, router_b.reshape(N_EXPERTS, 1))


def _routing_plan(eidx, rank, cnt, t):
    n_tiles = (TOP_K * t + N_EXPERTS * (MOE_TM - 1)) // MOE_TM
    n_rows = n_tiles * MOE_TM
    counts = cnt[:, 0].astype(I32)
    padded = ((counts + MOE_TM - 1) // MOE_TM) * MOE_TM
    ends = jnp.cumsum(padded)
    offs = ends - padded
    dest = offs[eidx] + rank
    n_valid = (ends[-1] // MOE_TM).astype(I32)
    tile_start = jnp.arange(n_tiles, dtype=I32) * MOE_TM
    tile_expert = jnp.sum((tile_start[:, None] >= ends[None, :]).astype(I32), axis=1)
    last = jnp.minimum(jnp.maximum(n_valid - 1, 0), n_tiles - 1)
    tile_expert = jnp.where(jnp.arange(n_tiles) < n_valid, tile_expert, tile_expert[last])
    tile_expert = jnp.minimum(tile_expert, N_EXPERTS - 1).astype(I32)
    tok = jnp.tile(jnp.arange(t, dtype=I32), TOP_K)
    src_tok = jnp.zeros((n_rows,), I32).at[dest.reshape(-1)].set(tok)
    return dest, src_tok, tile_expert, n_valid.reshape(1), n_tiles


def _row_copy(src_hbm, row, dst_vmem, slot, sem):
    return pltpu.make_async_copy(src_hbm.at[pl.ds(row, 1), :], dst_vmem.at[pl.ds(slot, 1), :], sem)


def _gather_kernel(nrows_ref, src_ref, hp_hbm, o_ref, buf_ref, sem):
    tg, half = buf_ref.shape
    valid = pl.program_id(0) * tg < nrows_ref[0]

    @pl.when(valid)
    def _():
        def issue(r, carry):
            _row_copy(hp_hbm, src_ref[0, r], buf_ref, r, sem).start()
            return carry

        def wait(r, carry):
            _row_copy(hp_hbm, 0, buf_ref, r, sem).wait()
            return carry

        lax.fori_loop(0, tg, issue, 0)
        lax.fori_loop(0, tg, wait, 0)
        w = buf_ref[...]
        o_ref[:, :half] = _unpack_lo(w).astype(BF16)
        o_ref[:, half:] = _unpack_hi(w).astype(BF16)

    @pl.when(jnp.logical_not(valid))
    def _():
        o_ref[...] = jnp.zeros_like(o_ref)


def _gather_rows(hp, src_tok, n_valid_rows):
    n_rows = src_tok.shape[0]
    half = hp.shape[1]
    tg = GATHER_ROWS
    return pl.pallas_call(
        _gather_kernel,
        grid_spec=pltpu.PrefetchScalarGridSpec(
            num_scalar_prefetch=1,
            grid=(n_rows // tg,),
            in_specs=[pl.BlockSpec((None, 1, tg), lambda i, n: (i, 0, 0), memory_space=pltpu.SMEM),
                      pl.BlockSpec(memory_space=pl.ANY)],
            out_specs=pl.BlockSpec((tg, 2 * half), lambda i, n: (i, 0)),
            scratch_shapes=[pltpu.VMEM((tg, half), U32), pltpu.SemaphoreType.DMA(())]),
        out_shape=jax.ShapeDtypeStruct((n_rows, 2 * half), BF16),
        compiler_params=_cparams("arbitrary"),
        name="moe_gather",
    )(n_valid_rows, src_tok.reshape(n_rows // tg, 1, tg), hp)


def _expert_changed(te_ref):
    i = pl.program_id(1)
    return jnp.logical_or(i == 0, te_ref[i] != te_ref[jnp.maximum(i - 1, 0)])


def _moe_a_kernel(te_ref, nv_ref, x_ref, wg_ref, wu_ref, o_ref, wgb_ref, wub_ref):
    @pl.when(_expert_changed(te_ref))
    def _():
        wgb_ref[...] = wg_ref[...].astype(BF16)
        wub_ref[...] = wu_ref[...].astype(BF16)

    @pl.when(pl.program_id(1) < nv_ref[0])
    def _():
        x = x_ref[...]
        gate = _dot(x, wgb_ref[...])
        up = _dot(x, wub_ref[...])
        o_ref[...] = (gate * jax.nn.sigmoid(gate) * up).astype(BF16)

    @pl.when(pl.program_id(1) >= nv_ref[0])
    def _():
        o_ref[...] = jnp.zeros_like(o_ref)


def _moe_b_kernel(te_ref, nv_ref, a_ref, wd_ref, o_ref, wdb_ref):
    @pl.when(_expert_changed(te_ref))
    def _():
        wdb_ref[...] = wd_ref[...].astype(BF16)

    @pl.when(pl.program_id(1) < nv_ref[0])
    def _():
        y = _dot(a_ref[...], wdb_ref[...])
        half = y.shape[1] // 2
        o_ref[...] = _pack_bf16_pair(y[:, :half], y[:, half:])

    @pl.when(pl.program_id(1) >= nv_ref[0])
    def _():
        o_ref[...] = jnp.zeros_like(o_ref)


def _moe_experts(xs, tile_expert, n_valid, w_gate, w_up, w_down):
    n_rows, d = xs.shape
    f = w_gate.shape[2]
    n_tiles = n_rows // MOE_TM
    row_idx = lambda j, i, te, nv: (jnp.minimum(i, nv[0] - 1), 0)
    a = pl.pallas_call(
        _moe_a_kernel,
        grid_spec=pltpu.PrefetchScalarGridSpec(
            num_scalar_prefetch=2,
            grid=(f // MOE_A_TN, n_tiles),
            in_specs=[pl.BlockSpec((MOE_TM, d), row_idx),
                      pl.BlockSpec((None, d, MOE_A_TN), lambda j, i, te, nv: (te[i], 0, j)),
                      pl.BlockSpec((None, d, MOE_A_TN), lambda j, i, te, nv: (te[i], 0, j))],
            out_specs=pl.BlockSpec((MOE_TM, MOE_A_TN), lambda j, i, te, nv: (i, j)),
            scratch_shapes=[pltpu.VMEM((d, MOE_A_TN), BF16)] * 2),
        out_shape=jax.ShapeDtypeStruct((n_rows, f), BF16),
        compiler_params=_cparams("arbitrary", "arbitrary"),
        name="moe_gate_up",
    )(tile_expert, n_valid, xs, w_gate, w_up)
    ys = pl.pallas_call(
        _moe_b_kernel,
        grid_spec=pltpu.PrefetchScalarGridSpec(
            num_scalar_prefetch=2,
            grid=(d // MOE_B_TN, n_tiles),
            in_specs=[pl.BlockSpec((MOE_TM, f), row_idx),
                      pl.BlockSpec((None, f, MOE_B_TN), lambda j, i, te, nv: (te[i], 0, j))],
            out_specs=pl.BlockSpec((MOE_TM, MOE_B_TN // 2), lambda j, i, te, nv: (i, j)),
            scratch_shapes=[pltpu.VMEM((f, MOE_B_TN), BF16)]),
        out_shape=jax.ShapeDtypeStruct((n_rows, d // 2), U32),
        compiler_params=_cparams("arbitrary", "arbitrary"),
        name="moe_down",
    )(tile_expert, n_valid, a, w_down)
    return ys


def _combine_kernel(dest_ref, ys_hbm, x_ref, ew_ref, gate_ref, *rest, with_norm):
    if with_norm:
        g_ref, sh_ref, sc_ref, xo_ref, ho_ref, buf_ref, sem = rest
    else:
        xo_ref, buf_ref, sem = rest
    tc = x_ref.shape[0]
    n = TOP_K * tc

    def issue(r, carry):
        _row_copy(ys_hbm, dest_ref[0, r], buf_ref, r, sem).start()
        return carry

    def wait(r, carry):
        _row_copy(ys_hbm, 0, buf_ref, r, sem).wait()
        return carry

    lax.fori_loop(0, n, issue, 0)
    lax.fori_loop(0, n, wait, 0)
    w1, w2 = buf_ref[0:tc, :], buf_ref[tc:n, :]
    a1, a2 = ew_ref[:, 0:1], ew_ref[:, 1:2]
    lo = a1 * _unpack_lo(w1) + a2 * _unpack_lo(w2)
    hi = a1 * _unpack_hi(w1) + a2 * _unpack_hi(w2)
    q = MOE_B_TN // 2
    for j in range(lo.shape[1] // q):
        for part, src in ((0, lo), (1, hi)):
            cols = slice(j * 2 * q + part * q, j * 2 * q + (part + 1) * q)
            xo_ref[:, cols] = x_ref[:, cols] + gate_ref[:, cols] * src[:, j * q:(j + 1) * q]
    if with_norm:
        ho_ref[...] = (_rms(xo_ref[...], g_ref[...]) * (1.0 + sc_ref[...]) + sh_ref[...]).astype(BF16)


def _combine(x2, ys, dest, ew, mods, layer, seq, norm=None):
    t, d = x2.shape
    tc = COMBINE_ROWS
    n_t = t // tc
    dest_tiles = dest.reshape(TOP_K, n_t, tc).transpose(1, 0, 2).reshape(n_t, 1, TOP_K * tc)
    row = lambda i: (i * tc) // seq
    mspec = lambda lay, chunk: pl.BlockSpec((None, None, None, 1, d), _mod_spec(lay, chunk, row))
    in_specs = [pl.BlockSpec((None, 1, TOP_K * tc), lambda i: (i, 0, 0), memory_space=pltpu.SMEM),
                pl.BlockSpec(memory_space=pl.ANY),
                pl.BlockSpec((tc, d), lambda i: (i, 0)),
                pl.BlockSpec((tc, TOP_K), lambda i: (i, 0)),
                mspec(layer, 5)]
    args = [dest_tiles, ys, x2, ew.T, mods]
    out_shapes = [jax.ShapeDtypeStruct((t, d), F32)]
    out_specs = [pl.BlockSpec((tc, d), lambda i: (i, 0))]
    if norm is not None:
        next_layer, g = norm
        in_specs += [pl.BlockSpec((1, d), lambda i: (0, 0)), mspec(next_layer, 0), mspec(next_layer, 1)]
        args += [g.reshape(1, d), mods, mods]
        out_shapes.append(jax.ShapeDtypeStruct((t, d), BF16))
        out_specs.append(pl.BlockSpec((tc, d), lambda i: (i, 0)))
    return pl.pallas_call(
        functools.partial(_combine_kernel, with_norm=norm is not None),
        grid=(n_t,),
        in_specs=in_specs,
        out_specs=out_specs,
        out_shape=out_shapes,
        scratch_shapes=[pltpu.VMEM((TOP_K * tc, d // 2), U32), pltpu.SemaphoreType.DMA(())],
        compiler_params=_cparams("arbitrary"),
        name="moe_combine",
    )(*args)


def _moe_layer(x2, norm_g, mods, layer, seq, router_w, router_b, w_gate, w_up, w_down, next_norm):
    t = x2.shape[0]
    hp, eidx, ew, rank, cnt = _prenorm_router(x2, norm_g, mods, layer, seq, router_w, router_b)
    dest, src_tok, tile_expert, n_valid, _ = _routing_plan(eidx, rank, cnt, t)
    xs = _gather_rows(hp, src_tok, n_valid * MOE_TM)
    ys = _moe_experts(xs, tile_expert, n_valid, w_gate, w_up, w_down)
    return _combine(x2, ys, dest, ew, mods, layer, seq, next_norm)


def _sgu_kernel(z_ref, ws_ref, bs_ref, g_ref, b_ref, o_ref):
    width = o_ref.shape[1]
    gdim = width // SGU_GROUPS
    for c in range(z_ref.shape[0] // SGU_CHUNK):
        rows = slice(c * SGU_CHUNK, (c + 1) * SGU_CHUNK)
        v = z_ref[rows, width:].astype(F32)
        mu = jnp.mean(v, axis=-1, keepdims=True)
        vc = v - mu
        var = jnp.mean(vc * vc, axis=-1, keepdims=True)
        vn = (vc * lax.rsqrt(var + NORM_EPS) * g_ref[...] + b_ref[...]).astype(BF16)
        for g in range(SGU_GROUPS):
            cols = slice(g * gdim, (g + 1) * gdim)
            s = _dot(ws_ref[g].astype(BF16), vn[:, cols]) + bs_ref[:, g:g + 1]
            o_ref[rows, cols] = (z_ref[rows, cols].astype(F32) * s).astype(BF16)


def _sgu_gate(z, w_s, b_s, ln_g, ln_b):
    t, two_w = z.shape
    width = two_w // 2
    tm = ROW_TILE
    return pl.pallas_call(
        _sgu_kernel,
        grid=(t // tm,),
        in_specs=[pl.BlockSpec((tm, two_w), lambda i: (i, 0)),
                  pl.BlockSpec((SGU_GROUPS, SGU_CHUNK, SGU_CHUNK), lambda i: (0, 0, 0)),
                  pl.BlockSpec((SGU_CHUNK, SGU_GROUPS), lambda i: (0, 0)),
                  pl.BlockSpec((1, width), lambda i: (0, 0)),
                  pl.BlockSpec((1, width), lambda i: (0, 0))],
        out_specs=pl.BlockSpec((tm, width), lambda i: (i, 0)),
        out_shape=jax.ShapeDtypeStruct((t, width), BF16),
        compiler_params=_cparams("arbitrary"),
        name="sgu_gate",
    )(z, w_s, b_s.T, ln_g.reshape(1, width), ln_b.reshape(1, width))


def kernel(x, c, ctx, c_ctx, ada_w, ada_b, norm1_g, norm2_g, na_w_qkv, na_q_g, na_k_g, na_rpb, na_w_o,
           sgu_w_uv, sgu_b_uv, sgu_ln_g, sgu_ln_b, sgu_w_s, sgu_b_s, sgu_w_out, sgu_b_out,
           router_w, router_b, moe_w_gate, moe_w_up, moe_w_down):
    batch, seq, d = x.shape
    ctx_len = ctx.shape[1]
    t = batch * seq
    x2 = x.reshape(t, d)
    ctx2 = ctx.reshape(batch * ctx_len, d)
    tm, tn = MM_TM, MM_TN

    mod_rows = 16
    cpad = jnp.concatenate([c, c_ctx[None, :], jnp.zeros((mod_rows - batch - 1, d), F32)], axis=0)
    mods = _ada_mods(cpad, ada_w, ada_b).reshape(ada_w.shape[0], mod_rows, 6, 1, d)
    gate_spec = lambda layer, chunk: pl.BlockSpec(
        (None, None, None, 1, tn), lambda j, i: (layer, (i * tm) // seq, chunk, 0, j))
    res_spec = pl.BlockSpec((tm, tn), lambda j, i: (i, j))
    bias_spec = pl.BlockSpec((1, tn), lambda j, i: (0, j))

    h_all = _prenorm0(x2, ctx2, norm1_g[0], mods, 0, seq, batch)
    q_rot, q_pl, k_rot, v = _qkv(h_all, na_w_qkv[0], na_q_g[0], na_k_g[0], t, seq, d)
    att = _attention(q_rot, q_pl, k_rot, v, na_rpb[0], batch, seq, ctx_len)
    (x2,) = _ws_matmul(att, na_w_o[0], 0, d, _ep_residual, [x2, mods], [res_spec, gate_spec(0, 2)],
                       [jax.ShapeDtypeStruct((t, d), F32)], [res_spec], "attn_out")
    x2, h = _moe_layer(x2, norm2_g[0], mods, 0, seq, router_w, router_b,
                       moe_w_gate[0], moe_w_up[0], moe_w_down[0], (1, norm1_g[1]))

    width = sgu_w_uv.shape[2] // 2
    (z,) = _ws_matmul(h, sgu_w_uv[0], 0, 2 * width, _ep_bias_gelu, [sgu_b_uv[0].reshape(1, -1)], [bias_spec],
                      [jax.ShapeDtypeStruct((t, 2 * width), BF16)], [res_spec], "sgu_uv")
    gated = _sgu_gate(z, sgu_w_s[0], sgu_b_s[0], sgu_ln_g[0], sgu_ln_b[0])
    (x2,) = _ws_matmul(gated, sgu_w_out[0], 0, d, _ep_bias_residual,
                       [x2, mods, sgu_b_out[0].reshape(1, -1)], [res_spec, gate_spec(1, 2), bias_spec],
                       [jax.ShapeDtypeStruct((t, d), F32)], [res_spec], "sgu_out")
    (x2,) = _moe_layer(x2, norm2_g[1], mods, 1, seq, router_w, router_b,
                       moe_w_gate[1], moe_w_up[1], moe_w_down[1], None)
    return x2.reshape(batch, seq, d)
```

```python
import functools

import jax
import jax.numpy as jnp
from jax import lax
from jax.experimental import pallas as pl
from jax.experimental.pallas import tpu as pltpu

F32, BF16, I32, U32 = jnp.float32, jnp.bfloat16, jnp.int32, jnp.uint32

GRID_W = 64
NORM_EPS = 1e-6
NA_HEAD_DIM = 128
NA_KH = 8
NA_KW = 16
NA_SCALE = NA_HEAD_DIM ** -0.5
ROPE_BASE = 10000.0
ROPE_FREQS = NA_HEAD_DIM // 4
SGU_CHUNK = 128
SGU_GROUPS = 16
N_EXPERTS = 16
N_GROUPS = 4
EXPERTS_PER_GROUP = N_EXPERTS // N_GROUPS
TOP_K = 2
MASK_VALUE = -1e30

LANES = 128
VMEM_LIMIT_BYTES = 56 * 1024 * 1024
MM_TM = 1024
MM_TN = 512
MM_RC = 256
ROW_TILE = 512
MOE_TM = 512
MOE_A_TN = 512
MOE_B_TN = 4096
GATHER_ROWS = 512
COMBINE_ROWS = 256
COMBINE_CHUNK_ROWS = 16
COMBINE_CHUNK_COLS = 512
NORM_CHUNK_ROWS = 16
NORM_CHUNK_COLS = 512
DMA_CHUNK_ROWS = 16
ATTN_CTX_CHUNK = 256
ATTN_DEPTH = 4


def _cparams(*sem):
    return pltpu.CompilerParams(dimension_semantics=sem, vmem_limit_bytes=VMEM_LIMIT_BYTES)


def _dot(a, b):
    return jnp.dot(a, b, preferred_element_type=F32)


def _dot_nt(a, b, precision=None):
    return lax.dot_general(a, b, (((1,), (1,)), ((), ())), precision=precision,
                           preferred_element_type=F32)


def _rms(x, g):
    return x * lax.rsqrt(jnp.mean(x * x, axis=-1, keepdims=True) + NORM_EPS) * g


def _pack_bf16_pair(lo, hi):
    lo_b = lax.bitcast_convert_type(lo.astype(BF16).astype(F32), U32) >> 16
    hi_b = lax.bitcast_convert_type(hi.astype(BF16).astype(F32), U32) & jnp.uint32(0xFFFF0000)
    return lo_b | hi_b


def _unpack_lo(w):
    return lax.bitcast_convert_type(w << 16, F32)


def _unpack_hi(w):
    return lax.bitcast_convert_type(w & jnp.uint32(0xFFFF0000), F32)


def _ada_kernel(c_ref, w_ref, b_ref, o_ref):
    c = c_ref[...]
    a = (c * jax.nn.sigmoid(c)).astype(BF16)
    o_ref[...] = _dot(a, w_ref[...].astype(BF16)) + b_ref[...]


def _ada_mods(cpad, ada_w, ada_b):
    n_layers, d, n = ada_w.shape
    rows = cpad.shape[0]
    tn = MM_TN
    return pl.pallas_call(
        _ada_kernel,
        grid=(n_layers, n // tn),
        in_specs=[pl.BlockSpec((rows, d), lambda l, j: (0, 0)),
                  pl.BlockSpec((None, d, tn), lambda l, j: (l, 0, j)),
                  pl.BlockSpec((None, 1, tn), lambda l, j: (l, 0, j))],
        out_specs=pl.BlockSpec((None, rows, tn), lambda l, j: (l, 0, j)),
        out_shape=jax.ShapeDtypeStruct((n_layers, rows, n), F32),
        compiler_params=_cparams("arbitrary", "arbitrary"),
        name="ada_mods",
    )(cpad, ada_w, ada_b.reshape(n_layers, 1, n))


def _mod_spec(layer, chunk, row_fn):
    def idx(*g):
        return (layer, row_fn(*g), chunk, 0, 0)
    return idx


def _prenorm0_kernel(x_ref, ctx_ref, g_ref, sh_ref, sc_ref, o_ref, *, n_lat):
    i = pl.program_id(0)

    def emit(v):
        o_ref[...] = (_rms(v, g_ref[...]) * (1.0 + sc_ref[...]) + sh_ref[...]).astype(BF16)

    @pl.when(i < n_lat)
    def _():
        emit(x_ref[...])

    @pl.when(i >= n_lat)
    def _():
        emit(ctx_ref[...])


def _prenorm0(x2, ctx2, g, mods, layer, seq, ctx_row):
    t, d = x2.shape
    tc = ctx2.shape[0]
    tm = ROW_TILE
    n_lat, n_ctx = t // tm, tc // tm
    row = lambda i: jnp.where(i < n_lat, (i * tm) // seq, ctx_row)
    mspec = lambda chunk: pl.BlockSpec((None, None, None, 1, d), _mod_spec(layer, chunk, row))
    return pl.pallas_call(
        functools.partial(_prenorm0_kernel, n_lat=n_lat),
        grid=(n_lat + n_ctx,),
        in_specs=[pl.BlockSpec((tm, d), lambda i: (jnp.minimum(i, n_lat - 1), 0)),
                  pl.BlockSpec((tm, d), lambda i: (jnp.maximum(i - n_lat, 0), 0)),
                  pl.BlockSpec((1, d), lambda i: (0, 0)),
                  mspec(0), mspec(1)],
        out_specs=pl.BlockSpec((tm, d), lambda i: (i, 0)),
        out_shape=jax.ShapeDtypeStruct((t + tc, d), BF16),
        compiler_params=_cparams("arbitrary"),
        name="prenorm0",
    )(x2, ctx2, g.reshape(1, d), mods, mods)


def _ws_body(x_ref, w_ref, *refs, n_extra, n_out, epilogue, tm, rc):
    extra, outs, wb_ref = refs[:n_extra], refs[n_extra:n_extra + n_out], refs[-1]

    @pl.when(pl.program_id(1) == 0)
    def _():
        wb_ref[...] = w_ref[...].astype(BF16)

    for c in range(tm // rc):
        rows = slice(c * rc, (c + 1) * rc)
        epilogue(_dot(x_ref[rows, :], wb_ref[...]), rows, extra, outs)


def _ws_matmul(x, w, col0, n_cols, epilogue, extras, extra_specs, out_shapes, out_specs, name,
               tm=MM_TM, tn=MM_TN, m=None):
    k = x.shape[1]
    m = x.shape[0] if m is None else m
    assert m % tm == 0 and n_cols % tn == 0 and col0 % tn == 0
    jb = col0 // tn
    body = functools.partial(_ws_body, n_extra=len(extras), n_out=len(out_shapes),
                             epilogue=epilogue, tm=tm, rc=MM_RC)
    return pl.pallas_call(
        body,
        grid=(n_cols // tn, m // tm),
        in_specs=[pl.BlockSpec((tm, k), lambda j, i: (i, 0)),
                  pl.BlockSpec((k, tn), lambda j, i: (0, j + jb))] + list(extra_specs),
        out_specs=out_specs,
        out_shape=out_shapes,
        scratch_shapes=[pltpu.VMEM((k, tn), BF16)],
        compiler_params=_cparams("arbitrary", "arbitrary"),
        name=name,
    )(x, w, *extras)


def _swap32(y):
    lane = lax.broadcasted_iota(I32, y.shape, 1)
    return jnp.where((lane & 32) != 0, pltpu.roll(y, 32, 1), pltpu.roll(y, 96, 1))


def _ep_q(acc, rows, extra, outs):
    g_ref, cos_ref, sin_ref = extra
    qrot_ref, qpl_ref = outs
    cos, sin = cos_ref[rows, :], sin_ref[rows, :]
    for h in range(acc.shape[1] // NA_HEAD_DIM):
        cols = slice(h * NA_HEAD_DIM, (h + 1) * NA_HEAD_DIM)
        y = _rms(acc[:, cols], g_ref[...]) * NA_SCALE
        qpl_ref[rows, cols] = y.astype(BF16)
        qrot_ref[rows, cols] = (y * cos + _swap32(y) * sin).astype(BF16)


def _ep_k(acc, rows, extra, outs):
    g_ref, cos_ref, sin_ref = extra
    (krot_ref,) = outs
    cos, sin = cos_ref[rows, :], sin_ref[rows, :]
    for h in range(acc.shape[1] // NA_HEAD_DIM):
        cols = slice(h * NA_HEAD_DIM, (h + 1) * NA_HEAD_DIM)
        y = _rms(acc[:, cols], g_ref[...])
        krot_ref[rows, cols] = (y * cos + _swap32(y) * sin).astype(BF16)


def _ep_cast(acc, rows, extra, outs):
    outs[0][rows, :] = acc.astype(BF16)


def _ep_residual(acc, rows, extra, outs):
    x_ref, gate_ref = extra
    outs[0][rows, :] = x_ref[rows, :] + gate_ref[...] * acc


def _ep_bias_residual(acc, rows, extra, outs):
    x_ref, gate_ref, b_ref = extra
    outs[0][rows, :] = x_ref[rows, :] + gate_ref[...] * (acc + b_ref[...])


def _ep_bias_gelu(acc, rows, extra, outs):
    (b_ref,) = extra
    a = acc + b_ref[...]
    outs[0][rows, :] = (0.5 * a * (1.0 + lax.erf(a * (2.0 ** -0.5)))).astype(BF16)


def _rope_tables(seq, extra_rows):
    t = jnp.arange(seq, dtype=I32)
    pos = jnp.stack([t // GRID_W, t % GRID_W], axis=-1).astype(F32)
    inv_freq = ROPE_BASE ** (-jnp.arange(ROPE_FREQS, dtype=F32) / ROPE_FREQS)
    ang = pos[:, :, None] * inv_freq
    cos, sin = jnp.cos(ang), jnp.sin(ang)
    cos = jnp.stack([cos, cos], axis=2).reshape(seq, NA_HEAD_DIM)
    sin = jnp.stack([-sin, sin], axis=2).reshape(seq, NA_HEAD_DIM)
    cos = jnp.concatenate([cos, jnp.ones((extra_rows, NA_HEAD_DIM), F32)], axis=0)
    sin = jnp.concatenate([sin, jnp.zeros((extra_rows, NA_HEAD_DIM), F32)], axis=0)
    return cos, sin


def _qkv(h_all, w_qkv, q_g, k_g, t, seq, d):
    tm, tn = MM_TM, MM_TN
    m_all = h_all.shape[0]
    n_lat, per_seq = t // tm, seq // tm
    cos, sin = _rope_tables(seq, tm)
    g_spec = pl.BlockSpec((1, NA_HEAD_DIM), lambda j, i: (0, 0))
    tab_idx = lambda j, i: (jnp.where(i < n_lat, i % per_seq, per_seq), 0)
    tab_spec = pl.BlockSpec((tm, NA_HEAD_DIM), tab_idx)
    out_spec = pl.BlockSpec((tm, tn), lambda j, i: (i, j))
    q_rot, q_pl = _ws_matmul(
        h_all, w_qkv, 0, d, _ep_q, [q_g.reshape(1, -1), cos, sin], [g_spec, tab_spec, tab_spec],
        [jax.ShapeDtypeStruct((t, d), BF16)] * 2, [out_spec, out_spec], "qkv_q", m=t)
    (k_rot,) = _ws_matmul(
        h_all, w_qkv, d, d, _ep_k, [k_g.reshape(1, -1), cos, sin], [g_spec, tab_spec, tab_spec],
        [jax.ShapeDtypeStruct((m_all, d), BF16)], [out_spec], "qkv_k")
    (v,) = _ws_matmul(
        h_all, w_qkv, 2 * d, d, _ep_cast, [], [],
        [jax.ShapeDtypeStruct((m_all, d), BF16)], [out_spec], "qkv_v")
    return q_rot, q_pl, k_rot, v


def _bias_tables(rpb):
    n_heads, n_dr, _ = rpb.shape
    q = jnp.arange(GRID_W, dtype=I32)[:, None]
    kc = jnp.arange(GRID_W, dtype=I32)[None, :]
    dc = jnp.clip(kc - q + NA_KW - 1, 0, 2 * NA_KW - 2)
    cs = jnp.clip(q - NA_KW // 2, 0, GRID_W - NA_KW)
    in_win = (kc >= cs) & (kc < cs + NA_KW)
    c = jnp.where(in_win[None, None], rpb[:, :, dc], MASK_VALUE)
    c = c.transpose(0, 2, 1, 3).reshape(n_heads, GRID_W, n_dr * GRID_W).astype(F32)
    width = (n_dr + 2) * GRID_W
    c = jnp.pad(c, ((0, 0), (0, 0), (0, width - n_dr * GRID_W)))
    return c[:, :, :width - GRID_W], c[:, :, GRID_W:]


def _pipelined(n, start, finish, depth):
    pending = [start(i) for i in range(min(depth, n))]
    for i in range(n):
        s = pending.pop(0)
        if i + depth < n:
            pending.append(start(i + depth))
        finish(i, s)


def _attn_kernel(q_ref, qp_ref, k_ref, v_ref, kc_ref, vc_ref, c0_ref, c1_ref, o_ref,
                 mc_ref, lc_ref, oc_ref, *, rows):
    win = NA_KH * GRID_W
    half = NA_KH // 2
    seq = rows * GRID_W
    hd = q_ref.shape[1]

    def ctx_scores(c):
        return _dot_nt(qp_ref[c * ATTN_CTX_CHUNK:(c + 1) * ATTN_CTX_CHUNK, :], kc_ref[...])

    def ctx_finish(c, s):
        sl = slice(c * ATTN_CTX_CHUNK, (c + 1) * ATTN_CTX_CHUNK)
        m = jnp.max(s, axis=-1, keepdims=True)
        p = jnp.exp(s - m)
        mc_ref[sl, :] = jnp.broadcast_to(m, (ATTN_CTX_CHUNK, hd))
        lc_ref[sl, :] = jnp.broadcast_to(jnp.sum(p, axis=-1, keepdims=True), (ATTN_CTX_CHUNK, hd))
        oc_ref[sl, :] = _dot(p.astype(BF16), vc_ref[...])

    _pipelined(seq // ATTN_CTX_CHUNK, ctx_scores, ctx_finish, 2)

    def key_start(r):
        return min(max(r - half, 0), rows - NA_KH)

    def loc_scores(r):
        k0 = key_start(r) * GRID_W
        off = (NA_KH - 1 - (r - key_start(r))) * GRID_W
        if off % LANES == 0:
            bias = c0_ref[:, off:off + win]
        else:
            bias = c1_ref[:, off - GRID_W:off - GRID_W + win]
        return _dot_nt(q_ref[r * GRID_W:(r + 1) * GRID_W, :], k_ref[k0:k0 + win, :]) + bias

    def loc_finish(r, s):
        sl = slice(r * GRID_W, (r + 1) * GRID_W)
        k0 = key_start(r) * GRID_W
        m_loc = jnp.max(s, axis=-1, keepdims=True)
        p = jnp.exp(s - m_loc)
        l_loc = jnp.sum(p, axis=-1, keepdims=True)
        o_loc = _dot(p.astype(BF16), v_ref[k0:k0 + win, :])
        m_ctx = mc_ref[sl, :]
        m = jnp.maximum(m_loc, m_ctx)
        a = jnp.exp(m_loc - m)
        b = jnp.exp(m_ctx - m)
        denom = a * l_loc + b * lc_ref[sl, :]
        o_ref[sl, :] = ((a * o_loc + b * oc_ref[sl, :]) / denom).astype(BF16)

    _pipelined(rows, loc_scores, loc_finish, ATTN_DEPTH)


def _attention(q_rot, q_pl, k_rot, v, rpb, batch, seq, ctx_len):
    t, d = q_rot.shape
    n_heads = d // NA_HEAD_DIM
    rows = seq // GRID_W
    c0, c1 = _bias_tables(rpb)
    ctx_blk0 = t // ctx_len
    lat = pl.BlockSpec((seq, NA_HEAD_DIM), lambda h, b: (b, h))
    cx = pl.BlockSpec((ctx_len, NA_HEAD_DIM), lambda h, b: (ctx_blk0 + b, h))
    tab = pl.BlockSpec((None, GRID_W, c0.shape[2]), lambda h, b: (h, 0, 0))
    return pl.pallas_call(
        functools.partial(_attn_kernel, rows=rows),
        grid=(n_heads, batch),
        in_specs=[lat, lat, lat, lat, cx, cx, tab, tab],
        out_specs=lat,
        out_shape=jax.ShapeDtypeStruct((t, d), BF16),
        scratch_shapes=[pltpu.VMEM((seq, NA_HEAD_DIM), F32)] * 3,
        compiler_params=_cparams("arbitrary", "arbitrary"),
        name="attention",
    )(q_rot, q_pl, k_rot, v, k_rot, v, c0, c1)


def _router_logits(wrt_ref, h1_ref, h2_ref):
    w = wrt_ref[...]
    w1 = w.astype(BF16)
    w2 = (w - w1.astype(F32)).astype(BF16)
    a = _dot_nt(jnp.concatenate([w1, w2], axis=0), h1_ref[...])
    b = _dot_nt(w1, h2_ref[...])
    return a[:N_EXPERTS] + (a[N_EXPERTS:] + b)


def _route(logits, rb_ref, carry_ref, eidx_ref, ew_ref, rank_ref, cnt_ref):
    tm = logits.shape[1]
    scores = jax.nn.sigmoid(logits)
    sel = scores + rb_ref[...]

    def top2(vals):
        def first_max(vs):
            m = functools.reduce(jnp.maximum, vs)
            idx = jnp.full(m.shape, len(vs) - 1, I32)
            for k in range(len(vs) - 2, -1, -1):
                idx = jnp.where(vs[k] == m, k, idx)
            return m, idx
        m1, i1 = first_max(vals)
        m2, i2 = first_max([jnp.where(i1 == k, -jnp.inf, v) for k, v in enumerate(vals)])
        return m1, i1, m2, i2

    grp = []
    for g in range(N_GROUPS):
        vals = [sel[g * EXPERTS_PER_GROUP + k:g * EXPERTS_PER_GROUP + k + 1, :]
                for k in range(EXPERTS_PER_GROUP)]
        grp.append(top2(vals))
    gsum = [m1 + m2 for m1, _, m2, _ in grp]
    gmax = functools.reduce(jnp.maximum, gsum)
    g_idx = jnp.full(gmax.shape, N_GROUPS - 1, I32)
    for g in range(N_GROUPS - 2, -1, -1):
        g_idx = jnp.where(gsum[g] == gmax, g, g_idx)
    i1 = grp[N_GROUPS - 1][1]
    i2 = grp[N_GROUPS - 1][3]
    for g in range(N_GROUPS - 2, -1, -1):
        i1 = jnp.where(g_idx == g, grp[g][1], i1)
        i2 = jnp.where(g_idx == g, grp[g][3], i2)
    e1 = g_idx * EXPERTS_PER_GROUP + i1
    e2 = g_idx * EXPERTS_PER_GROUP + i2
    e_iota = lax.broadcasted_iota(I32, (N_EXPERTS, tm), 0)
    hit1, hit2 = e_iota == e1, e_iota == e2
    s1 = jnp.sum(jnp.where(hit1, scores, 0.0), axis=0, keepdims=True)
    s2 = jnp.sum(jnp.where(hit2, scores, 0.0), axis=0, keepdims=True)
    tot = s1 + s2
    eidx_ref[0:1, :] = e1
    eidx_ref[1:2, :] = e2
    ew_ref[0:1, :] = s1 / tot
    ew_ref[1:2, :] = s2 / tot
    onehot = (hit1 | hit2).astype(BF16)
    upper = (lax.broadcasted_iota(I32, (tm, tm), 0) < lax.broadcasted_iota(I32, (tm, tm), 1)).astype(BF16)
    before = _dot(onehot, upper) + carry_ref[...]
    rank_ref[0:1, :] = jnp.sum(jnp.where(hit1, before, 0.0), axis=0, keepdims=True).astype(I32)
    rank_ref[1:2, :] = jnp.sum(jnp.where(hit2, before, 0.0), axis=0, keepdims=True).astype(I32)
    carry_ref[...] += jnp.sum(onehot.astype(F32), axis=1, keepdims=True)
    cnt_ref[...] = jnp.broadcast_to(carry_ref[...], cnt_ref.shape)


def _prenorm_router_kernel(x_ref, g_ref, sh_ref, sc_ref, wrt_ref, rb_ref,
                           hp_ref, eidx_ref, ew_ref, rank_ref, cnt_ref, carry_ref, h1_ref, h2_ref):
    @pl.when(pl.program_id(0) == 0)
    def _():
        carry_ref[...] = jnp.zeros_like(carry_ref)

    tm, d = x_ref.shape
    half = d // 2
    rc, cw = NORM_CHUNK_ROWS, NORM_CHUNK_COLS

    def chunk(ci, carry):
        rows = pl.ds(pl.multiple_of(ci * rc, rc), rc)
        ss = jnp.zeros((rc, cw), F32)
        for cc in range(d // cw):
            xv = x_ref[rows, cc * cw:(cc + 1) * cw]
            ss = ss + xv * xv
        inv = lax.rsqrt(jnp.sum(ss, axis=-1, keepdims=True) / d + NORM_EPS)

        def modulated(cols):
            h = x_ref[rows, cols] * inv * g_ref[:, cols] * (1.0 + sc_ref[:, cols]) + sh_ref[:, cols]
            h1 = h.astype(BF16)
            h1_ref[rows, cols] = h1
            h2_ref[rows, cols] = (h - h1.astype(F32)).astype(BF16)
            return h

        for cc in range(half // cw):
            lo = modulated(slice(cc * cw, (cc + 1) * cw))
            hi = modulated(slice(half + cc * cw, half + (cc + 1) * cw))
            hp_ref[rows, cc * cw:(cc + 1) * cw] = _pack_bf16_pair(lo, hi)
        return carry

    lax.fori_loop(0, tm // rc, chunk, 0)
    _route(_router_logits(wrt_ref, h1_ref, h2_ref), rb_ref, carry_ref, eidx_ref, ew_ref, rank_ref, cnt_ref)


def _router_out(t, d, tm):
    shapes = [jax.ShapeDtypeStruct((t, d // 2), U32),
              jax.ShapeDtypeStruct((TOP_K, t), I32),
              jax.ShapeDtypeStruct((TOP_K, t), F32),
              jax.ShapeDtypeStruct((TOP_K, t), I32),
              jax.ShapeDtypeStruct((N_EXPERTS, LANES), F32)]
    specs = [pl.BlockSpec((tm, d // 2), lambda i: (i, 0)),
             pl.BlockSpec((TOP_K, tm), lambda i: (0, i)),
             pl.BlockSpec((TOP_K, tm), lambda i: (0, i)),
             pl.BlockSpec((TOP_K, tm), lambda i: (0, i)),
             pl.BlockSpec((N_EXPERTS, LANES), lambda i: (0, 0))]
    return shapes, specs


def _prenorm_router(x2, g, mods, layer, seq, router_w, router_b):
    t, d = x2.shape
    tm = ROW_TILE
    row = lambda i: (i * tm) // seq
    mspec = lambda chunk: pl.BlockSpec((None, None, None, 1, d), _mod_spec(layer, chunk, row))
    shapes, specs = _router_out(t, d, tm)
    return pl.pallas_call(
        _prenorm_router_kernel,
        grid=(t // tm,),
        in_specs=[pl.BlockSpec((tm, d), lambda i: (i, 0)),
                  pl.BlockSpec((1, d), lambda i: (0, 0)),
                  mspec(3), mspec(4),
                  pl.BlockSpec((N_EXPERTS, d), lambda i: (0, 0)),
                  pl.BlockSpec((N_EXPERTS, 1), lambda i: (0, 0))],
        out_specs=specs,
        out_shape=shapes,
        scratch_shapes=[pltpu.VMEM((N_EXPERTS, 1), F32), pltpu.VMEM((tm, d), BF16), pltpu.VMEM((tm, d), BF16)],
        compiler_params=_cparams("arbitrary"),
        name="prenorm_router",
    )(x2, g.reshape(1, d), mods, mods, router_w.T, router_b.reshape(N_EXPERTS, 1))


def _routing_plan(eidx, rank, cnt, t):
    n_tiles = (TOP_K * t + N_EXPERTS * (MOE_TM - 1)) // MOE_TM
    n_rows = n_tiles * MOE_TM
    counts = cnt[:, 0].astype(I32)
    padded = ((counts + MOE_TM - 1) // MOE_TM) * MOE_TM
    ends = jnp.cumsum(padded)
    offs = ends - padded
    hit = eidx[None] == jnp.arange(N_EXPERTS, dtype=I32)[:, None, None]
    dest = jnp.sum(jnp.where(hit, offs[:, None, None], 0), axis=0) + rank
    n_valid = (ends[-1] // MOE_TM).astype(I32)
    tile_start = jnp.arange(n_tiles, dtype=I32) * MOE_TM
    tile_expert = jnp.sum((tile_start[:, None] >= ends[None, :]).astype(I32), axis=1)
    last = jnp.minimum(jnp.maximum(n_valid - 1, 0), n_tiles - 1)
    tile_expert = jnp.where(jnp.arange(n_tiles) < n_valid, tile_expert, tile_expert[last])
    tile_expert = jnp.minimum(tile_expert, N_EXPERTS - 1).astype(I32)
    tok = jnp.tile(jnp.arange(t, dtype=I32), TOP_K)
    src_tok = jnp.zeros((n_rows,), I32).at[dest.reshape(-1)].set(tok)
    return dest, src_tok, tile_expert, n_valid.reshape(1), n_tiles


def _row_copy(src_hbm, row, dst_vmem, slot, sem):
    return pltpu.make_async_copy(src_hbm.at[pl.ds(row, 1), :], dst_vmem.at[pl.ds(slot, 1), :], sem)


def _all_rows_wait(src_hbm, dst_vmem, sem):
    pltpu.make_async_copy(src_hbm.at[pl.ds(0, dst_vmem.shape[0]), :], dst_vmem, sem).wait()


def _issue_rows(src_hbm, idx_ref, r0, count, dst_vmem, sem):
    for k in range(count):
        _row_copy(src_hbm, idx_ref[0, r0 + k], dst_vmem, r0 + k, sem).start()


def _gather_kernel(nrows_ref, src0_ref, srcn_ref, hp_hbm, o_ref, buf_ref, sem):
    i = pl.program_id(0)
    _, tg, half = buf_ref.shape
    rc = DMA_CHUNK_ROWS
    slot = i % 2
    cur_valid = i * tg < nrows_ref[0]
    nxt_valid = (i + 1) * tg < nrows_ref[0]

    @pl.when(i == 0)
    def _():
        def body(c, carry):
            _issue_rows(hp_hbm, src0_ref, pl.multiple_of(c * rc, rc), rc, buf_ref.at[0], sem.at[0])
            return carry
        lax.fori_loop(0, tg // rc, body, 0)

    @pl.when(cur_valid)
    def _():
        _all_rows_wait(hp_hbm, buf_ref.at[slot], sem.at[slot])

    def unpack(r0):
        rows = pl.ds(r0, rc)
        w = buf_ref[slot, rows, :]
        o_ref[rows, :half] = _unpack_lo(w).astype(BF16)
        o_ref[rows, half:] = _unpack_hi(w).astype(BF16)

    @pl.when(jnp.logical_and(cur_valid, nxt_valid))
    def _():
        def body(c, carry):
            r0 = pl.multiple_of(c * rc, rc)
            _issue_rows(hp_hbm, srcn_ref, r0, rc, buf_ref.at[1 - slot], sem.at[1 - slot])
            unpack(r0)
            return carry
        lax.fori_loop(0, tg // rc, body, 0)

    @pl.when(jnp.logical_and(cur_valid, jnp.logical_not(nxt_valid)))
    def _():
        def body(c, carry):
            unpack(pl.multiple_of(c * rc, rc))
            return carry
        lax.fori_loop(0, tg // rc, body, 0)

    @pl.when(jnp.logical_not(cur_valid))
    def _():
        o_ref[...] = jnp.zeros_like(o_ref)


def _gather_rows(hp, src_tok, n_valid_rows):
    n_rows = src_tok.shape[0]
    half = hp.shape[1]
    tg = GATHER_ROWS
    n_t = n_rows // tg
    src_tiles = src_tok.reshape(n_t, 1, tg)
    return pl.pallas_call(
        _gather_kernel,
        grid_spec=pltpu.PrefetchScalarGridSpec(
            num_scalar_prefetch=1,
            grid=(n_t,),
            in_specs=[pl.BlockSpec((None, 1, tg), lambda i, n: (0, 0, 0), memory_space=pltpu.SMEM),
                      pl.BlockSpec((None, 1, tg), lambda i, n: (jnp.minimum(i + 1, n_t - 1), 0, 0),
                                   memory_space=pltpu.SMEM),
                      pl.BlockSpec(memory_space=pl.ANY)],
            out_specs=pl.BlockSpec((tg, 2 * half), lambda i, n: (i, 0)),
            scratch_shapes=[pltpu.VMEM((2, tg, half), U32), pltpu.SemaphoreType.DMA((2,))]),
        out_shape=jax.ShapeDtypeStruct((n_rows, 2 * half), BF16),
        compiler_params=_cparams("arbitrary"),
        name="moe_gather",
    )(n_valid_rows, src_tiles, src_tiles, hp)


def _expert_changed(te_ref):
    i = pl.program_id(1)
    return jnp.logical_or(i == 0, te_ref[i] != te_ref[jnp.maximum(i - 1, 0)])


def _moe_a_kernel(te_ref, nv_ref, x_ref, wg_ref, wu_ref, o_ref, wgb_ref, wub_ref):
    @pl.when(_expert_changed(te_ref))
    def _():
        wgb_ref[...] = wg_ref[...].astype(BF16)
        wub_ref[...] = wu_ref[...].astype(BF16)

    @pl.when(pl.program_id(1) < nv_ref[0])
    def _():
        x = x_ref[...]
        gate = _dot(x, wgb_ref[...])
        up = _dot(x, wub_ref[...])
        o_ref[...] = (gate * jax.nn.sigmoid(gate) * up).astype(BF16)

    @pl.when(pl.program_id(1) >= nv_ref[0])
    def _():
        o_ref[...] = jnp.zeros_like(o_ref)


def _moe_b_kernel(te_ref, nv_ref, a_ref, wd_ref, o_ref, wdb_ref):
    @pl.when(_expert_changed(te_ref))
    def _():
        wdb_ref[...] = wd_ref[...].astype(BF16)

    @pl.when(pl.program_id(1) < nv_ref[0])
    def _():
        y = _dot(a_ref[...], wdb_ref[...])
        half = y.shape[1] // 2
        o_ref[...] = _pack_bf16_pair(y[:, :half], y[:, half:])

    @pl.when(pl.program_id(1) >= nv_ref[0])
    def _():
        o_ref[...] = jnp.zeros_like(o_ref)


def _moe_experts(xs, tile_expert, n_valid, w_gate, w_up, w_down, layer):
    n_rows, d = xs.shape
    f = w_gate.shape[3]
    n_tiles = n_rows // MOE_TM
    row_idx = lambda j, i, te, nv: (jnp.minimum(i, nv[0] - 1), 0)
    w_idx = lambda j, i, te, nv: (layer, te[i], 0, j)
    a = pl.pallas_call(
        _moe_a_kernel,
        grid_spec=pltpu.PrefetchScalarGridSpec(
            num_scalar_prefetch=2,
            grid=(f // MOE_A_TN, n_tiles),
            in_specs=[pl.BlockSpec((MOE_TM, d), row_idx),
                      pl.BlockSpec((None, None, d, MOE_A_TN), w_idx),
                      pl.BlockSpec((None, None, d, MOE_A_TN), w_idx)],
            out_specs=pl.BlockSpec((MOE_TM, MOE_A_TN), lambda j, i, te, nv: (i, j)),
            scratch_shapes=[pltpu.VMEM((d, MOE_A_TN), BF16)] * 2),
        out_shape=jax.ShapeDtypeStruct((n_rows, f), BF16),
        compiler_params=_cparams("arbitrary", "arbitrary"),
        name="moe_gate_up",
    )(tile_expert, n_valid, xs, w_gate, w_up)
    ys = pl.pallas_call(
        _moe_b_kernel,
        grid_spec=pltpu.PrefetchScalarGridSpec(
            num_scalar_prefetch=2,
            grid=(d // MOE_B_TN, n_tiles),
            in_specs=[pl.BlockSpec((MOE_TM, f), row_idx),
                      pl.BlockSpec((None, None, f, MOE_B_TN), w_idx)],
            out_specs=pl.BlockSpec((MOE_TM, MOE_B_TN // 2), lambda j, i, te, nv: (i, j)),
            scratch_shapes=[pltpu.VMEM((f, MOE_B_TN), BF16)]),
        out_shape=jax.ShapeDtypeStruct((n_rows, d // 2), U32),
        compiler_params=_cparams("arbitrary", "arbitrary"),
        name="moe_down",
    )(tile_expert, n_valid, a, w_down)
    return ys


def _combine_kernel(dest0_ref, destn_ref, ys_hbm, x_ref, ew_ref, gate_ref, *rest, with_norm):
    if with_norm:
        g_ref, sh_ref, sc_ref, xo_ref, ho_ref, buf_ref, sem = rest
    else:
        xo_ref, buf_ref, sem = rest
    i = pl.program_id(0)
    slot = i % 2
    tc, d = x_ref.shape
    n = TOP_K * tc
    rc, cw = COMBINE_CHUNK_ROWS, COMBINE_CHUNK_COLS
    n_chunks = tc // rc
    per_chunk = n // n_chunks
    q = MOE_B_TN // 2

    @pl.when(i == 0)
    def _():
        def body(c, carry):
            r0 = pl.multiple_of(c * per_chunk, per_chunk)
            _issue_rows(ys_hbm, dest0_ref, r0, per_chunk, buf_ref.at[0], sem.at[0])
            return carry
        lax.fori_loop(0, n_chunks, body, 0)

    _all_rows_wait(ys_hbm, buf_ref.at[slot], sem.at[slot])

    def chunk(ci, carry, prefetch):
        if prefetch:
            p0 = pl.multiple_of(ci * per_chunk, per_chunk)
            _issue_rows(ys_hbm, destn_ref, p0, per_chunk, buf_ref.at[1 - slot], sem.at[1 - slot])
        r0 = pl.multiple_of(ci * rc, rc)
        rows, rows2 = pl.ds(r0, rc), pl.ds(r0 + tc, rc)
        a1, a2 = ew_ref[rows, 0:1], ew_ref[rows, 1:2]
        ss = jnp.zeros((rc, cw), F32)
        for j in range(d // (2 * q)):
            for cc in range(q // cw):
                pcols = slice(j * q + cc * cw, j * q + (cc + 1) * cw)
                w1, w2 = buf_ref[slot, rows, pcols], buf_ref[slot, rows2, pcols]
                for part, unpack in ((0, _unpack_lo), (1, _unpack_hi)):
                    c0 = j * 2 * q + part * q + cc * cw
                    cols = slice(c0, c0 + cw)
                    xo = x_ref[rows, cols] + gate_ref[:, cols] * (a1 * unpack(w1) + a2 * unpack(w2))
                    xo_ref[rows, cols] = xo
                    if with_norm:
                        ss = ss + xo * xo
        if with_norm:
            inv = lax.rsqrt(jnp.sum(ss, axis=-1, keepdims=True) / d + NORM_EPS)
            for cc in range(d // cw):
                cols = slice(cc * cw, (cc + 1) * cw)
                y = xo_ref[rows, cols] * inv * g_ref[:, cols]
                ho_ref[rows, cols] = (y * (1.0 + sc_ref[:, cols]) + sh_ref[:, cols]).astype(BF16)
        return carry

    has_next = i + 1 < pl.num_programs(0)

    @pl.when(has_next)
    def _():
        lax.fori_loop(0, n_chunks, functools.partial(chunk, prefetch=True), 0)

    @pl.when(jnp.logical_not(has_next))
    def _():
        lax.fori_loop(0, n_chunks, functools.partial(chunk, prefetch=False), 0)


def _combine(x2, ys, dest, ew, mods, layer, seq, norm=None):
    t, d = x2.shape
    tc = COMBINE_ROWS
    n_t = t // tc
    dest_tiles = dest.reshape(TOP_K, n_t, tc).transpose(1, 0, 2).reshape(n_t, 1, TOP_K * tc)
    row = lambda i: (i * tc) // seq
    mspec = lambda lay, chunk: pl.BlockSpec((None, None, None, 1, d), _mod_spec(lay, chunk, row))
    in_specs = [pl.BlockSpec((None, 1, TOP_K * tc), lambda i: (0, 0, 0), memory_space=pltpu.SMEM),
                pl.BlockSpec((None, 1, TOP_K * tc), lambda i: (jnp.minimum(i + 1, n_t - 1), 0, 0),
                             memory_space=pltpu.SMEM),
                pl.BlockSpec(memory_space=pl.ANY),
                pl.BlockSpec((tc, d), lambda i: (i, 0)),
                pl.BlockSpec((tc, TOP_K), lambda i: (i, 0)),
                mspec(layer, 5)]
    args = [dest_tiles, dest_tiles, ys, x2, ew.T, mods]
    out_shapes = [jax.ShapeDtypeStruct((t, d), F32)]
    out_specs = [pl.BlockSpec((tc, d), lambda i: (i, 0))]
    if norm is not None:
        next_layer, g = norm
        in_specs += [pl.BlockSpec((1, d), lambda i: (0, 0)), mspec(next_layer, 0), mspec(next_layer, 1)]
        args += [g.reshape(1, d), mods, mods]
        out_shapes.append(jax.ShapeDtypeStruct((t, d), BF16))
        out_specs.append(pl.BlockSpec((tc, d), lambda i: (i, 0)))
    return pl.pallas_call(
        functools.partial(_combine_kernel, with_norm=norm is not None),
        grid=(n_t,),
        in_specs=in_specs,
        out_specs=out_specs,
        out_shape=out_shapes,
        scratch_shapes=[pltpu.VMEM((2, TOP_K * tc, d // 2), U32), pltpu.SemaphoreType.DMA((2,))],
        compiler_params=_cparams("arbitrary"),
        name="moe_combine",
    )(*args)


def _moe_layer(x2, norm_g, mods, layer, seq, router_w, router_b, w_gate, w_up, w_down, next_norm):
    t = x2.shape[0]
    hp, eidx, ew, rank, cnt = _prenorm_router(x2, norm_g, mods, layer, seq, router_w, router_b)
    dest, src_tok, tile_expert, n_valid, _ = _routing_plan(eidx, rank, cnt, t)
    xs = _gather_rows(hp, src_tok, n_valid * MOE_TM)
    ys = _moe_experts(xs, tile_expert, n_valid, w_gate, w_up, w_down, layer)
    return _combine(x2, ys, dest, ew, mods, layer, seq, next_norm)


def _sgu_kernel(z_ref, ws_ref, bs_ref, g_ref, b_ref, o_ref):
    width = o_ref.shape[1]
    gdim = width // SGU_GROUPS
    for c in range(z_ref.shape[0] // SGU_CHUNK):
        rows = slice(c * SGU_CHUNK, (c + 1) * SGU_CHUNK)
        v = z_ref[rows, width:].astype(F32)
        mu = jnp.mean(v, axis=-1, keepdims=True)
        vc = v - mu
        var = jnp.mean(vc * vc, axis=-1, keepdims=True)
        vn = (vc * lax.rsqrt(var + NORM_EPS) * g_ref[...] + b_ref[...]).astype(BF16)
        for g in range(SGU_GROUPS):
            cols = slice(g * gdim, (g + 1) * gdim)
            s = _dot(ws_ref[g].astype(BF16), vn[:, cols]) + bs_ref[:, g:g + 1]
            o_ref[rows, cols] = (z_ref[rows, cols].astype(F32) * s).astype(BF16)


def _sgu_gate(z, w_s, b_s, ln_g, ln_b):
    t, two_w = z.shape
    width = two_w // 2
    tm = ROW_TILE
    return pl.pallas_call(
        _sgu_kernel,
        grid=(t // tm,),
        in_specs=[pl.BlockSpec((tm, two_w), lambda i: (i, 0)),
                  pl.BlockSpec((SGU_GROUPS, SGU_CHUNK, SGU_CHUNK), lambda i: (0, 0, 0)),
                  pl.BlockSpec((SGU_CHUNK, SGU_GROUPS), lambda i: (0, 0)),
                  pl.BlockSpec((1, width), lambda i: (0, 0)),
                  pl.BlockSpec((1, width), lambda i: (0, 0))],
        out_specs=pl.BlockSpec((tm, width), lambda i: (i, 0)),
        out_shape=jax.ShapeDtypeStruct((t, width), BF16),
        compiler_params=_cparams("arbitrary"),
        name="sgu_gate",
    )(z, w_s, b_s.T, ln_g.reshape(1, width), ln_b.reshape(1, width))


def kernel(x, c, ctx, c_ctx, ada_w, ada_b, norm1_g, norm2_g, na_w_qkv, na_q_g, na_k_g, na_rpb, na_w_o,
           sgu_w_uv, sgu_b_uv, sgu_ln_g, sgu_ln_b, sgu_w_s, sgu_b_s, sgu_w_out, sgu_b_out,
           router_w, router_b, moe_w_gate, moe_w_up, moe_w_down):
    batch, seq, d = x.shape
    ctx_len = ctx.shape[1]
    t = batch * seq
    x2 = x.reshape(t, d)
    ctx2 = ctx.reshape(batch * ctx_len, d)
    tm, tn = MM_TM, MM_TN

    mod_rows = 16
    cpad = jnp.concatenate([c, c_ctx[None, :], jnp.zeros((mod_rows - batch - 1, d), F32)], axis=0)
    mods = _ada_mods(cpad, ada_w, ada_b).reshape(ada_w.shape[0], mod_rows, 6, 1, d)
    gate_spec = lambda layer, chunk: pl.BlockSpec(
        (None, None, None, 1, tn), lambda j, i: (layer, (i * tm) // seq, chunk, 0, j))
    res_spec = pl.BlockSpec((tm, tn), lambda j, i: (i, j))
    bias_spec = pl.BlockSpec((1, tn), lambda j, i: (0, j))

    h_all = _prenorm0(x2, ctx2, norm1_g[0], mods, 0, seq, batch)
    q_rot, q_pl, k_rot, v = _qkv(h_all, na_w_qkv[0], na_q_g[0], na_k_g[0], t, seq, d)
    att = _attention(q_rot, q_pl, k_rot, v, na_rpb[0], batch, seq, ctx_len)
    (x2,) = _ws_matmul(att, na_w_o[0], 0, d, _ep_residual, [x2, mods], [res_spec, gate_spec(0, 2)],
                       [jax.ShapeDtypeStruct((t, d), F32)], [res_spec], "attn_out")
    x2, h = _moe_layer(x2, norm2_g[0], mods, 0, seq, router_w, router_b,
                       moe_w_gate, moe_w_up, moe_w_down, (1, norm1_g[1]))

    width = sgu_w_uv.shape[2] // 2
    (z,) = _ws_matmul(h, sgu_w_uv[0], 0, 2 * width, _ep_bias_gelu, [sgu_b_uv[0].reshape(1, -1)], [bias_spec],
                      [jax.ShapeDtypeStruct((t, 2 * width), BF16)], [res_spec], "sgu_uv")
    gated = _sgu_gate(z, sgu_w_s[0], sgu_b_s[0], sgu_ln_g[0], sgu_ln_b[0])
    (x2,) = _ws_matmul(gated, sgu_w_out[0], 0, d, _ep_bias_residual,
                       [x2, mods, sgu_b_out[0].reshape(1, -1)], [res_spec, gate_spec(1, 2), bias_spec],
                       [jax.ShapeDtypeStruct((t, d), F32)], [res_spec], "sgu_out")
    (x2,) = _moe_layer(x2, norm2_g[1], mods, 1, seq, router_w, router_b,
                       moe_w_gate, moe_w_up, moe_w_down, None)
    return x2.reshape(batch, seq, d)
```

```python
import functools

import jax
import jax.numpy as jnp
from jax import lax
from jax.experimental import pallas as pl
from jax.experimental.pallas import tpu as pltpu

F32, BF16, I32, U32 = jnp.float32, jnp.bfloat16, jnp.int32, jnp.uint32

GRID_W = 64
NORM_EPS = 1e-6
NA_HEAD_DIM = 128
NA_KH = 8
NA_KW = 16
LOG2_E = 1.4426950408889634
NA_QSCALE = NA_HEAD_DIM ** -0.5 * LOG2_E
ROPE_BASE = 10000.0
ROPE_FREQS = NA_HEAD_DIM // 4
SGU_CHUNK = 128
SGU_GROUPS = 16
N_EXPERTS = 16
N_GROUPS = 4
EXPERTS_PER_GROUP = N_EXPERTS // N_GROUPS
TOP_K = 2
MASK_VALUE = -1e30

LANES = 128
VMEM_LIMIT_BYTES = 56 * 1024 * 1024
MM_TM = 1024
MM_TN = 512
MM_RC = 256
ROW_TILE = 512
MOE_TM = 256
MOE_A_TN = 512
MOE_B_TN = 4096
GATHER_ROWS = 512
COMBINE_ROWS = 256
COMBINE_CHUNK_ROWS = 16
COMBINE_CHUNK_COLS = 512
NORM_CHUNK_ROWS = 16
NORM_CHUNK_COLS = 512
DMA_CHUNK_ROWS = 16
ATTN_CTX_CHUNK = 256
ATTN_DEPTH = 4


def _cparams(*sem):
    return pltpu.CompilerParams(dimension_semantics=sem, vmem_limit_bytes=VMEM_LIMIT_BYTES)


def _dot(a, b):
    return jnp.dot(a, b, preferred_element_type=F32)


def _dot_nt(a, b, precision=None):
    return lax.dot_general(a, b, (((1,), (1,)), ((), ())), precision=precision,
                           preferred_element_type=F32)


def _rms(x, g):
    return x * lax.rsqrt(jnp.mean(x * x, axis=-1, keepdims=True) + NORM_EPS) * g


def _pack_bf16_pair(lo, hi):
    lo_b = lax.bitcast_convert_type(lo.astype(BF16).astype(F32), U32) >> 16
    hi_b = lax.bitcast_convert_type(hi.astype(BF16).astype(F32), U32) & jnp.uint32(0xFFFF0000)
    return lo_b | hi_b


def _unpack_lo(w):
    return lax.bitcast_convert_type(w << 16, F32)


def _unpack_hi(w):
    return lax.bitcast_convert_type(w & jnp.uint32(0xFFFF0000), F32)


def _ada_kernel(c_ref, w_ref, b_ref, o_ref):
    c = c_ref[...]
    a = (c * jax.nn.sigmoid(c)).astype(BF16)
    o_ref[...] = _dot(a, w_ref[...].astype(BF16)) + b_ref[...]


def _ada_mods(cpad, ada_w, ada_b):
    n_layers, d, n = ada_w.shape
    rows = cpad.shape[0]
    tn = MM_TN
    return pl.pallas_call(
        _ada_kernel,
        grid=(n_layers, n // tn),
        in_specs=[pl.BlockSpec((rows, d), lambda l, j: (0, 0)),
                  pl.BlockSpec((None, d, tn), lambda l, j: (l, 0, j)),
                  pl.BlockSpec((None, 1, tn), lambda l, j: (l, 0, j))],
        out_specs=pl.BlockSpec((None, rows, tn), lambda l, j: (l, 0, j)),
        out_shape=jax.ShapeDtypeStruct((n_layers, rows, n), F32),
        compiler_params=_cparams("arbitrary", "arbitrary"),
        name="ada_mods",
    )(cpad, ada_w, ada_b.reshape(n_layers, 1, n))


def _mod_spec(layer, chunk, row_fn):
    def idx(*g):
        return (layer, row_fn(*g), chunk, 0, 0)
    return idx


def _prenorm0_kernel(x_ref, ctx_ref, g_ref, sh_ref, sc_ref, o_ref, *, n_lat):
    i = pl.program_id(0)

    def emit(v):
        o_ref[...] = (_rms(v, g_ref[...]) * (1.0 + sc_ref[...]) + sh_ref[...]).astype(BF16)

    @pl.when(i < n_lat)
    def _():
        emit(x_ref[...])

    @pl.when(i >= n_lat)
    def _():
        emit(ctx_ref[...])


def _prenorm0(x2, ctx2, g, mods, layer, seq, ctx_row):
    t, d = x2.shape
    tc = ctx2.shape[0]
    tm = ROW_TILE
    n_lat, n_ctx = t // tm, tc // tm
    row = lambda i: jnp.where(i < n_lat, (i * tm) // seq, ctx_row)
    mspec = lambda chunk: pl.BlockSpec((None, None, None, 1, d), _mod_spec(layer, chunk, row))
    return pl.pallas_call(
        functools.partial(_prenorm0_kernel, n_lat=n_lat),
        grid=(n_lat + n_ctx,),
        in_specs=[pl.BlockSpec((tm, d), lambda i: (jnp.minimum(i, n_lat - 1), 0)),
                  pl.BlockSpec((tm, d), lambda i: (jnp.maximum(i - n_lat, 0), 0)),
                  pl.BlockSpec((1, d), lambda i: (0, 0)),
                  mspec(0), mspec(1)],
        out_specs=pl.BlockSpec((tm, d), lambda i: (i, 0)),
        out_shape=jax.ShapeDtypeStruct((t + tc, d), BF16),
        compiler_params=_cparams("arbitrary"),
        name="prenorm0",
    )(x2, ctx2, g.reshape(1, d), mods, mods)


def _ws_body(x_ref, w_ref, *refs, n_extra, n_out, epilogue, tm, rc):
    extra, outs, wb_ref = refs[:n_extra], refs[n_extra:n_extra + n_out], refs[-1]

    @pl.when(pl.program_id(1) == 0)
    def _():
        wb_ref[...] = w_ref[...].astype(BF16)

    for c in range(tm // rc):
        rows = slice(c * rc, (c + 1) * rc)
        epilogue(_dot(x_ref[rows, :], wb_ref[...]), rows, extra, outs)


def _ws_matmul(x, w, col0, n_cols, epilogue, extras, extra_specs, out_shapes, out_specs, name,
               tm=MM_TM, tn=MM_TN, m=None):
    k = x.shape[1]
    m = x.shape[0] if m is None else m
    assert m % tm == 0 and n_cols % tn == 0 and col0 % tn == 0
    jb = col0 // tn
    body = functools.partial(_ws_body, n_extra=len(extras), n_out=len(out_shapes),
                             epilogue=epilogue, tm=tm, rc=MM_RC)
    return pl.pallas_call(
        body,
        grid=(n_cols // tn, m // tm),
        in_specs=[pl.BlockSpec((tm, k), lambda j, i: (i, 0)),
                  pl.BlockSpec((k, tn), lambda j, i: (0, j + jb))] + list(extra_specs),
        out_specs=out_specs,
        out_shape=out_shapes,
        scratch_shapes=[pltpu.VMEM((k, tn), BF16)],
        compiler_params=_cparams("arbitrary", "arbitrary"),
        name=name,
    )(x, w, *extras)


def _swap32(y):
    lane = lax.broadcasted_iota(I32, y.shape, 1)
    return jnp.where((lane & 32) != 0, pltpu.roll(y, 32, 1), pltpu.roll(y, 96, 1))


def _ep_q(acc, rows, extra, outs):
    g_ref, cos_ref, sin_ref = extra
    qrot_ref, qpl_ref = outs
    cos, sin = cos_ref[rows, :], sin_ref[rows, :]
    for h in range(acc.shape[1] // NA_HEAD_DIM):
        cols = slice(h * NA_HEAD_DIM, (h + 1) * NA_HEAD_DIM)
        y = _rms(acc[:, cols], g_ref[...]) * NA_QSCALE
        qpl_ref[rows, cols] = y.astype(BF16)
        qrot_ref[rows, cols] = (y * cos + _swap32(y) * sin).astype(BF16)


def _ep_k(acc, rows, extra, outs):
    g_ref, cos_ref, sin_ref = extra
    (krot_ref,) = outs
    cos, sin = cos_ref[rows, :], sin_ref[rows, :]
    for h in range(acc.shape[1] // NA_HEAD_DIM):
        cols = slice(h * NA_HEAD_DIM, (h + 1) * NA_HEAD_DIM)
        y = _rms(acc[:, cols], g_ref[...])
        krot_ref[rows, cols] = (y * cos + _swap32(y) * sin).astype(BF16)


def _ep_cast(acc, rows, extra, outs):
    outs[0][rows, :] = acc.astype(BF16)


def _ep_residual(acc, rows, extra, outs):
    x_ref, gate_ref = extra
    outs[0][rows, :] = x_ref[rows, :] + gate_ref[...] * acc


def _ep_bias_residual(acc, rows, extra, outs):
    x_ref, gate_ref, b_ref = extra
    outs[0][rows, :] = x_ref[rows, :] + gate_ref[...] * (acc + b_ref[...])


def _ep_bias_gelu(acc, rows, extra, outs):
    (b_ref,) = extra
    a = acc + b_ref[...]
    outs[0][rows, :] = (0.5 * a * (1.0 + lax.erf(a * (2.0 ** -0.5)))).astype(BF16)


def _rope_tables(seq, extra_rows):
    t = jnp.arange(seq, dtype=I32)
    pos = jnp.stack([t // GRID_W, t % GRID_W], axis=-1).astype(F32)
    inv_freq = ROPE_BASE ** (-jnp.arange(ROPE_FREQS, dtype=F32) / ROPE_FREQS)
    ang = pos[:, :, None] * inv_freq
    cos, sin = jnp.cos(ang), jnp.sin(ang)
    cos = jnp.stack([cos, cos], axis=2).reshape(seq, NA_HEAD_DIM)
    sin = jnp.stack([-sin, sin], axis=2).reshape(seq, NA_HEAD_DIM)
    cos = jnp.concatenate([cos, jnp.ones((extra_rows, NA_HEAD_DIM), F32)], axis=0)
    sin = jnp.concatenate([sin, jnp.zeros((extra_rows, NA_HEAD_DIM), F32)], axis=0)
    return cos, sin


def _qkv(h_all, w_qkv, q_g, k_g, t, seq, d):
    tm, tn = MM_TM, MM_TN
    m_all = h_all.shape[0]
    n_lat, per_seq = t // tm, seq // tm
    cos, sin = _rope_tables(seq, tm)
    g_spec = pl.BlockSpec((1, NA_HEAD_DIM), lambda j, i: (0, 0))
    tab_idx = lambda j, i: (jnp.where(i < n_lat, i % per_seq, per_seq), 0)
    tab_spec = pl.BlockSpec((tm, NA_HEAD_DIM), tab_idx)
    out_spec = pl.BlockSpec((tm, tn), lambda j, i: (i, j))
    q_rot, q_pl = _ws_matmul(
        h_all, w_qkv, 0, d, _ep_q, [q_g.reshape(1, -1), cos, sin], [g_spec, tab_spec, tab_spec],
        [jax.ShapeDtypeStruct((t, d), BF16)] * 2, [out_spec, out_spec], "qkv_q", m=t)
    (k_rot,) = _ws_matmul(
        h_all, w_qkv, d, d, _ep_k, [k_g.reshape(1, -1), cos, sin], [g_spec, tab_spec, tab_spec],
        [jax.ShapeDtypeStruct((m_all, d), BF16)], [out_spec], "qkv_k")
    (v,) = _ws_matmul(
        h_all, w_qkv, 2 * d, d, _ep_cast, [], [],
        [jax.ShapeDtypeStruct((m_all, d), BF16)], [out_spec], "qkv_v")
    return q_rot, q_pl, k_rot, v


def _bias_tables(rpb):
    n_heads, n_dr, _ = rpb.shape
    q = jnp.arange(GRID_W, dtype=I32)[:, None]
    kc = jnp.arange(GRID_W, dtype=I32)[None, :]
    dc = jnp.clip(kc - q + NA_KW - 1, 0, 2 * NA_KW - 2)
    cs = jnp.clip(q - NA_KW // 2, 0, GRID_W - NA_KW)
    in_win = (kc >= cs) & (kc < cs + NA_KW)
    c = jnp.where(in_win[None, None], rpb[:, :, dc] * LOG2_E, MASK_VALUE)
    c = c.transpose(0, 2, 1, 3).reshape(n_heads, GRID_W, n_dr * GRID_W).astype(F32)
    width = (n_dr + 2) * GRID_W
    c = jnp.pad(c, ((0, 0), (0, 0), (0, width - n_dr * GRID_W)))
    return c[:, :, :width - GRID_W], c[:, :, GRID_W:]


def _pipelined(n, start, finish, depth):
    pending = [start(i) for i in range(min(depth, n))]
    for i in range(n):
        s = pending.pop(0)
        if i + depth < n:
            pending.append(start(i + depth))
        finish(i, s)


def _attn_kernel(q_ref, qp_ref, k_ref, v_ref, kc_ref, vc_ref, c0_ref, c1_ref, o_ref,
                 mc_ref, lc_ref, oc_ref, *, rows):
    win = NA_KH * GRID_W
    half = NA_KH // 2
    seq = rows * GRID_W
    hd = q_ref.shape[1]

    def ctx_scores(c):
        return _dot_nt(qp_ref[c * ATTN_CTX_CHUNK:(c + 1) * ATTN_CTX_CHUNK, :], kc_ref[...])

    def ctx_finish(c, s):
        sl = slice(c * ATTN_CTX_CHUNK, (c + 1) * ATTN_CTX_CHUNK)
        m = jnp.max(s, axis=-1, keepdims=True)
        p = jnp.exp2(s - m)
        mc_ref[sl, :] = jnp.broadcast_to(m, (ATTN_CTX_CHUNK, hd))
        lc_ref[sl, :] = jnp.broadcast_to(jnp.sum(p, axis=-1, keepdims=True), (ATTN_CTX_CHUNK, hd))
        oc_ref[sl, :] = _dot(p.astype(BF16), vc_ref[...])

    _pipelined(seq // ATTN_CTX_CHUNK, ctx_scores, ctx_finish, 2)

    def key_start(r):
        return min(max(r - half, 0), rows - NA_KH)

    def loc_scores(r):
        k0 = key_start(r) * GRID_W
        off = (NA_KH - 1 - (r - key_start(r))) * GRID_W
        if off % LANES == 0:
            bias = c0_ref[:, off:off + win]
        else:
            bias = c1_ref[:, off - GRID_W:off - GRID_W + win]
        return _dot_nt(q_ref[r * GRID_W:(r + 1) * GRID_W, :], k_ref[k0:k0 + win, :]) + bias

    def loc_finish(r, s):
        sl = slice(r * GRID_W, (r + 1) * GRID_W)
        k0 = key_start(r) * GRID_W
        m_loc = jnp.max(s, axis=-1, keepdims=True)
        p = jnp.exp2(s - m_loc)
        l_loc = jnp.sum(p, axis=-1, keepdims=True)
        o_loc = _dot(p.astype(BF16), v_ref[k0:k0 + win, :])
        m_ctx = mc_ref[sl, :]
        m = jnp.maximum(m_loc, m_ctx)
        a = jnp.exp2(m_loc - m)
        b = jnp.exp2(m_ctx - m)
        denom = a * l_loc + b * lc_ref[sl, :]
        o_ref[sl, :] = ((a * o_loc + b * oc_ref[sl, :]) / denom).astype(BF16)

    _pipelined(rows, loc_scores, loc_finish, ATTN_DEPTH)


def _attention(q_rot, q_pl, k_rot, v, rpb, batch, seq, ctx_len):
    t, d = q_rot.shape
    n_heads = d // NA_HEAD_DIM
    rows = seq // GRID_W
    c0, c1 = _bias_tables(rpb)
    ctx_blk0 = t // ctx_len
    lat = pl.BlockSpec((seq, NA_HEAD_DIM), lambda h, b: (b, h))
    cx = pl.BlockSpec((ctx_len, NA_HEAD_DIM), lambda h, b: (ctx_blk0 + b, h))
    tab = pl.BlockSpec((None, GRID_W, c0.shape[2]), lambda h, b: (h, 0, 0))
    return pl.pallas_call(
        functools.partial(_attn_kernel, rows=rows),
        grid=(n_heads, batch),
        in_specs=[lat, lat, lat, lat, cx, cx, tab, tab],
        out_specs=lat,
        out_shape=jax.ShapeDtypeStruct((t, d), BF16),
        scratch_shapes=[pltpu.VMEM((seq, NA_HEAD_DIM), F32)] * 3,
        compiler_params=_cparams("arbitrary", "arbitrary"),
        name="attention",
    )(q_rot, q_pl, k_rot, v, k_rot, v, c0, c1)


def _router_logits(wrt_ref, h1_ref, h2_ref):
    w = wrt_ref[...]
    w1 = w.astype(BF16)
    w2 = (w - w1.astype(F32)).astype(BF16)
    a = _dot_nt(jnp.concatenate([w1, w2], axis=0), h1_ref[...])
    b = _dot_nt(w1, h2_ref[...])
    return a[:N_EXPERTS] + (a[N_EXPERTS:] + b)


def _route(logits, rb_ref, carry_ref, eidx_ref, ew_ref, rank_ref, cnt_ref):
    tm = logits.shape[1]
    scores = jax.nn.sigmoid(logits)
    sel = scores + rb_ref[...]

    def top2(vals):
        def first_max(vs):
            m = functools.reduce(jnp.maximum, vs)
            idx = jnp.full(m.shape, len(vs) - 1, I32)
            for k in range(len(vs) - 2, -1, -1):
                idx = jnp.where(vs[k] == m, k, idx)
            return m, idx
        m1, i1 = first_max(vals)
        m2, i2 = first_max([jnp.where(i1 == k, -jnp.inf, v) for k, v in enumerate(vals)])
        return m1, i1, m2, i2

    grp = []
    for g in range(N_GROUPS):
        vals = [sel[g * EXPERTS_PER_GROUP + k:g * EXPERTS_PER_GROUP + k + 1, :]
                for k in range(EXPERTS_PER_GROUP)]
        grp.append(top2(vals))
    gsum = [m1 + m2 for m1, _, m2, _ in grp]
    gmax = functools.reduce(jnp.maximum, gsum)
    g_idx = jnp.full(gmax.shape, N_GROUPS - 1, I32)
    for g in range(N_GROUPS - 2, -1, -1):
        g_idx = jnp.where(gsum[g] == gmax, g, g_idx)
    i1 = grp[N_GROUPS - 1][1]
    i2 = grp[N_GROUPS - 1][3]
    for g in range(N_GROUPS - 2, -1, -1):
        i1 = jnp.where(g_idx == g, grp[g][1], i1)
        i2 = jnp.where(g_idx == g, grp[g][3], i2)
    e1 = g_idx * EXPERTS_PER_GROUP + i1
    e2 = g_idx * EXPERTS_PER_GROUP + i2
    e_iota = lax.broadcasted_iota(I32, (N_EXPERTS, tm), 0)
    hit1, hit2 = e_iota == e1, e_iota == e2
    s1 = jnp.sum(jnp.where(hit1, scores, 0.0), axis=0, keepdims=True)
    s2 = jnp.sum(jnp.where(hit2, scores, 0.0), axis=0, keepdims=True)
    tot = s1 + s2
    eidx_ref[0:1, :] = e1
    eidx_ref[1:2, :] = e2
    ew_ref[0:1, :] = s1 / tot
    ew_ref[1:2, :] = s2 / tot
    onehot = (hit1 | hit2).astype(BF16)
    upper = (lax.broadcasted_iota(I32, (tm, tm), 0) < lax.broadcasted_iota(I32, (tm, tm), 1)).astype(BF16)
    before = _dot(onehot, upper) + carry_ref[...]
    rank_ref[0:1, :] = jnp.sum(jnp.where(hit1, before, 0.0), axis=0, keepdims=True).astype(I32)
    rank_ref[1:2, :] = jnp.sum(jnp.where(hit2, before, 0.0), axis=0, keepdims=True).astype(I32)
    carry_ref[...] += jnp.sum(onehot.astype(F32), axis=1, keepdims=True)
    cnt_ref[...] = jnp.broadcast_to(carry_ref[...], cnt_ref.shape)


def _prenorm_router_kernel(x_ref, g_ref, sh_ref, sc_ref, wrt_ref, rb_ref,
                           hp_ref, eidx_ref, ew_ref, rank_ref, cnt_ref, carry_ref, h1_ref, h2_ref, gs_ref):
    @pl.when(pl.program_id(0) == 0)
    def _():
        carry_ref[...] = jnp.zeros_like(carry_ref)

    tm, d = x_ref.shape
    half = d // 2
    rc, cw = NORM_CHUNK_ROWS, NORM_CHUNK_COLS
    gs_ref[...] = g_ref[...] * (1.0 + sc_ref[...])

    def chunk(ci, carry):
        rows = pl.ds(pl.multiple_of(ci * rc, rc), rc)
        ss = jnp.zeros((rc, cw), F32)
        for cc in range(d // cw):
            xv = x_ref[rows, cc * cw:(cc + 1) * cw]
            ss = ss + xv * xv
        inv = lax.rsqrt(jnp.sum(ss, axis=-1, keepdims=True) / d + NORM_EPS)

        def modulated_bits(cols):
            h = x_ref[rows, cols] * inv * gs_ref[:, cols] + sh_ref[:, cols]
            h1 = h.astype(BF16)
            hb = h1.astype(F32)
            h1_ref[rows, cols] = h1
            h2_ref[rows, cols] = (h - hb).astype(BF16)
            return lax.bitcast_convert_type(hb, U32)

        for cc in range(half // cw):
            lo = modulated_bits(slice(cc * cw, (cc + 1) * cw))
            hi = modulated_bits(slice(half + cc * cw, half + (cc + 1) * cw))
            hp_ref[rows, cc * cw:(cc + 1) * cw] = (lo >> 16) | hi
        return carry

    lax.fori_loop(0, tm // rc, chunk, 0, unroll=4)
    _route(_router_logits(wrt_ref, h1_ref, h2_ref), rb_ref, carry_ref, eidx_ref, ew_ref, rank_ref, cnt_ref)


def _router_out(t, d, tm):
    shapes = [jax.ShapeDtypeStruct((t, d // 2), U32),
              jax.ShapeDtypeStruct((TOP_K, t), I32),
              jax.ShapeDtypeStruct((TOP_K, t), F32),
              jax.ShapeDtypeStruct((TOP_K, t), I32),
              jax.ShapeDtypeStruct((N_EXPERTS, LANES), F32)]
    specs = [pl.BlockSpec((tm, d // 2), lambda i: (i, 0)),
             pl.BlockSpec((TOP_K, tm), lambda i: (0, i)),
             pl.BlockSpec((TOP_K, tm), lambda i: (0, i)),
             pl.BlockSpec((TOP_K, tm), lambda i: (0, i)),
             pl.BlockSpec((N_EXPERTS, LANES), lambda i: (0, 0))]
    return shapes, specs


def _prenorm_router(x2, g, mods, layer, seq, router_w, router_b):
    t, d = x2.shape
    tm = ROW_TILE
    row = lambda i: (i * tm) // seq
    mspec = lambda chunk: pl.BlockSpec((None, None, None, 1, d), _mod_spec(layer, chunk, row))
    shapes, specs = _router_out(t, d, tm)
    return pl.pallas_call(
        _prenorm_router_kernel,
        grid=(t // tm,),
        in_specs=[pl.BlockSpec((tm, d), lambda i: (i, 0)),
                  pl.BlockSpec((1, d), lambda i: (0, 0)),
                  mspec(3), mspec(4),
                  pl.BlockSpec((N_EXPERTS, d), lambda i: (0, 0)),
                  pl.BlockSpec((N_EXPERTS, 1), lambda i: (0, 0))],
        out_specs=specs,
        out_shape=shapes,
        scratch_shapes=[pltpu.VMEM((N_EXPERTS, 1), F32), pltpu.VMEM((tm, d), BF16), pltpu.VMEM((tm, d), BF16),
                        pltpu.VMEM((1, d), F32)],
        compiler_params=_cparams("arbitrary"),
        name="prenorm_router",
    )(x2, g.reshape(1, d), mods, mods, router_w.T, router_b.reshape(N_EXPERTS, 1))


def _routing_plan(eidx, rank, cnt, t):
    assert GATHER_ROWS % MOE_TM == 0
    max_rows = (TOP_K * t + N_EXPERTS * (MOE_TM - 1)) // MOE_TM * MOE_TM
    n_rows = -(-max_rows // GATHER_ROWS) * GATHER_ROWS
    n_tiles = n_rows // MOE_TM
    counts = cnt[:, 0].astype(I32)
    padded = ((counts + MOE_TM - 1) // MOE_TM) * MOE_TM
    ends = jnp.cumsum(padded)
    offs = ends - padded
    hit = eidx[None] == jnp.arange(N_EXPERTS, dtype=I32)[:, None, None]
    dest = jnp.sum(jnp.where(hit, offs[:, None, None], 0), axis=0) + rank
    n_valid = (ends[-1] // MOE_TM).astype(I32)
    tile_start = jnp.arange(n_tiles, dtype=I32) * MOE_TM
    tile_expert = jnp.sum((tile_start[:, None] >= ends[None, :]).astype(I32), axis=1)
    last = jnp.minimum(jnp.maximum(n_valid - 1, 0), n_tiles - 1)
    tile_expert = jnp.where(jnp.arange(n_tiles) < n_valid, tile_expert, tile_expert[last])
    tile_expert = jnp.minimum(tile_expert, N_EXPERTS - 1).astype(I32)
    tok = jnp.tile(jnp.arange(t, dtype=I32), TOP_K)
    src_tok = jnp.zeros((n_rows,), I32).at[dest.reshape(-1)].set(tok)
    return dest, src_tok, tile_expert, n_valid.reshape(1), n_tiles


def _row_copy(src_hbm, row, dst_vmem, slot, sem):
    return pltpu.make_async_copy(src_hbm.at[pl.ds(row, 1), :], dst_vmem.at[pl.ds(slot, 1), :], sem)


def _all_rows_wait(src_hbm, dst_vmem, sem):
    pltpu.make_async_copy(src_hbm.at[pl.ds(0, dst_vmem.shape[0]), :], dst_vmem, sem).wait()


def _issue_rows(src_hbm, idx_ref, r0, count, dst_vmem, sem):
    for k in range(count):
        _row_copy(src_hbm, idx_ref[0, r0 + k], dst_vmem, r0 + k, sem).start(priority=k % 2)


def _gather_kernel(nrows_ref, src0_ref, srcn_ref, hp_hbm, o_ref, buf_ref, sem):
    i = pl.program_id(0)
    _, tg, half = buf_ref.shape
    rc = DMA_CHUNK_ROWS
    slot = i % 2
    cur_valid = i * tg < nrows_ref[0]
    nxt_valid = (i + 1) * tg < nrows_ref[0]

    @pl.when(i == 0)
    def _():
        def body(c, carry):
            _issue_rows(hp_hbm, src0_ref, pl.multiple_of(c * rc, rc), rc, buf_ref.at[0], sem.at[0])
            return carry
        lax.fori_loop(0, tg // rc, body, 0)

    @pl.when(cur_valid)
    def _():
        _all_rows_wait(hp_hbm, buf_ref.at[slot], sem.at[slot])

    def unpack(r0):
        rows = pl.ds(r0, rc)
        w = buf_ref[slot, rows, :]
        o_ref[rows, :half] = _unpack_lo(w).astype(BF16)
        o_ref[rows, half:] = _unpack_hi(w).astype(BF16)

    @pl.when(jnp.logical_and(cur_valid, nxt_valid))
    def _():
        def body(c, carry):
            r0 = pl.multiple_of(c * rc, rc)
            unpack(r0)
            _issue_rows(hp_hbm, srcn_ref, r0, rc, buf_ref.at[1 - slot], sem.at[1 - slot])
            return carry
        lax.fori_loop(0, tg // rc, body, 0)

    @pl.when(jnp.logical_and(cur_valid, jnp.logical_not(nxt_valid)))
    def _():
        def body(c, carry):
            unpack(pl.multiple_of(c * rc, rc))
            return carry
        lax.fori_loop(0, tg // rc, body, 0)

    @pl.when(jnp.logical_not(cur_valid))
    def _():
        o_ref[...] = jnp.zeros_like(o_ref)


def _gather_rows(hp, src_tok, n_valid_rows):
    n_rows = src_tok.shape[0]
    half = hp.shape[1]
    tg = GATHER_ROWS
    n_t = n_rows // tg
    src_tiles = src_tok.reshape(n_t, 1, tg)
    return pl.pallas_call(
        _gather_kernel,
        grid_spec=pltpu.PrefetchScalarGridSpec(
            num_scalar_prefetch=1,
            grid=(n_t,),
            in_specs=[pl.BlockSpec((None, 1, tg), lambda i, n: (0, 0, 0), memory_space=pltpu.SMEM),
                      pl.BlockSpec((None, 1, tg), lambda i, n: (jnp.minimum(i + 1, n_t - 1), 0, 0),
                                   memory_space=pltpu.SMEM),
                      pl.BlockSpec(memory_space=pl.ANY)],
            out_specs=pl.BlockSpec((tg, 2 * half), lambda i, n: (i, 0)),
            scratch_shapes=[pltpu.VMEM((2, tg, half), U32), pltpu.SemaphoreType.DMA((2,))]),
        out_shape=jax.ShapeDtypeStruct((n_rows, 2 * half), BF16),
        compiler_params=_cparams("arbitrary"),
        name="moe_gather",
    )(n_valid_rows, src_tiles, src_tiles, hp)


def _expert_changed(te_ref):
    i = pl.program_id(1)
    return jnp.logical_or(i == 0, te_ref[i] != te_ref[jnp.maximum(i - 1, 0)])


def _moe_a_kernel(te_ref, nv_ref, x_ref, wg_ref, wu_ref, o_ref, wgb_ref, wub_ref):
    @pl.when(_expert_changed(te_ref))
    def _():
        wgb_ref[...] = wg_ref[...].astype(BF16)
        wub_ref[...] = wu_ref[...].astype(BF16)

    @pl.when(pl.program_id(1) < nv_ref[0])
    def _():
        x = x_ref[...]
        gate = _dot(x, wgb_ref[...])
        up = _dot(x, wub_ref[...])
        o_ref[...] = (gate * jax.nn.sigmoid(gate) * up).astype(BF16)

    @pl.when(pl.program_id(1) >= nv_ref[0])
    def _():
        o_ref[...] = jnp.zeros_like(o_ref)


def _moe_b_kernel(te_ref, nv_ref, a_ref, wd_ref, o_ref, wdb_ref):
    @pl.when(_expert_changed(te_ref))
    def _():
        wdb_ref[...] = wd_ref[...].astype(BF16)

    @pl.when(pl.program_id(1) < nv_ref[0])
    def _():
        y = _dot(a_ref[...], wdb_ref[...])
        half = y.shape[1] // 2
        o_ref[...] = _pack_bf16_pair(y[:, :half], y[:, half:])

    @pl.when(pl.program_id(1) >= nv_ref[0])
    def _():
        o_ref[...] = jnp.zeros_like(o_ref)


def _moe_experts(xs, tile_expert, n_valid, w_gate, w_up, w_down, layer):
    n_rows, d = xs.shape
    f = w_gate.shape[3]
    n_tiles = n_rows // MOE_TM
    row_idx = lambda j, i, te, nv: (jnp.minimum(i, nv[0] - 1), 0)
    w_idx = lambda j, i, te, nv: (layer, te[i], 0, j)
    a = pl.pallas_call(
        _moe_a_kernel,
        grid_spec=pltpu.PrefetchScalarGridSpec(
            num_scalar_prefetch=2,
            grid=(f // MOE_A_TN, n_tiles),
            in_specs=[pl.BlockSpec((MOE_TM, d), row_idx),
                      pl.BlockSpec((None, None, d, MOE_A_TN), w_idx),
                      pl.BlockSpec((None, None, d, MOE_A_TN), w_idx)],
            out_specs=pl.BlockSpec((MOE_TM, MOE_A_TN), lambda j, i, te, nv: (i, j)),
            scratch_shapes=[pltpu.VMEM((d, MOE_A_TN), BF16)] * 2),
        out_shape=jax.ShapeDtypeStruct((n_rows, f), BF16),
        compiler_params=_cparams("arbitrary", "arbitrary"),
        name="moe_gate_up",
    )(tile_expert, n_valid, xs, w_gate, w_up)
    ys = pl.pallas_call(
        _moe_b_kernel,
        grid_spec=pltpu.PrefetchScalarGridSpec(
            num_scalar_prefetch=2,
            grid=(d // MOE_B_TN, n_tiles),
            in_specs=[pl.BlockSpec((MOE_TM, f), row_idx),
                      pl.BlockSpec((None, None, f, MOE_B_TN), w_idx)],
            out_specs=pl.BlockSpec((MOE_TM, MOE_B_TN // 2), lambda j, i, te, nv: (i, j)),
            scratch_shapes=[pltpu.VMEM((f, MOE_B_TN), BF16)]),
        out_shape=jax.ShapeDtypeStruct((n_rows, d // 2), U32),
        compiler_params=_cparams("arbitrary", "arbitrary"),
        name="moe_down",
    )(tile_expert, n_valid, a, w_down)
    return ys


def _combine_kernel(dest0_ref, destn_ref, ys_hbm, x_ref, ew_ref, gate_ref, *rest, with_norm):
    if with_norm:
        g_ref, sh_ref, sc_ref, xo_ref, ho_ref, buf_ref, sem = rest
    else:
        xo_ref, buf_ref, sem = rest
    i = pl.program_id(0)
    slot = i % 2
    tc, d = x_ref.shape
    n = TOP_K * tc
    rc, cw = COMBINE_CHUNK_ROWS, COMBINE_CHUNK_COLS
    n_chunks = tc // rc
    per_chunk = n // n_chunks
    q = MOE_B_TN // 2

    @pl.when(i == 0)
    def _():
        def body(c, carry):
            r0 = pl.multiple_of(c * per_chunk, per_chunk)
            _issue_rows(ys_hbm, dest0_ref, r0, per_chunk, buf_ref.at[0], sem.at[0])
            return carry
        lax.fori_loop(0, n_chunks, body, 0)

    _all_rows_wait(ys_hbm, buf_ref.at[slot], sem.at[slot])

    def chunk(ci, carry, prefetch):
        r0 = pl.multiple_of(ci * rc, rc)
        rows, rows2 = pl.ds(r0, rc), pl.ds(r0 + tc, rc)
        a1, a2 = ew_ref[rows, 0:1], ew_ref[rows, 1:2]
        ss = jnp.zeros((rc, cw), F32)
        for j in range(d // (2 * q)):
            for cc in range(q // cw):
                pcols = slice(j * q + cc * cw, j * q + (cc + 1) * cw)
                w1, w2 = buf_ref[slot, rows, pcols], buf_ref[slot, rows2, pcols]
                for part, unpack in ((0, _unpack_lo), (1, _unpack_hi)):
                    c0 = j * 2 * q + part * q + cc * cw
                    cols = slice(c0, c0 + cw)
                    xo = x_ref[rows, cols] + gate_ref[:, cols] * (a1 * unpack(w1) + a2 * unpack(w2))
                    xo_ref[rows, cols] = xo
                    if with_norm:
                        ss = ss + xo * xo
        if with_norm:
            inv = lax.rsqrt(jnp.sum(ss, axis=-1, keepdims=True) / d + NORM_EPS)
            for cc in range(d // cw):
                cols = slice(cc * cw, (cc + 1) * cw)
                y = xo_ref[rows, cols] * inv * g_ref[:, cols]
                ho_ref[rows, cols] = (y * (1.0 + sc_ref[:, cols]) + sh_ref[:, cols]).astype(BF16)
        if prefetch:
            p0 = pl.multiple_of(ci * per_chunk, per_chunk)
            _issue_rows(ys_hbm, destn_ref, p0, per_chunk, buf_ref.at[1 - slot], sem.at[1 - slot])
        return carry

    has_next = i + 1 < pl.num_programs(0)

    @pl.when(has_next)
    def _():
        lax.fori_loop(0, n_chunks, functools.partial(chunk, prefetch=True), 0, unroll=2)

    @pl.when(jnp.logical_not(has_next))
    def _():
        lax.fori_loop(0, n_chunks, functools.partial(chunk, prefetch=False), 0, unroll=2)


def _combine(x2, ys, dest, ew, mods, layer, seq, norm=None):
    t, d = x2.shape
    tc = COMBINE_ROWS
    n_t = t // tc
    dest_tiles = dest.reshape(TOP_K, n_t, tc).transpose(1, 0, 2).reshape(n_t, 1, TOP_K * tc)
    row = lambda i: (i * tc) // seq
    mspec = lambda lay, chunk: pl.BlockSpec((None, None, None, 1, d), _mod_spec(lay, chunk, row))
    in_specs = [pl.BlockSpec((None, 1, TOP_K * tc), lambda i: (0, 0, 0), memory_space=pltpu.SMEM),
                pl.BlockSpec((None, 1, TOP_K * tc), lambda i: (jnp.minimum(i + 1, n_t - 1), 0, 0),
                             memory_space=pltpu.SMEM),
                pl.BlockSpec(memory_space=pl.ANY),
                pl.BlockSpec((tc, d), lambda i: (i, 0)),
                pl.BlockSpec((tc, TOP_K), lambda i: (i, 0)),
                mspec(layer, 5)]
    args = [dest_tiles, dest_tiles, ys, x2, ew.T, mods]
    out_shapes = [jax.ShapeDtypeStruct((t, d), F32)]
    out_specs = [pl.BlockSpec((tc, d), lambda i: (i, 0))]
    if norm is not None:
        next_layer, g = norm
        in_specs += [pl.BlockSpec((1, d), lambda i: (0, 0)), mspec(next_layer, 0), mspec(next_layer, 1)]
        args += [g.reshape(1, d), mods, mods]
        out_shapes.append(jax.ShapeDtypeStruct((t, d), BF16))
        out_specs.append(pl.BlockSpec((tc, d), lambda i: (i, 0)))
    return pl.pallas_call(
        functools.partial(_combine_kernel, with_norm=norm is not None),
        grid=(n_t,),
        in_specs=in_specs,
        out_specs=out_specs,
        out_shape=out_shapes,
        scratch_shapes=[pltpu.VMEM((2, TOP_K * tc, d // 2), U32), pltpu.SemaphoreType.DMA((2,))],
        compiler_params=_cparams("arbitrary"),
        name="moe_combine",
    )(*args)


def _moe_layer(x2, norm_g, mods, layer, seq, router_w, router_b, w_gate, w_up, w_down, next_norm):
    t = x2.shape[0]
    hp, eidx, ew, rank, cnt = _prenorm_router(x2, norm_g, mods, layer, seq, router_w, router_b)
    dest, src_tok, tile_expert, n_valid, _ = _routing_plan(eidx, rank, cnt, t)
    xs = _gather_rows(hp, src_tok, n_valid * MOE_TM)
    ys = _moe_experts(xs, tile_expert, n_valid, w_gate, w_up, w_down, layer)
    return _combine(x2, ys, dest, ew, mods, layer, seq, next_norm)


def _sgu_kernel(z_ref, ws_ref, bs_ref, g_ref, b_ref, o_ref):
    width = o_ref.shape[1]
    gdim = width // SGU_GROUPS
    for c in range(z_ref.shape[0] // SGU_CHUNK):
        rows = slice(c * SGU_CHUNK, (c + 1) * SGU_CHUNK)
        v = z_ref[rows, width:].astype(F32)
        mu = jnp.mean(v, axis=-1, keepdims=True)
        vc = v - mu
        var = jnp.mean(vc * vc, axis=-1, keepdims=True)
        vn = (vc * lax.rsqrt(var + NORM_EPS) * g_ref[...] + b_ref[...]).astype(BF16)
        for g in range(SGU_GROUPS):
            cols = slice(g * gdim, (g + 1) * gdim)
            s = _dot(ws_ref[g].astype(BF16), vn[:, cols]) + bs_ref[:, g:g + 1]
            o_ref[rows, cols] = (z_ref[rows, cols].astype(F32) * s).astype(BF16)


def _sgu_gate(z, w_s, b_s, ln_g, ln_b):
    t, two_w = z.shape
    width = two_w // 2
    tm = ROW_TILE
    return pl.pallas_call(
        _sgu_kernel,
        grid=(t // tm,),
        in_specs=[pl.BlockSpec((tm, two_w), lambda i: (i, 0)),
                  pl.BlockSpec((SGU_GROUPS, SGU_CHUNK, SGU_CHUNK), lambda i: (0, 0, 0)),
                  pl.BlockSpec((SGU_CHUNK, SGU_GROUPS), lambda i: (0, 0)),
                  pl.BlockSpec((1, width), lambda i: (0, 0)),
                  pl.BlockSpec((1, width), lambda i: (0, 0))],
        out_specs=pl.BlockSpec((tm, width), lambda i: (i, 0)),
        out_shape=jax.ShapeDtypeStruct((t, width), BF16),
        compiler_params=_cparams("arbitrary"),
        name="sgu_gate",
    )(z, w_s, b_s.T, ln_g.reshape(1, width), ln_b.reshape(1, width))


def kernel(x, c, ctx, c_ctx, ada_w, ada_b, norm1_g, norm2_g, na_w_qkv, na_q_g, na_k_g, na_rpb, na_w_o,
           sgu_w_uv, sgu_b_uv, sgu_ln_g, sgu_ln_b, sgu_w_s, sgu_b_s, sgu_w_out, sgu_b_out,
           router_w, router_b, moe_w_gate, moe_w_up, moe_w_down):
    batch, seq, d = x.shape
    ctx_len = ctx.shape[1]
    t = batch * seq
    x2 = x.reshape(t, d)
    ctx2 = ctx.reshape(batch * ctx_len, d)
    tm, tn = MM_TM, MM_TN

    mod_rows = 16
    cpad = jnp.concatenate([c, c_ctx[None, :], jnp.zeros((mod_rows - batch - 1, d), F32)], axis=0)
    mods = _ada_mods(cpad, ada_w, ada_b).reshape(ada_w.shape[0], mod_rows, 6, 1, d)
    gate_spec = lambda layer, chunk: pl.BlockSpec(
        (None, None, None, 1, tn), lambda j, i: (layer, (i * tm) // seq, chunk, 0, j))
    res_spec = pl.BlockSpec((tm, tn), lambda j, i: (i, j))
    bias_spec = pl.BlockSpec((1, tn), lambda j, i: (0, j))

    h_all = _prenorm0(x2, ctx2, norm1_g[0], mods, 0, seq, batch)
    q_rot, q_pl, k_rot, v = _qkv(h_all, na_w_qkv[0], na_q_g[0], na_k_g[0], t, seq, d)
    att = _attention(q_rot, q_pl, k_rot, v, na_rpb[0], batch, seq, ctx_len)
    (x2,) = _ws_matmul(att, na_w_o[0], 0, d, _ep_residual, [x2, mods], [res_spec, gate_spec(0, 2)],
                       [jax.ShapeDtypeStruct((t, d), F32)], [res_spec], "attn_out")
    x2, h = _moe_layer(x2, norm2_g[0], mods, 0, seq, router_w, router_b,
                       moe_w_gate, moe_w_up, moe_w_down, (1, norm1_g[1]))

    width = sgu_w_uv.shape[2] // 2
    (z,) = _ws_matmul(h, sgu_w_uv[0], 0, 2 * width, _ep_bias_gelu, [sgu_b_uv[0].reshape(1, -1)], [bias_spec],
                      [jax.ShapeDtypeStruct((t, 2 * width), BF16)], [res_spec], "sgu_uv")
    gated = _sgu_gate(z, sgu_w_s[0], sgu_b_s[0], sgu_ln_g[0], sgu_ln_b[0])
    (x2,) = _ws_matmul(gated, sgu_w_out[0], 0, d, _ep_bias_residual,
                       [x2, mods, sgu_b_out[0].reshape(1, -1)], [res_spec, gate_spec(1, 2), bias_spec],
                       [jax.ShapeDtypeStruct((t, d), F32)], [res_spec], "sgu_out")
    (x2,) = _moe_layer(x2, norm2_g[1], mods, 1, seq, router_w, router_b,
                       moe_w_gate, moe_w_up, moe_w_down, None)
    return x2.reshape(batch, seq, d)
```

```python
import functools

import jax
import jax.numpy as jnp
from jax import lax
from jax.experimental import pallas as pl
from jax.experimental.pallas import tpu as pltpu

F32, BF16, I32, U32 = jnp.float32, jnp.bfloat16, jnp.int32, jnp.uint32

GRID_W = 64
NORM_EPS = 1e-6
NA_HEAD_DIM = 128
NA_KH = 8
NA_KW = 16
LOG2_E = 1.4426950408889634
NA_QSCALE = NA_HEAD_DIM ** -0.5 * LOG2_E
ROPE_BASE = 10000.0
ROPE_FREQS = NA_HEAD_DIM // 4
SGU_CHUNK = 128
SGU_GROUPS = 16
N_EXPERTS = 16
N_GROUPS = 4
EXPERTS_PER_GROUP = N_EXPERTS // N_GROUPS
TOP_K = 2
MASK_VALUE = -1e30

LANES = 128
SUBLANES = 8
VMEM_LIMIT_BYTES = 56 * 1024 * 1024
MM_TM = 1024
MM_TN = 512
MM_RC = 256
ROW_TILE = 512
MOE_TM = 512
MOE_A_TN = 512
MOE_B_TN = 4096
GATHER_ROWS = 512
COMBINE_ROWS = 256
COMBINE_CHUNK_ROWS = 16
COMBINE_CHUNK_COLS = 512
NORM_CHUNK_ROWS = 16
NORM_CHUNK_COLS = 512
DMA_CHUNK_ROWS = 16
ATTN_CTX_CHUNK = 256
ATTN_DEPTH = 4


def _cparams(*sem):
    return pltpu.CompilerParams(dimension_semantics=sem, vmem_limit_bytes=VMEM_LIMIT_BYTES)


def _dot(a, b):
    return jnp.dot(a, b, preferred_element_type=F32)


def _dot_nt(a, b, precision=None):
    return lax.dot_general(a, b, (((1,), (1,)), ((), ())), precision=precision,
                           preferred_element_type=F32)


def _rms(x, g):
    return x * lax.rsqrt(jnp.mean(x * x, axis=-1, keepdims=True) + NORM_EPS) * g


def _pack_bf16_pair(lo, hi):
    lo_b = lax.bitcast_convert_type(lo.astype(BF16).astype(F32), U32) >> 16
    hi_b = lax.bitcast_convert_type(hi.astype(BF16).astype(F32), U32) & jnp.uint32(0xFFFF0000)
    return lo_b | hi_b


def _unpack_lo(w):
    return lax.bitcast_convert_type(w << 16, F32)


def _unpack_hi(w):
    return lax.bitcast_convert_type(w & jnp.uint32(0xFFFF0000), F32)


def _ada_kernel(c_ref, w_ref, b_ref, o_ref):
    c = c_ref[...]
    a = (c * jax.nn.sigmoid(c)).astype(BF16)
    o_ref[...] = _dot(a, w_ref[...].astype(BF16)) + b_ref[...]


def _ada_mods(cpad, ada_w, ada_b):
    n_layers, d, n = ada_w.shape
    rows = cpad.shape[0]
    tn = MM_TN
    return pl.pallas_call(
        _ada_kernel,
        grid=(n_layers, n // tn),
        in_specs=[pl.BlockSpec((rows, d), lambda l, j: (0, 0)),
                  pl.BlockSpec((None, d, tn), lambda l, j: (l, 0, j)),
                  pl.BlockSpec((None, 1, tn), lambda l, j: (l, 0, j))],
        out_specs=pl.BlockSpec((None, rows, tn), lambda l, j: (l, 0, j)),
        out_shape=jax.ShapeDtypeStruct((n_layers, rows, n), F32),
        compiler_params=_cparams("arbitrary", "arbitrary"),
        name="ada_mods",
    )(cpad, ada_w, ada_b.reshape(n_layers, 1, n))


def _mod_spec(layer, chunk, row_fn):
    def idx(*g):
        return (layer, row_fn(*g), chunk, 0, 0)
    return idx


def _prenorm0_kernel(x_ref, ctx_ref, g_ref, sh_ref, sc_ref, o_ref, *, n_lat):
    i = pl.program_id(0)

    def emit(v):
        o_ref[...] = (_rms(v, g_ref[...]) * (1.0 + sc_ref[...]) + sh_ref[...]).astype(BF16)

    @pl.when(i < n_lat)
    def _():
        emit(x_ref[...])

    @pl.when(i >= n_lat)
    def _():
        emit(ctx_ref[...])


def _prenorm0(x2, ctx2, g, mods, layer, seq, ctx_row):
    t, d = x2.shape
    tc = ctx2.shape[0]
    tm = ROW_TILE
    n_lat, n_ctx = t // tm, tc // tm
    row = lambda i: jnp.where(i < n_lat, (i * tm) // seq, ctx_row)
    mspec = lambda chunk: pl.BlockSpec((None, None, None, 1, d), _mod_spec(layer, chunk, row))
    return pl.pallas_call(
        functools.partial(_prenorm0_kernel, n_lat=n_lat),
        grid=(n_lat + n_ctx,),
        in_specs=[pl.BlockSpec((tm, d), lambda i: (jnp.minimum(i, n_lat - 1), 0)),
                  pl.BlockSpec((tm, d), lambda i: (jnp.maximum(i - n_lat, 0), 0)),
                  pl.BlockSpec((1, d), lambda i: (0, 0)),
                  mspec(0), mspec(1)],
        out_specs=pl.BlockSpec((tm, d), lambda i: (i, 0)),
        out_shape=jax.ShapeDtypeStruct((t + tc, d), BF16),
        compiler_params=_cparams("arbitrary"),
        name="prenorm0",
    )(x2, ctx2, g.reshape(1, d), mods, mods)


def _ws_body(x_ref, w_ref, *refs, n_extra, n_out, epilogue, tm, rc):
    extra, outs, wb_ref = refs[:n_extra], refs[n_extra:n_extra + n_out], refs[-1]

    @pl.when(pl.program_id(1) == 0)
    def _():
        wb_ref[...] = w_ref[...].astype(BF16)

    for c in range(tm // rc):
        rows = slice(c * rc, (c + 1) * rc)
        epilogue(_dot(x_ref[rows, :], wb_ref[...]), rows, extra, outs)


def _ws_matmul(x, w, col0, n_cols, epilogue, extras, extra_specs, out_shapes, out_specs, name,
               tm=MM_TM, tn=MM_TN, m=None):
    k = x.shape[1]
    m = x.shape[0] if m is None else m
    assert m % tm == 0 and n_cols % tn == 0 and col0 % tn == 0
    jb = col0 // tn
    body = functools.partial(_ws_body, n_extra=len(extras), n_out=len(out_shapes),
                             epilogue=epilogue, tm=tm, rc=MM_RC)
    return pl.pallas_call(
        body,
        grid=(n_cols // tn, m // tm),
        in_specs=[pl.BlockSpec((tm, k), lambda j, i: (i, 0)),
                  pl.BlockSpec((k, tn), lambda j, i: (0, j + jb))] + list(extra_specs),
        out_specs=out_specs,
        out_shape=out_shapes,
        scratch_shapes=[pltpu.VMEM((k, tn), BF16)],
        compiler_params=_cparams("arbitrary", "arbitrary"),
        name=name,
    )(x, w, *extras)


def _swap32(y):
    lane = lax.broadcasted_iota(I32, y.shape, 1)
    return jnp.where((lane & 32) != 0, pltpu.roll(y, 32, 1), pltpu.roll(y, 96, 1))


def _ep_q(acc, rows, extra, outs):
    g_ref, cos_ref, sin_ref = extra
    qrot_ref, qpl_ref = outs
    cos, sin = cos_ref[rows, :], sin_ref[rows, :]
    for h in range(acc.shape[1] // NA_HEAD_DIM):
        cols = slice(h * NA_HEAD_DIM, (h + 1) * NA_HEAD_DIM)
        y = _rms(acc[:, cols], g_ref[...]) * NA_QSCALE
        qpl_ref[rows, cols] = y.astype(BF16)
        qrot_ref[rows, cols] = (y * cos + _swap32(y) * sin).astype(BF16)


def _ep_k(acc, rows, extra, outs):
    g_ref, cos_ref, sin_ref = extra
    (krot_ref,) = outs
    cos, sin = cos_ref[rows, :], sin_ref[rows, :]
    for h in range(acc.shape[1] // NA_HEAD_DIM):
        cols = slice(h * NA_HEAD_DIM, (h + 1) * NA_HEAD_DIM)
        y = _rms(acc[:, cols], g_ref[...])
        krot_ref[rows, cols] = (y * cos + _swap32(y) * sin).astype(BF16)


def _ep_cast(acc, rows, extra, outs):
    outs[0][rows, :] = acc.astype(BF16)


def _ep_residual(acc, rows, extra, outs):
    x_ref, gate_ref = extra
    outs[0][rows, :] = x_ref[rows, :] + gate_ref[...] * acc


def _ep_bias_residual(acc, rows, extra, outs):
    x_ref, gate_ref, b_ref = extra
    outs[0][rows, :] = x_ref[rows, :] + gate_ref[...] * (acc + b_ref[...])


def _ep_bias_gelu(acc, rows, extra, outs):
    (b_ref,) = extra
    a = acc + b_ref[...]
    outs[0][rows, :] = (0.5 * a * (1.0 + lax.erf(a * (2.0 ** -0.5)))).astype(BF16)


def _rope_tables(seq, extra_rows):
    t = jnp.arange(seq, dtype=I32)
    pos = jnp.stack([t // GRID_W, t % GRID_W], axis=-1).astype(F32)
    inv_freq = ROPE_BASE ** (-jnp.arange(ROPE_FREQS, dtype=F32) / ROPE_FREQS)
    ang = pos[:, :, None] * inv_freq
    cos, sin = jnp.cos(ang), jnp.sin(ang)
    cos = jnp.stack([cos, cos], axis=2).reshape(seq, NA_HEAD_DIM)
    sin = jnp.stack([-sin, sin], axis=2).reshape(seq, NA_HEAD_DIM)
    cos = jnp.concatenate([cos, jnp.ones((extra_rows, NA_HEAD_DIM), F32)], axis=0)
    sin = jnp.concatenate([sin, jnp.zeros((extra_rows, NA_HEAD_DIM), F32)], axis=0)
    return cos, sin


def _qkv(h_all, w_qkv, q_g, k_g, t, seq, d):
    tm, tn = MM_TM, MM_TN
    m_all = h_all.shape[0]
    n_lat, per_seq = t // tm, seq // tm
    cos, sin = _rope_tables(seq, tm)
    g_spec = pl.BlockSpec((1, NA_HEAD_DIM), lambda j, i: (0, 0))
    tab_idx = lambda j, i: (jnp.where(i < n_lat, i % per_seq, per_seq), 0)
    tab_spec = pl.BlockSpec((tm, NA_HEAD_DIM), tab_idx)
    out_spec = pl.BlockSpec((tm, tn), lambda j, i: (i, j))
    q_rot, q_pl = _ws_matmul(
        h_all, w_qkv, 0, d, _ep_q, [q_g.reshape(1, -1), cos, sin], [g_spec, tab_spec, tab_spec],
        [jax.ShapeDtypeStruct((t, d), BF16)] * 2, [out_spec, out_spec], "qkv_q", m=t)
    (k_rot,) = _ws_matmul(
        h_all, w_qkv, d, d, _ep_k, [k_g.reshape(1, -1), cos, sin], [g_spec, tab_spec, tab_spec],
        [jax.ShapeDtypeStruct((m_all, d), BF16)], [out_spec], "qkv_k")
    (v,) = _ws_matmul(
        h_all, w_qkv, 2 * d, d, _ep_cast, [], [],
        [jax.ShapeDtypeStruct((m_all, d), BF16)], [out_spec], "qkv_v")
    return q_rot, q_pl, k_rot, v


def _bias_tables(rpb):
    n_heads, n_dr, _ = rpb.shape
    q = jnp.arange(GRID_W, dtype=I32)[:, None]
    kc = jnp.arange(GRID_W, dtype=I32)[None, :]
    dc = jnp.clip(kc - q + NA_KW - 1, 0, 2 * NA_KW - 2)
    cs = jnp.clip(q - NA_KW // 2, 0, GRID_W - NA_KW)
    in_win = (kc >= cs) & (kc < cs + NA_KW)
    c = jnp.where(in_win[None, None], rpb[:, :, dc] * LOG2_E, MASK_VALUE)
    c = c.transpose(0, 2, 1, 3).reshape(n_heads, GRID_W, n_dr * GRID_W).astype(F32)
    width = (n_dr + 2) * GRID_W
    c = jnp.pad(c, ((0, 0), (0, 0), (0, width - n_dr * GRID_W)))
    return c[:, :, :width - GRID_W], c[:, :, GRID_W:]


def _pipelined(n, start, finish, depth):
    pending = [start(i) for i in range(min(depth, n))]
    for i in range(n):
        s = pending.pop(0)
        if i + depth < n:
            pending.append(start(i + depth))
        finish(i, s)


def _attn_kernel(q_ref, qp_ref, k_ref, v_ref, kc_ref, vc_ref, c0_ref, c1_ref, o_ref,
                 mc_ref, lc_ref, oc_ref, *, rows):
    win = NA_KH * GRID_W
    half = NA_KH // 2
    seq = rows * GRID_W
    hd = q_ref.shape[1]

    def ctx_scores(c):
        return _dot_nt(qp_ref[c * ATTN_CTX_CHUNK:(c + 1) * ATTN_CTX_CHUNK, :], kc_ref[...])

    def ctx_finish(c, s):
        sl = slice(c * ATTN_CTX_CHUNK, (c + 1) * ATTN_CTX_CHUNK)
        m = jnp.max(s, axis=-1, keepdims=True)
        p = jnp.exp2(s - m)
        mc_ref[sl, :] = jnp.broadcast_to(m, (ATTN_CTX_CHUNK, hd))
        lc_ref[sl, :] = jnp.broadcast_to(jnp.sum(p, axis=-1, keepdims=True), (ATTN_CTX_CHUNK, hd))
        oc_ref[sl, :] = _dot(p.astype(BF16), vc_ref[...])

    _pipelined(seq // ATTN_CTX_CHUNK, ctx_scores, ctx_finish, 2)

    def key_start(r):
        return min(max(r - half, 0), rows - NA_KH)

    def loc_scores(r):
        k0 = key_start(r) * GRID_W
        off = (NA_KH - 1 - (r - key_start(r))) * GRID_W
        if off % LANES == 0:
            bias = c0_ref[:, off:off + win]
        else:
            bias = c1_ref[:, off - GRID_W:off - GRID_W + win]
        return _dot_nt(q_ref[r * GRID_W:(r + 1) * GRID_W, :], k_ref[k0:k0 + win, :]) + bias

    def loc_finish(r, s):
        sl = slice(r * GRID_W, (r + 1) * GRID_W)
        k0 = key_start(r) * GRID_W
        m_loc = jnp.max(s, axis=-1, keepdims=True)
        p = jnp.exp2(s - m_loc)
        l_loc = jnp.sum(p, axis=-1, keepdims=True)
        o_loc = _dot(p.astype(BF16), v_ref[k0:k0 + win, :])
        m_ctx = mc_ref[sl, :]
        m = jnp.maximum(m_loc, m_ctx)
        a = jnp.exp2(m_loc - m)
        b = jnp.exp2(m_ctx - m)
        denom = a * l_loc + b * lc_ref[sl, :]
        o_ref[sl, :] = ((a * o_loc + b * oc_ref[sl, :]) / denom).astype(BF16)

    _pipelined(rows, loc_scores, loc_finish, ATTN_DEPTH)


def _attention(q_rot, q_pl, k_rot, v, rpb, batch, seq, ctx_len):
    t, d = q_rot.shape
    n_heads = d // NA_HEAD_DIM
    rows = seq // GRID_W
    c0, c1 = _bias_tables(rpb)
    ctx_blk0 = t // ctx_len
    lat = pl.BlockSpec((seq, NA_HEAD_DIM), lambda h, b: (b, h))
    cx = pl.BlockSpec((ctx_len, NA_HEAD_DIM), lambda h, b: (ctx_blk0 + b, h))
    tab = pl.BlockSpec((None, GRID_W, c0.shape[2]), lambda h, b: (h, 0, 0))
    return pl.pallas_call(
        functools.partial(_attn_kernel, rows=rows),
        grid=(n_heads, batch),
        in_specs=[lat, lat, lat, lat, cx, cx, tab, tab],
        out_specs=lat,
        out_shape=jax.ShapeDtypeStruct((t, d), BF16),
        scratch_shapes=[pltpu.VMEM((seq, NA_HEAD_DIM), F32)] * 3,
        compiler_params=_cparams("arbitrary", "arbitrary"),
        name="attention",
    )(q_rot, q_pl, k_rot, v, k_rot, v, c0, c1)


def _router_logits(wrt_ref, h1_ref, h2_ref):
    w = wrt_ref[...]
    w1 = w.astype(BF16)
    w2 = (w - w1.astype(F32)).astype(BF16)
    a = _dot_nt(jnp.concatenate([w1, w2], axis=0), h1_ref[...])
    b = _dot_nt(w1, h2_ref[...])
    return a[:N_EXPERTS] + (a[N_EXPERTS:] + b)


def _route(logits, rb_ref, carry_ref, eidx_ref, ew_ref, rank_ref, cnt_ref):
    tm = logits.shape[1]
    scores = jax.nn.sigmoid(logits)
    sel = scores + rb_ref[...]

    def top2(vals):
        def first_max(vs):
            m = functools.reduce(jnp.maximum, vs)
            idx = jnp.full(m.shape, len(vs) - 1, I32)
            for k in range(len(vs) - 2, -1, -1):
                idx = jnp.where(vs[k] == m, k, idx)
            return m, idx
        m1, i1 = first_max(vals)
        m2, i2 = first_max([jnp.where(i1 == k, -jnp.inf, v) for k, v in enumerate(vals)])
        return m1, i1, m2, i2

    grp = []
    for g in range(N_GROUPS):
        vals = [sel[g * EXPERTS_PER_GROUP + k:g * EXPERTS_PER_GROUP + k + 1, :]
                for k in range(EXPERTS_PER_GROUP)]
        grp.append(top2(vals))
    gsum = [m1 + m2 for m1, _, m2, _ in grp]
    gmax = functools.reduce(jnp.maximum, gsum)
    g_idx = jnp.full(gmax.shape, N_GROUPS - 1, I32)
    for g in range(N_GROUPS - 2, -1, -1):
        g_idx = jnp.where(gsum[g] == gmax, g, g_idx)
    i1 = grp[N_GROUPS - 1][1]
    i2 = grp[N_GROUPS - 1][3]
    for g in range(N_GROUPS - 2, -1, -1):
        i1 = jnp.where(g_idx == g, grp[g][1], i1)
        i2 = jnp.where(g_idx == g, grp[g][3], i2)
    e1 = g_idx * EXPERTS_PER_GROUP + i1
    e2 = g_idx * EXPERTS_PER_GROUP + i2
    e_iota = lax.broadcasted_iota(I32, (N_EXPERTS, tm), 0)
    hit1, hit2 = e_iota == e1, e_iota == e2
    s1 = jnp.sum(jnp.where(hit1, scores, 0.0), axis=0, keepdims=True)
    s2 = jnp.sum(jnp.where(hit2, scores, 0.0), axis=0, keepdims=True)
    tot = s1 + s2
    eidx_ref[0:1, :] = e1
    eidx_ref[1:2, :] = e2
    ew_ref[0:1, :] = s1 / tot
    ew_ref[1:2, :] = s2 / tot
    onehot = (hit1 | hit2).astype(BF16)
    upper = (lax.broadcasted_iota(I32, (tm, tm), 0) < lax.broadcasted_iota(I32, (tm, tm), 1)).astype(BF16)
    before = _dot(onehot, upper) + carry_ref[...]
    rank_ref[0:1, :] = jnp.sum(jnp.where(hit1, before, 0.0), axis=0, keepdims=True).astype(I32)
    rank_ref[1:2, :] = jnp.sum(jnp.where(hit2, before, 0.0), axis=0, keepdims=True).astype(I32)
    carry_ref[...] += jnp.sum(onehot.astype(F32), axis=1, keepdims=True)
    cnt_ref[...] = jnp.broadcast_to(carry_ref[...], cnt_ref.shape)


def _prenorm_router_kernel(x_ref, g_ref, sh_ref, sc_ref, wrt_ref, rb_ref,
                           hp_ref, eidx_ref, ew_ref, rank_ref, cnt_ref, carry_ref, h1_ref, h2_ref, gs_ref):
    @pl.when(pl.program_id(0) == 0)
    def _():
        carry_ref[...] = jnp.zeros_like(carry_ref)

    tm, d = x_ref.shape
    half = d // 2
    rc, cw = NORM_CHUNK_ROWS, NORM_CHUNK_COLS
    gs_ref[...] = g_ref[...] * (1.0 + sc_ref[...])

    def chunk(ci, carry):
        rows = pl.ds(pl.multiple_of(ci * rc, rc), rc)
        ss = jnp.zeros((rc, cw), F32)
        for cc in range(d // cw):
            xv = x_ref[rows, cc * cw:(cc + 1) * cw]
            ss = ss + xv * xv
        inv = lax.rsqrt(jnp.sum(ss, axis=-1, keepdims=True) / d + NORM_EPS)

        def modulated_bits(cols):
            h = x_ref[rows, cols] * inv * gs_ref[:, cols] + sh_ref[:, cols]
            h1 = h.astype(BF16)
            hb = h1.astype(F32)
            h1_ref[rows, cols] = h1
            h2_ref[rows, cols] = (h - hb).astype(BF16)
            return lax.bitcast_convert_type(hb, U32)

        for cc in range(half // cw):
            lo = modulated_bits(slice(cc * cw, (cc + 1) * cw))
            hi = modulated_bits(slice(half + cc * cw, half + (cc + 1) * cw))
            hp_ref[rows, cc * cw:(cc + 1) * cw] = (lo >> 16) | hi
        return carry

    lax.fori_loop(0, tm // rc, chunk, 0, unroll=4)
    _route(_router_logits(wrt_ref, h1_ref, h2_ref), rb_ref, carry_ref, eidx_ref, ew_ref, rank_ref, cnt_ref)


def _router_out(t, d, tm):
    shapes = [jax.ShapeDtypeStruct((t, d // 2), U32),
              jax.ShapeDtypeStruct((TOP_K, t), I32),
              jax.ShapeDtypeStruct((TOP_K, t), F32),
              jax.ShapeDtypeStruct((TOP_K, t), I32),
              jax.ShapeDtypeStruct((N_EXPERTS, LANES), F32)]
    specs = [pl.BlockSpec((tm, d // 2), lambda i: (i, 0)),
             pl.BlockSpec((TOP_K, tm), lambda i: (0, i)),
             pl.BlockSpec((TOP_K, tm), lambda i: (0, i)),
             pl.BlockSpec((TOP_K, tm), lambda i: (0, i)),
             pl.BlockSpec((N_EXPERTS, LANES), lambda i: (0, 0))]
    return shapes, specs


def _prenorm_router(x2, g, mods, layer, seq, router_w, router_b):
    t, d = x2.shape
    tm = ROW_TILE
    row = lambda i: (i * tm) // seq
    mspec = lambda chunk: pl.BlockSpec((None, None, None, 1, d), _mod_spec(layer, chunk, row))
    shapes, specs = _router_out(t, d, tm)
    return pl.pallas_call(
        _prenorm_router_kernel,
        grid=(t // tm,),
        in_specs=[pl.BlockSpec((tm, d), lambda i: (i, 0)),
                  pl.BlockSpec((1, d), lambda i: (0, 0)),
                  mspec(3), mspec(4),
                  pl.BlockSpec((N_EXPERTS, d), lambda i: (0, 0)),
                  pl.BlockSpec((N_EXPERTS, 1), lambda i: (0, 0))],
        out_specs=specs,
        out_shape=shapes,
        scratch_shapes=[pltpu.VMEM((N_EXPERTS, 1), F32), pltpu.VMEM((tm, d), BF16), pltpu.VMEM((tm, d), BF16),
                        pltpu.VMEM((1, d), F32)],
        compiler_params=_cparams("arbitrary"),
        name="prenorm_router",
    )(x2, g.reshape(1, d), mods, mods, router_w.T, router_b.reshape(N_EXPERTS, 1))


def _routing_plan(eidx, rank, cnt, t):
    assert GATHER_ROWS % MOE_TM == 0
    max_rows = (TOP_K * t + N_EXPERTS * (MOE_TM - 1)) // MOE_TM * MOE_TM
    n_rows = -(-max_rows // GATHER_ROWS) * GATHER_ROWS
    n_tiles = n_rows // MOE_TM
    counts = cnt[:, 0].astype(I32)
    padded = ((counts + MOE_TM - 1) // MOE_TM) * MOE_TM
    ends = jnp.cumsum(padded)
    offs = ends - padded
    hit = eidx[None] == jnp.arange(N_EXPERTS, dtype=I32)[:, None, None]
    dest = jnp.sum(jnp.where(hit, offs[:, None, None], 0), axis=0) + rank
    n_valid = (ends[-1] // MOE_TM).astype(I32)
    tile_start = jnp.arange(n_tiles, dtype=I32) * MOE_TM
    tile_expert = jnp.sum((tile_start[:, None] >= ends[None, :]).astype(I32), axis=1)
    last = jnp.minimum(jnp.maximum(n_valid - 1, 0), n_tiles - 1)
    tile_expert = jnp.where(jnp.arange(n_tiles) < n_valid, tile_expert, tile_expert[last])
    tile_expert = jnp.minimum(tile_expert, N_EXPERTS - 1).astype(I32)
    tok = jnp.tile(jnp.arange(t, dtype=I32), TOP_K)
    src_tok = jnp.zeros((n_rows,), I32).at[dest.reshape(-1)].set(tok)
    return dest, src_tok, tile_expert, n_valid.reshape(1), n_tiles


def _row_copy(src_tiles, row, dst_tiles, tile, sub, sem):
    src = src_tiles.at[lax.shift_right_logical(row, 3), pl.ds(row & (SUBLANES - 1), 1), :]
    return pltpu.make_async_copy(src, dst_tiles.at[tile, pl.ds(sub, 1), :], sem)


def _all_rows_wait(src_tiles, dst_tiles, sem):
    pltpu.make_async_copy(src_tiles.at[pl.ds(0, dst_tiles.shape[0])], dst_tiles, sem).wait()


def _issue_rows(src_tiles, idx_ref, r0, count, dst_tiles, sem):
    tile0 = lax.shift_right_logical(r0, 3)
    for k in range(count):
        _row_copy(src_tiles, idx_ref[0, r0 + k], dst_tiles, tile0 + k // SUBLANES, k % SUBLANES,
                  sem).start(priority=k % 2)


def _load_rows(buf_ref, slot, r0, count, cols=slice(None)):
    w = buf_ref[slot, pl.ds(lax.shift_right_logical(r0, 3), count // SUBLANES), :, cols]
    return w.reshape(count, w.shape[-1])


def _gather_kernel(nrows_ref, src0_ref, srcn_ref, hp_hbm, o_ref, buf_ref, sem):
    i = pl.program_id(0)
    tg, half = o_ref.shape[0], buf_ref.shape[-1]
    rc = DMA_CHUNK_ROWS
    slot = i % 2
    cur_valid = i * tg < nrows_ref[0]
    nxt_valid = (i + 1) * tg < nrows_ref[0]

    @pl.when(i == 0)
    def _():
        def body(c, carry):
            _issue_rows(hp_hbm, src0_ref, pl.multiple_of(c * rc, rc), rc, buf_ref.at[0], sem.at[0])
            return carry
        lax.fori_loop(0, tg // rc, body, 0)

    @pl.when(cur_valid)
    def _():
        _all_rows_wait(hp_hbm, buf_ref.at[slot], sem.at[slot])

    def unpack(r0):
        rows = pl.ds(r0, rc)
        w = _load_rows(buf_ref, slot, r0, rc)
        o_ref[rows, :half] = _unpack_lo(w).astype(BF16)
        o_ref[rows, half:] = _unpack_hi(w).astype(BF16)

    @pl.when(jnp.logical_and(cur_valid, nxt_valid))
    def _():
        def body(c, carry):
            r0 = pl.multiple_of(c * rc, rc)
            unpack(r0)
            _issue_rows(hp_hbm, srcn_ref, r0, rc, buf_ref.at[1 - slot], sem.at[1 - slot])
            return carry
        lax.fori_loop(0, tg // rc, body, 0)

    @pl.when(jnp.logical_and(cur_valid, jnp.logical_not(nxt_valid)))
    def _():
        def body(c, carry):
            unpack(pl.multiple_of(c * rc, rc))
            return carry
        lax.fori_loop(0, tg // rc, body, 0)

    @pl.when(jnp.logical_not(cur_valid))
    def _():
        o_ref[...] = jnp.zeros_like(o_ref)


def _gather_rows(hp, src_tok, n_valid_rows):
    n_rows = src_tok.shape[0]
    half = hp.shape[1]
    tg = GATHER_ROWS
    n_t = n_rows // tg
    src_tiles = src_tok.reshape(n_t, 1, tg)
    return pl.pallas_call(
        _gather_kernel,
        grid_spec=pltpu.PrefetchScalarGridSpec(
            num_scalar_prefetch=1,
            grid=(n_t,),
            in_specs=[pl.BlockSpec((None, 1, tg), lambda i, n: (0, 0, 0), memory_space=pltpu.SMEM),
                      pl.BlockSpec((None, 1, tg), lambda i, n: (jnp.minimum(i + 1, n_t - 1), 0, 0),
                                   memory_space=pltpu.SMEM),
                      pl.BlockSpec(memory_space=pl.ANY)],
            out_specs=pl.BlockSpec((tg, 2 * half), lambda i, n: (i, 0)),
            scratch_shapes=[pltpu.VMEM((2, tg // SUBLANES, SUBLANES, half), U32),
                            pltpu.SemaphoreType.DMA((2,))]),
        out_shape=jax.ShapeDtypeStruct((n_rows, 2 * half), BF16),
        compiler_params=_cparams("arbitrary"),
        name="moe_gather",
    )(n_valid_rows, src_tiles, src_tiles, hp.reshape(hp.shape[0] // SUBLANES, SUBLANES, half))


def _expert_changed(te_ref):
    i = pl.program_id(1)
    return jnp.logical_or(i == 0, te_ref[i] != te_ref[jnp.maximum(i - 1, 0)])


def _moe_a_kernel(te_ref, nv_ref, x_ref, wg_ref, wu_ref, o_ref, wgb_ref, wub_ref):
    @pl.when(_expert_changed(te_ref))
    def _():
        wgb_ref[...] = wg_ref[...].astype(BF16)
        wub_ref[...] = wu_ref[...].astype(BF16)

    @pl.when(pl.program_id(1) < nv_ref[0])
    def _():
        x = x_ref[...]
        gate = _dot(x, wgb_ref[...])
        up = _dot(x, wub_ref[...])
        o_ref[...] = (gate * jax.nn.sigmoid(gate) * up).astype(BF16)

    @pl.when(pl.program_id(1) >= nv_ref[0])
    def _():
        o_ref[...] = jnp.zeros_like(o_ref)


def _moe_b_kernel(te_ref, nv_ref, a_ref, wd_ref, o_ref, wdb_ref):
    @pl.when(_expert_changed(te_ref))
    def _():
        wdb_ref[...] = wd_ref[...].astype(BF16)

    @pl.when(pl.program_id(1) < nv_ref[0])
    def _():
        y = _dot(a_ref[...], wdb_ref[...])
        half = y.shape[1] // 2
        o_ref[...] = _pack_bf16_pair(y[:, :half], y[:, half:])

    @pl.when(pl.program_id(1) >= nv_ref[0])
    def _():
        o_ref[...] = jnp.zeros_like(o_ref)


def _moe_experts(xs, tile_expert, n_valid, w_gate, w_up, w_down, layer):
    n_rows, d = xs.shape
    f = w_gate.shape[3]
    n_tiles = n_rows // MOE_TM
    row_idx = lambda j, i, te, nv: (jnp.minimum(i, nv[0] - 1), 0)
    w_idx = lambda j, i, te, nv: (layer, te[i], 0, j)
    a = pl.pallas_call(
        _moe_a_kernel,
        grid_spec=pltpu.PrefetchScalarGridSpec(
            num_scalar_prefetch=2,
            grid=(f // MOE_A_TN, n_tiles),
            in_specs=[pl.BlockSpec((MOE_TM, d), row_idx),
                      pl.BlockSpec((None, None, d, MOE_A_TN), w_idx),
                      pl.BlockSpec((None, None, d, MOE_A_TN), w_idx)],
            out_specs=pl.BlockSpec((MOE_TM, MOE_A_TN), lambda j, i, te, nv: (i, j)),
            scratch_shapes=[pltpu.VMEM((d, MOE_A_TN), BF16)] * 2),
        out_shape=jax.ShapeDtypeStruct((n_rows, f), BF16),
        compiler_params=_cparams("arbitrary", "arbitrary"),
        name="moe_gate_up",
    )(tile_expert, n_valid, xs, w_gate, w_up)
    ys = pl.pallas_call(
        _moe_b_kernel,
        grid_spec=pltpu.PrefetchScalarGridSpec(
            num_scalar_prefetch=2,
            grid=(d // MOE_B_TN, n_tiles),
            in_specs=[pl.BlockSpec((MOE_TM, f), row_idx),
                      pl.BlockSpec((None, None, f, MOE_B_TN), w_idx)],
            out_specs=pl.BlockSpec((MOE_TM, MOE_B_TN // 2), lambda j, i, te, nv: (i, j)),
            scratch_shapes=[pltpu.VMEM((f, MOE_B_TN), BF16)]),
        out_shape=jax.ShapeDtypeStruct((n_rows, d // 2), U32),
        compiler_params=_cparams("arbitrary", "arbitrary"),
        name="moe_down",
    )(tile_expert, n_valid, a, w_down)
    return ys


def _combine_kernel(dest0_ref, destn_ref, ys_hbm, x_ref, ew_ref, gate_ref, *rest, with_norm):
    if with_norm:
        g_ref, sh_ref, sc_ref, xo_ref, ho_ref, buf_ref, sem = rest
    else:
        xo_ref, buf_ref, sem = rest
    i = pl.program_id(0)
    slot = i % 2
    tc, d = x_ref.shape
    n = TOP_K * tc
    rc, cw = COMBINE_CHUNK_ROWS, COMBINE_CHUNK_COLS
    n_chunks = tc // rc
    per_chunk = n // n_chunks
    q = MOE_B_TN // 2

    @pl.when(i == 0)
    def _():
        def body(c, carry):
            r0 = pl.multiple_of(c * per_chunk, per_chunk)
            _issue_rows(ys_hbm, dest0_ref, r0, per_chunk, buf_ref.at[0], sem.at[0])
            return carry
        lax.fori_loop(0, n_chunks, body, 0)

    _all_rows_wait(ys_hbm, buf_ref.at[slot], sem.at[slot])

    def chunk(ci, carry, prefetch):
        r0 = pl.multiple_of(ci * rc, rc)
        rows, rows2 = pl.ds(r0, rc), pl.ds(r0 + tc, rc)
        a1, a2 = ew_ref[rows, 0:1], ew_ref[rows, 1:2]
        ss = jnp.zeros((rc, cw), F32)
        for j in range(d // (2 * q)):
            for cc in range(q // cw):
                pcols = slice(j * q + cc * cw, j * q + (cc + 1) * cw)
                w1 = _load_rows(buf_ref, slot, r0, rc, pcols)
                w2 = _load_rows(buf_ref, slot, r0 + tc, rc, pcols)
                for part, unpack in ((0, _unpack_lo), (1, _unpack_hi)):
                    c0 = j * 2 * q + part * q + cc * cw
                    cols = slice(c0, c0 + cw)
                    xo = x_ref[rows, cols] + gate_ref[:, cols] * (a1 * unpack(w1) + a2 * unpack(w2))
                    xo_ref[rows, cols] = xo
                    if with_norm:
                        ss = ss + xo * xo
        if with_norm:
            inv = lax.rsqrt(jnp.sum(ss, axis=-1, keepdims=True) / d + NORM_EPS)
            for cc in range(d // cw):
                cols = slice(cc * cw, (cc + 1) * cw)
                y = xo_ref[rows, cols] * inv * g_ref[:, cols]
                ho_ref[rows, cols] = (y * (1.0 + sc_ref[:, cols]) + sh_ref[:, cols]).astype(BF16)
        if prefetch:
            p0 = pl.multiple_of(ci * per_chunk, per_chunk)
            _issue_rows(ys_hbm, destn_ref, p0, per_chunk, buf_ref.at[1 - slot], sem.at[1 - slot])
        return carry

    has_next = i + 1 < pl.num_programs(0)

    @pl.when(has_next)
    def _():
        lax.fori_loop(0, n_chunks, functools.partial(chunk, prefetch=True), 0, unroll=2)

    @pl.when(jnp.logical_not(has_next))
    def _():
        lax.fori_loop(0, n_chunks, functools.partial(chunk, prefetch=False), 0, unroll=2)


def _combine(x2, ys, dest, ew, mods, layer, seq, norm=None):
    t, d = x2.shape
    tc = COMBINE_ROWS
    n_t = t // tc
    dest_tiles = dest.reshape(TOP_K, n_t, tc).transpose(1, 0, 2).reshape(n_t, 1, TOP_K * tc)
    row = lambda i: (i * tc) // seq
    mspec = lambda lay, chunk: pl.BlockSpec((None, None, None, 1, d), _mod_spec(lay, chunk, row))
    in_specs = [pl.BlockSpec((None, 1, TOP_K * tc), lambda i: (0, 0, 0), memory_space=pltpu.SMEM),
                pl.BlockSpec((None, 1, TOP_K * tc), lambda i: (jnp.minimum(i + 1, n_t - 1), 0, 0),
                             memory_space=pltpu.SMEM),
                pl.BlockSpec(memory_space=pl.ANY),
                pl.BlockSpec((tc, d), lambda i: (i, 0)),
                pl.BlockSpec((tc, TOP_K), lambda i: (i, 0)),
                mspec(layer, 5)]
    args = [dest_tiles, dest_tiles, ys.reshape(ys.shape[0] // SUBLANES, SUBLANES, d // 2), x2, ew.T, mods]
    out_shapes = [jax.ShapeDtypeStruct((t, d), F32)]
    out_specs = [pl.BlockSpec((tc, d), lambda i: (i, 0))]
    if norm is not None:
        next_layer, g = norm
        in_specs += [pl.BlockSpec((1, d), lambda i: (0, 0)), mspec(next_layer, 0), mspec(next_layer, 1)]
        args += [g.reshape(1, d), mods, mods]
        out_shapes.append(jax.ShapeDtypeStruct((t, d), BF16))
        out_specs.append(pl.BlockSpec((tc, d), lambda i: (i, 0)))
    return pl.pallas_call(
        functools.partial(_combine_kernel, with_norm=norm is not None),
        grid=(n_t,),
        in_specs=in_specs,
        out_specs=out_specs,
        out_shape=out_shapes,
        scratch_shapes=[pltpu.VMEM((2, TOP_K * tc // SUBLANES, SUBLANES, d // 2), U32),
                        pltpu.SemaphoreType.DMA((2,))],
        compiler_params=_cparams("arbitrary"),
        name="moe_combine",
    )(*args)


def _moe_layer(x2, norm_g, mods, layer, seq, router_w, router_b, w_gate, w_up, w_down, next_norm):
    t = x2.shape[0]
    hp, eidx, ew, rank, cnt = _prenorm_router(x2, norm_g, mods, layer, seq, router_w, router_b)
    dest, src_tok, tile_expert, n_valid, _ = _routing_plan(eidx, rank, cnt, t)
    xs = _gather_rows(hp, src_tok, n_valid * MOE_TM)
    ys = _moe_experts(xs, tile_expert, n_valid, w_gate, w_up, w_down, layer)
    return _combine(x2, ys, dest, ew, mods, layer, seq, next_norm)


def _sgu_kernel(z_ref, ws_ref, bs_ref, g_ref, b_ref, o_ref):
    width = o_ref.shape[1]
    gdim = width // SGU_GROUPS
    for c in range(z_ref.shape[0] // SGU_CHUNK):
        rows = slice(c * SGU_CHUNK, (c + 1) * SGU_CHUNK)
        v = z_ref[rows, width:].astype(F32)
        mu = jnp.mean(v, axis=-1, keepdims=True)
        vc = v - mu
        var = jnp.mean(vc * vc, axis=-1, keepdims=True)
        vn = (vc * lax.rsqrt(var + NORM_EPS) * g_ref[...] + b_ref[...]).astype(BF16)
        for g in range(SGU_GROUPS):
            cols = slice(g * gdim, (g + 1) * gdim)
            s = _dot(ws_ref[g].astype(BF16), vn[:, cols]) + bs_ref[:, g:g + 1]
            o_ref[rows, cols] = (z_ref[rows, cols].astype(F32) * s).astype(BF16)


def _sgu_gate(z, w_s, b_s, ln_g, ln_b):
    t, two_w = z.shape
    width = two_w // 2
    tm = ROW_TILE
    return pl.pallas_call(
        _sgu_kernel,
        grid=(t // tm,),
        in_specs=[pl.BlockSpec((tm, two_w), lambda i: (i, 0)),
                  pl.BlockSpec((SGU_GROUPS, SGU_CHUNK, SGU_CHUNK), lambda i: (0, 0, 0)),
                  pl.BlockSpec((SGU_CHUNK, SGU_GROUPS), lambda i: (0, 0)),
                  pl.BlockSpec((1, width), lambda i: (0, 0)),
                  pl.BlockSpec((1, width), lambda i: (0, 0))],
        out_specs=pl.BlockSpec((tm, width), lambda i: (i, 0)),
        out_shape=jax.ShapeDtypeStruct((t, width), BF16),
        compiler_params=_cparams("arbitrary"),
        name="sgu_gate",
    )(z, w_s, b_s.T, ln_g.reshape(1, width), ln_b.reshape(1, width))


def kernel(x, c, ctx, c_ctx, ada_w, ada_b, norm1_g, norm2_g, na_w_qkv, na_q_g, na_k_g, na_rpb, na_w_o,
           sgu_w_uv, sgu_b_uv, sgu_ln_g, sgu_ln_b, sgu_w_s, sgu_b_s, sgu_w_out, sgu_b_out,
           router_w, router_b, moe_w_gate, moe_w_up, moe_w_down):
    batch, seq, d = x.shape
    ctx_len = ctx.shape[1]
    t = batch * seq
    x2 = x.reshape(t, d)
    ctx2 = ctx.reshape(batch * ctx_len, d)
    tm, tn = MM_TM, MM_TN

    mod_rows = 16
    cpad = jnp.concatenate([c, c_ctx[None, :], jnp.zeros((mod_rows - batch - 1, d), F32)], axis=0)
    mods = _ada_mods(cpad, ada_w, ada_b).reshape(ada_w.shape[0], mod_rows, 6, 1, d)
    gate_spec = lambda layer, chunk: pl.BlockSpec(
        (None, None, None, 1, tn), lambda j, i: (layer, (i * tm) // seq, chunk, 0, j))
    res_spec = pl.BlockSpec((tm, tn), lambda j, i: (i, j))
    bias_spec = pl.BlockSpec((1, tn), lambda j, i: (0, j))

    h_all = _prenorm0(x2, ctx2, norm1_g[0], mods, 0, seq, batch)
    q_rot, q_pl, k_rot, v = _qkv(h_all, na_w_qkv[0], na_q_g[0], na_k_g[0], t, seq, d)
    att = _attention(q_rot, q_pl, k_rot, v, na_rpb[0], batch, seq, ctx_len)
    (x2,) = _ws_matmul(att, na_w_o[0], 0, d, _ep_residual, [x2, mods], [res_spec, gate_spec(0, 2)],
                       [jax.ShapeDtypeStruct((t, d), F32)], [res_spec], "attn_out")
    x2, h = _moe_layer(x2, norm2_g[0], mods, 0, seq, router_w, router_b,
                       moe_w_gate, moe_w_up, moe_w_down, (1, norm1_g[1]))

    width = sgu_w_uv.shape[2] // 2
    (z,) = _ws_matmul(h, sgu_w_uv[0], 0, 2 * width, _ep_bias_gelu, [sgu_b_uv[0].reshape(1, -1)], [bias_spec],
                      [jax.ShapeDtypeStruct((t, 2 * width), BF16)], [res_spec], "sgu_uv")
    gated = _sgu_gate(z, sgu_w_s[0], sgu_b_s[0], sgu_ln_g[0], sgu_ln_b[0])
    (x2,) = _ws_matmul(gated, sgu_w_out[0], 0, d, _ep_bias_residual,
                       [x2, mods, sgu_b_out[0].reshape(1, -1)], [res_spec, gate_spec(1, 2), bias_spec],
                       [jax.ShapeDtypeStruct((t, d), F32)], [res_spec], "sgu_out")
    (x2,) = _moe_layer(x2, norm2_g[1], mods, 1, seq, router_w, router_b,
                       moe_w_gate, moe_w_up, moe_w_down, None)
    return x2.reshape(batch, seq, d)
```

```python
import functools

import jax
import jax.numpy as jnp
from jax import lax
from jax.experimental import pallas as pl
from jax.experimental.pallas import tpu as pltpu

F32, BF16, I32, U32 = jnp.float32, jnp.bfloat16, jnp.int32, jnp.uint32

GRID_W = 64
NORM_EPS = 1e-6
NA_HEAD_DIM = 128
NA_KH = 8
NA_KW = 16
LOG2_E = 1.4426950408889634
NA_QSCALE = NA_HEAD_DIM ** -0.5 * LOG2_E
ROPE_BASE = 10000.0
ROPE_FREQS = NA_HEAD_DIM // 4
SGU_CHUNK = 128
SGU_GROUPS = 16
N_EXPERTS = 16
N_GROUPS = 4
EXPERTS_PER_GROUP = N_EXPERTS // N_GROUPS
TOP_K = 2
MASK_VALUE = -1e30

LANES = 128
SUBLANES = 8
VMEM_LIMIT_BYTES = 56 * 1024 * 1024
MM_TM = 1024
MM_TN = 512
MM_RC = 256
ROW_TILE = 512
MOE_TM = 512
MOE_A_TN = 512
MOE_B_TN = 4096
GATHER_ROWS = 512
COMBINE_ROWS = 256
COMBINE_CHUNK_ROWS = 16
COMBINE_CHUNK_COLS = 512
NORM_CHUNK_ROWS = 16
NORM_CHUNK_COLS = 512
DMA_CHUNK_ROWS = 16
ATTN_CTX_CHUNK = 256
ATTN_DEPTH = 4


def _cparams(*sem):
    return pltpu.CompilerParams(dimension_semantics=sem, vmem_limit_bytes=VMEM_LIMIT_BYTES)


def _dot(a, b):
    return jnp.dot(a, b, preferred_element_type=F32)


def _dot_nt(a, b, precision=None):
    return lax.dot_general(a, b, (((1,), (1,)), ((), ())), precision=precision,
                           preferred_element_type=F32)


def _rms(x, g):
    return x * lax.rsqrt(jnp.mean(x * x, axis=-1, keepdims=True) + NORM_EPS) * g


def _pack_bf16_pair(lo, hi):
    lo_b = lax.bitcast_convert_type(lo.astype(BF16).astype(F32), U32) >> 16
    hi_b = lax.bitcast_convert_type(hi.astype(BF16).astype(F32), U32) & jnp.uint32(0xFFFF0000)
    return lo_b | hi_b


def _unpack_lo(w):
    return lax.bitcast_convert_type(w << 16, F32)


def _unpack_hi(w):
    return lax.bitcast_convert_type(w & jnp.uint32(0xFFFF0000), F32)


def _ada_kernel(c_ref, w_ref, b_ref, o_ref):
    c = c_ref[...]
    a = (c * jax.nn.sigmoid(c)).astype(BF16)
    o_ref[...] = _dot(a, w_ref[...].astype(BF16)) + b_ref[...]


def _ada_mods(cpad, ada_w, ada_b):
    n_layers, d, n = ada_w.shape
    rows = cpad.shape[0]
    tn = MM_TN
    return pl.pallas_call(
        _ada_kernel,
        grid=(n_layers, n // tn),
        in_specs=[pl.BlockSpec((rows, d), lambda l, j: (0, 0)),
                  pl.BlockSpec((None, d, tn), lambda l, j: (l, 0, j)),
                  pl.BlockSpec((None, 1, tn), lambda l, j: (l, 0, j))],
        out_specs=pl.BlockSpec((None, rows, tn), lambda l, j: (l, 0, j)),
        out_shape=jax.ShapeDtypeStruct((n_layers, rows, n), F32),
        compiler_params=_cparams("arbitrary", "arbitrary"),
        name="ada_mods",
    )(cpad, ada_w, ada_b.reshape(n_layers, 1, n))


def _mod_spec(layer, chunk, row_fn):
    def idx(*g):
        return (layer, row_fn(*g), chunk, 0, 0)
    return idx


def _prenorm0_kernel(x_ref, ctx_ref, g_ref, sh_ref, sc_ref, o_ref, *, n_lat):
    i = pl.program_id(0)

    def emit(v):
        o_ref[...] = (_rms(v, g_ref[...]) * (1.0 + sc_ref[...]) + sh_ref[...]).astype(BF16)

    @pl.when(i < n_lat)
    def _():
        emit(x_ref[...])

    @pl.when(i >= n_lat)
    def _():
        emit(ctx_ref[...])


def _prenorm0(x2, ctx2, g, mods, layer, seq, ctx_row):
    t, d = x2.shape
    tc = ctx2.shape[0]
    tm = ROW_TILE
    n_lat, n_ctx = t // tm, tc // tm
    row = lambda i: jnp.where(i < n_lat, (i * tm) // seq, ctx_row)
    mspec = lambda chunk: pl.BlockSpec((None, None, None, 1, d), _mod_spec(layer, chunk, row))
    return pl.pallas_call(
        functools.partial(_prenorm0_kernel, n_lat=n_lat),
        grid=(n_lat + n_ctx,),
        in_specs=[pl.BlockSpec((tm, d), lambda i: (jnp.minimum(i, n_lat - 1), 0)),
                  pl.BlockSpec((tm, d), lambda i: (jnp.maximum(i - n_lat, 0), 0)),
                  pl.BlockSpec((1, d), lambda i: (0, 0)),
                  mspec(0), mspec(1)],
        out_specs=pl.BlockSpec((tm, d), lambda i: (i, 0)),
        out_shape=jax.ShapeDtypeStruct((t + tc, d), BF16),
        compiler_params=_cparams("arbitrary"),
        name="prenorm0",
    )(x2, ctx2, g.reshape(1, d), mods, mods)


def _ws_body(x_ref, w_ref, *refs, n_extra, n_out, epilogue, tm, rc):
    extra, outs, wb_ref = refs[:n_extra], refs[n_extra:n_extra + n_out], refs[-1]

    @pl.when(pl.program_id(1) == 0)
    def _():
        wb_ref[...] = w_ref[...].astype(BF16)

    for c in range(tm // rc):
        rows = slice(c * rc, (c + 1) * rc)
        epilogue(_dot(x_ref[rows, :], wb_ref[...]), rows, extra, outs)


def _ws_matmul(x, w, col0, n_cols, epilogue, extras, extra_specs, out_shapes, out_specs, name,
               tm=MM_TM, tn=MM_TN, m=None):
    k = x.shape[1]
    m = x.shape[0] if m is None else m
    assert m % tm == 0 and n_cols % tn == 0 and col0 % tn == 0
    jb = col0 // tn
    body = functools.partial(_ws_body, n_extra=len(extras), n_out=len(out_shapes),
                             epilogue=epilogue, tm=tm, rc=MM_RC)
    return pl.pallas_call(
        body,
        grid=(n_cols // tn, m // tm),
        in_specs=[pl.BlockSpec((tm, k), lambda j, i: (i, 0)),
                  pl.BlockSpec((k, tn), lambda j, i: (0, j + jb))] + list(extra_specs),
        out_specs=out_specs,
        out_shape=out_shapes,
        scratch_shapes=[pltpu.VMEM((k, tn), BF16)],
        compiler_params=_cparams("arbitrary", "arbitrary"),
        name=name,
    )(x, w, *extras)


def _swap32(y):
    lane = lax.broadcasted_iota(I32, y.shape, 1)
    return jnp.where((lane & 32) != 0, pltpu.roll(y, 32, 1), pltpu.roll(y, 96, 1))


def _ep_q(acc, rows, extra, outs):
    g_ref, cos_ref, sin_ref = extra
    qrot_ref, qpl_ref = outs
    cos, sin = cos_ref[rows, :], sin_ref[rows, :]
    for h in range(acc.shape[1] // NA_HEAD_DIM):
        cols = slice(h * NA_HEAD_DIM, (h + 1) * NA_HEAD_DIM)
        y = _rms(acc[:, cols], g_ref[...]) * NA_QSCALE
        qpl_ref[rows, cols] = y.astype(BF16)
        qrot_ref[rows, cols] = (y * cos + _swap32(y) * sin).astype(BF16)


def _ep_k(acc, rows, extra, outs):
    g_ref, cos_ref, sin_ref = extra
    (krot_ref,) = outs
    cos, sin = cos_ref[rows, :], sin_ref[rows, :]
    for h in range(acc.shape[1] // NA_HEAD_DIM):
        cols = slice(h * NA_HEAD_DIM, (h + 1) * NA_HEAD_DIM)
        y = _rms(acc[:, cols], g_ref[...])
        krot_ref[rows, cols] = (y * cos + _swap32(y) * sin).astype(BF16)


def _ep_cast(acc, rows, extra, outs):
    outs[0][rows, :] = acc.astype(BF16)


def _ep_residual(acc, rows, extra, outs):
    x_ref, gate_ref = extra
    outs[0][rows, :] = x_ref[rows, :] + gate_ref[...] * acc


def _ep_bias_residual(acc, rows, extra, outs):
    x_ref, gate_ref, b_ref = extra
    outs[0][rows, :] = x_ref[rows, :] + gate_ref[...] * (acc + b_ref[...])


def _ep_bias_gelu(acc, rows, extra, outs):
    (b_ref,) = extra
    a = acc + b_ref[...]
    outs[0][rows, :] = (0.5 * a * (1.0 + lax.erf(a * (2.0 ** -0.5)))).astype(BF16)


def _rope_tables(seq, extra_rows):
    t = jnp.arange(seq, dtype=I32)
    pos = jnp.stack([t // GRID_W, t % GRID_W], axis=-1).astype(F32)
    inv_freq = ROPE_BASE ** (-jnp.arange(ROPE_FREQS, dtype=F32) / ROPE_FREQS)
    ang = pos[:, :, None] * inv_freq
    cos, sin = jnp.cos(ang), jnp.sin(ang)
    cos = jnp.stack([cos, cos], axis=2).reshape(seq, NA_HEAD_DIM)
    sin = jnp.stack([-sin, sin], axis=2).reshape(seq, NA_HEAD_DIM)
    cos = jnp.concatenate([cos, jnp.ones((extra_rows, NA_HEAD_DIM), F32)], axis=0)
    sin = jnp.concatenate([sin, jnp.zeros((extra_rows, NA_HEAD_DIM), F32)], axis=0)
    return cos, sin


def _qkv(h_all, w_qkv, q_g, k_g, t, seq, d):
    tm, tn = MM_TM, MM_TN
    m_all = h_all.shape[0]
    n_lat, per_seq = t // tm, seq // tm
    cos, sin = _rope_tables(seq, tm)
    g_spec = pl.BlockSpec((1, NA_HEAD_DIM), lambda j, i: (0, 0))
    tab_idx = lambda j, i: (jnp.where(i < n_lat, i % per_seq, per_seq), 0)
    tab_spec = pl.BlockSpec((tm, NA_HEAD_DIM), tab_idx)
    out_spec = pl.BlockSpec((tm, tn), lambda j, i: (i, j))
    q_rot, q_pl = _ws_matmul(
        h_all, w_qkv, 0, d, _ep_q, [q_g.reshape(1, -1), cos, sin], [g_spec, tab_spec, tab_spec],
        [jax.ShapeDtypeStruct((t, d), BF16)] * 2, [out_spec, out_spec], "qkv_q", m=t)
    (k_rot,) = _ws_matmul(
        h_all, w_qkv, d, d, _ep_k, [k_g.reshape(1, -1), cos, sin], [g_spec, tab_spec, tab_spec],
        [jax.ShapeDtypeStruct((m_all, d), BF16)], [out_spec], "qkv_k")
    (v,) = _ws_matmul(
        h_all, w_qkv, 2 * d, d, _ep_cast, [], [],
        [jax.ShapeDtypeStruct((m_all, d), BF16)], [out_spec], "qkv_v")
    return q_rot, q_pl, k_rot, v


def _bias_tables(rpb):
    n_heads, n_dr, _ = rpb.shape
    q = jnp.arange(GRID_W, dtype=I32)[:, None]
    kc = jnp.arange(GRID_W, dtype=I32)[None, :]
    dc = jnp.clip(kc - q + NA_KW - 1, 0, 2 * NA_KW - 2)
    cs = jnp.clip(q - NA_KW // 2, 0, GRID_W - NA_KW)
    in_win = (kc >= cs) & (kc < cs + NA_KW)
    c = jnp.where(in_win[None, None], rpb[:, :, dc] * LOG2_E, MASK_VALUE)
    c = c.transpose(0, 2, 1, 3).reshape(n_heads, GRID_W, n_dr * GRID_W).astype(F32)
    width = (n_dr + 2) * GRID_W
    c = jnp.pad(c, ((0, 0), (0, 0), (0, width - n_dr * GRID_W)))
    return c[:, :, :width - GRID_W], c[:, :, GRID_W:]


def _pipelined(n, start, finish, depth):
    pending = [start(i) for i in range(min(depth, n))]
    for i in range(n):
        s = pending.pop(0)
        if i + depth < n:
            pending.append(start(i + depth))
        finish(i, s)


def _attn_kernel(q_ref, qp_ref, k_ref, v_ref, kc_ref, vc_ref, c0_ref, c1_ref, o_ref,
                 mc_ref, lc_ref, oc_ref, *, rows):
    win = NA_KH * GRID_W
    half = NA_KH // 2
    seq = rows * GRID_W
    hd = q_ref.shape[1]

    def ctx_scores(c):
        return _dot_nt(qp_ref[c * ATTN_CTX_CHUNK:(c + 1) * ATTN_CTX_CHUNK, :], kc_ref[...])

    def ctx_finish(c, s):
        sl = slice(c * ATTN_CTX_CHUNK, (c + 1) * ATTN_CTX_CHUNK)
        m = jnp.max(s, axis=-1, keepdims=True)
        p = jnp.exp2(s - m)
        mc_ref[sl, :] = jnp.broadcast_to(m, (ATTN_CTX_CHUNK, hd))
        lc_ref[sl, :] = jnp.broadcast_to(jnp.sum(p, axis=-1, keepdims=True), (ATTN_CTX_CHUNK, hd))
        oc_ref[sl, :] = _dot(p.astype(BF16), vc_ref[...])

    _pipelined(seq // ATTN_CTX_CHUNK, ctx_scores, ctx_finish, 2)

    def key_start(r):
        return min(max(r - half, 0), rows - NA_KH)

    def loc_scores(r):
        k0 = key_start(r) * GRID_W
        off = (NA_KH - 1 - (r - key_start(r))) * GRID_W
        if off % LANES == 0:
            bias = c0_ref[:, off:off + win]
        else:
            bias = c1_ref[:, off - GRID_W:off - GRID_W + win]
        return _dot_nt(q_ref[r * GRID_W:(r + 1) * GRID_W, :], k_ref[k0:k0 + win, :]) + bias

    def loc_finish(r, s):
        sl = slice(r * GRID_W, (r + 1) * GRID_W)
        k0 = key_start(r) * GRID_W
        m_loc = jnp.max(s, axis=-1, keepdims=True)
        p = jnp.exp2(s - m_loc)
        l_loc = jnp.sum(p, axis=-1, keepdims=True)
        o_loc = _dot(p.astype(BF16), v_ref[k0:k0 + win, :])
        m_ctx = mc_ref[sl, :]
        m = jnp.maximum(m_loc, m_ctx)
        a = jnp.exp2(m_loc - m)
        b = jnp.exp2(m_ctx - m)
        denom = a * l_loc + b * lc_ref[sl, :]
        o_ref[sl, :] = ((a * o_loc + b * oc_ref[sl, :]) / denom).astype(BF16)

    _pipelined(rows, loc_scores, loc_finish, ATTN_DEPTH)


def _attention(q_rot, q_pl, k_rot, v, rpb, batch, seq, ctx_len):
    t, d = q_rot.shape
    n_heads = d // NA_HEAD_DIM
    rows = seq // GRID_W
    c0, c1 = _bias_tables(rpb)
    ctx_blk0 = t // ctx_len
    lat = pl.BlockSpec((seq, NA_HEAD_DIM), lambda h, b: (b, h))
    cx = pl.BlockSpec((ctx_len, NA_HEAD_DIM), lambda h, b: (ctx_blk0 + b, h))
    tab = pl.BlockSpec((None, GRID_W, c0.shape[2]), lambda h, b: (h, 0, 0))
    return pl.pallas_call(
        functools.partial(_attn_kernel, rows=rows),
        grid=(n_heads, batch),
        in_specs=[lat, lat, lat, lat, cx, cx, tab, tab],
        out_specs=lat,
        out_shape=jax.ShapeDtypeStruct((t, d), BF16),
        scratch_shapes=[pltpu.VMEM((seq, NA_HEAD_DIM), F32)] * 3,
        compiler_params=_cparams("arbitrary", "arbitrary"),
        name="attention",
    )(q_rot, q_pl, k_rot, v, k_rot, v, c0, c1)


def _router_logits(wrt_ref, h1_ref, h2_ref):
    w = wrt_ref[...]
    w1 = w.astype(BF16)
    w2 = (w - w1.astype(F32)).astype(BF16)
    a = _dot_nt(jnp.concatenate([w1, w2], axis=0), h1_ref[...])
    b = _dot_nt(w1, h2_ref[...])
    return a[:N_EXPERTS] + (a[N_EXPERTS:] + b)


def _route(logits, rb_ref, carry_ref, eidx_ref, ew_ref, rank_ref, cnt_ref):
    tm = logits.shape[1]
    scores = jax.nn.sigmoid(logits)
    sel = scores + rb_ref[...]

    def top2(vals):
        def first_max(vs):
            m = functools.reduce(jnp.maximum, vs)
            idx = jnp.full(m.shape, len(vs) - 1, I32)
            for k in range(len(vs) - 2, -1, -1):
                idx = jnp.where(vs[k] == m, k, idx)
            return m, idx
        m1, i1 = first_max(vals)
        m2, i2 = first_max([jnp.where(i1 == k, -jnp.inf, v) for k, v in enumerate(vals)])
        return m1, i1, m2, i2

    grp = []
    for g in range(N_GROUPS):
        vals = [sel[g * EXPERTS_PER_GROUP + k:g * EXPERTS_PER_GROUP + k + 1, :]
                for k in range(EXPERTS_PER_GROUP)]
        grp.append(top2(vals))
    gsum = [m1 + m2 for m1, _, m2, _ in grp]
    gmax = functools.reduce(jnp.maximum, gsum)
    g_idx = jnp.full(gmax.shape, N_GROUPS - 1, I32)
    for g in range(N_GROUPS - 2, -1, -1):
        g_idx = jnp.where(gsum[g] == gmax, g, g_idx)
    i1 = grp[N_GROUPS - 1][1]
    i2 = grp[N_GROUPS - 1][3]
    for g in range(N_GROUPS - 2, -1, -1):
        i1 = jnp.where(g_idx == g, grp[g][1], i1)
        i2 = jnp.where(g_idx == g, grp[g][3], i2)
    e1 = g_idx * EXPERTS_PER_GROUP + i1
    e2 = g_idx * EXPERTS_PER_GROUP + i2
    e_iota = lax.broadcasted_iota(I32, (N_EXPERTS, tm), 0)
    hit1, hit2 = e_iota == e1, e_iota == e2
    s1 = jnp.sum(jnp.where(hit1, scores, 0.0), axis=0, keepdims=True)
    s2 = jnp.sum(jnp.where(hit2, scores, 0.0), axis=0, keepdims=True)
    tot = s1 + s2
    eidx_ref[0:1, :] = e1
    eidx_ref[1:2, :] = e2
    ew_ref[0:1, :] = s1 / tot
    ew_ref[1:2, :] = s2 / tot
    onehot = (hit1 | hit2).astype(BF16)
    upper = (lax.broadcasted_iota(I32, (tm, tm), 0) < lax.broadcasted_iota(I32, (tm, tm), 1)).astype(BF16)
    before = _dot(onehot, upper) + carry_ref[...]
    rank_ref[0:1, :] = jnp.sum(jnp.where(hit1, before, 0.0), axis=0, keepdims=True).astype(I32)
    rank_ref[1:2, :] = jnp.sum(jnp.where(hit2, before, 0.0), axis=0, keepdims=True).astype(I32)
    carry_ref[...] += jnp.sum(onehot.astype(F32), axis=1, keepdims=True)
    cnt_ref[...] = jnp.broadcast_to(carry_ref[...], cnt_ref.shape)


def _prenorm_router_kernel(x_ref, g_ref, sh_ref, sc_ref, wrt_ref, rb_ref,
                           hp_ref, eidx_ref, ew_ref, rank_ref, cnt_ref, carry_ref, h1_ref, h2_ref, gs_ref):
    @pl.when(pl.program_id(0) == 0)
    def _():
        carry_ref[...] = jnp.zeros_like(carry_ref)

    tm, d = x_ref.shape
    half = d // 2
    rc, cw = NORM_CHUNK_ROWS, NORM_CHUNK_COLS
    gs_ref[...] = g_ref[...] * (1.0 + sc_ref[...])

    def chunk(ci, carry):
        rows = pl.ds(pl.multiple_of(ci * rc, rc), rc)
        ss = jnp.zeros((rc, cw), F32)
        for cc in range(d // cw):
            xv = x_ref[rows, cc * cw:(cc + 1) * cw]
            ss = ss + xv * xv
        inv = lax.rsqrt(jnp.sum(ss, axis=-1, keepdims=True) / d + NORM_EPS)

        def modulated_bits(cols):
            h = x_ref[rows, cols] * inv * gs_ref[:, cols] + sh_ref[:, cols]
            h1 = h.astype(BF16)
            hb = h1.astype(F32)
            h1_ref[rows, cols] = h1
            h2_ref[rows, cols] = (h - hb).astype(BF16)
            return lax.bitcast_convert_type(hb, U32)

        for cc in range(half // cw):
            lo = modulated_bits(slice(cc * cw, (cc + 1) * cw))
            hi = modulated_bits(slice(half + cc * cw, half + (cc + 1) * cw))
            hp_ref[rows, cc * cw:(cc + 1) * cw] = (lo >> 16) | hi
        return carry

    lax.fori_loop(0, tm // rc, chunk, 0, unroll=4)
    _route(_router_logits(wrt_ref, h1_ref, h2_ref), rb_ref, carry_ref, eidx_ref, ew_ref, rank_ref, cnt_ref)


def _router_out(t, d, tm):
    shapes = [jax.ShapeDtypeStruct((t, d // 2), U32),
              jax.ShapeDtypeStruct((TOP_K, t), I32),
              jax.ShapeDtypeStruct((TOP_K, t), F32),
              jax.ShapeDtypeStruct((TOP_K, t), I32),
              jax.ShapeDtypeStruct((N_EXPERTS, LANES), F32)]
    specs = [pl.BlockSpec((tm, d // 2), lambda i: (i, 0)),
             pl.BlockSpec((TOP_K, tm), lambda i: (0, i)),
             pl.BlockSpec((TOP_K, tm), lambda i: (0, i)),
             pl.BlockSpec((TOP_K, tm), lambda i: (0, i)),
             pl.BlockSpec((N_EXPERTS, LANES), lambda i: (0, 0))]
    return shapes, specs


def _prenorm_router(x2, g, mods, layer, seq, router_w, router_b):
    t, d = x2.shape
    tm = ROW_TILE
    row = lambda i: (i * tm) // seq
    mspec = lambda chunk: pl.BlockSpec((None, None, None, 1, d), _mod_spec(layer, chunk, row))
    shapes, specs = _router_out(t, d, tm)
    return pl.pallas_call(
        _prenorm_router_kernel,
        grid=(t // tm,),
        in_specs=[pl.BlockSpec((tm, d), lambda i: (i, 0)),
                  pl.BlockSpec((1, d), lambda i: (0, 0)),
                  mspec(3), mspec(4),
                  pl.BlockSpec((N_EXPERTS, d), lambda i: (0, 0)),
                  pl.BlockSpec((N_EXPERTS, 1), lambda i: (0, 0))],
        out_specs=specs,
        out_shape=shapes,
        scratch_shapes=[pltpu.VMEM((N_EXPERTS, 1), F32), pltpu.VMEM((tm, d), BF16), pltpu.VMEM((tm, d), BF16),
                        pltpu.VMEM((1, d), F32)],
        compiler_params=_cparams("arbitrary"),
        name="prenorm_router",
    )(x2, g.reshape(1, d), mods, mods, router_w.T, router_b.reshape(N_EXPERTS, 1))


def _routing_plan(eidx, rank, cnt, t):
    assert GATHER_ROWS % MOE_TM == 0
    max_rows = (TOP_K * t + N_EXPERTS * (MOE_TM - 1)) // MOE_TM * MOE_TM
    n_rows = -(-max_rows // GATHER_ROWS) * GATHER_ROWS
    n_tiles = n_rows // MOE_TM
    counts = cnt[:, 0].astype(I32)
    padded = ((counts + MOE_TM - 1) // MOE_TM) * MOE_TM
    ends = jnp.cumsum(padded)
    offs = ends - padded
    hit = eidx[None] == jnp.arange(N_EXPERTS, dtype=I32)[:, None, None]
    dest = jnp.sum(jnp.where(hit, offs[:, None, None], 0), axis=0) + rank
    n_valid = (ends[-1] // MOE_TM).astype(I32)
    tile_start = jnp.arange(n_tiles, dtype=I32) * MOE_TM
    tile_expert = jnp.sum((tile_start[:, None] >= ends[None, :]).astype(I32), axis=1)
    last = jnp.minimum(jnp.maximum(n_valid - 1, 0), n_tiles - 1)
    tile_expert = jnp.where(jnp.arange(n_tiles) < n_valid, tile_expert, tile_expert[last])
    tile_expert = jnp.minimum(tile_expert, N_EXPERTS - 1).astype(I32)
    tok = jnp.tile(jnp.arange(t, dtype=I32), TOP_K)
    src_tok = (jnp.arange(n_rows, dtype=I32) % t).at[dest.reshape(-1)].set(tok)
    return dest, src_tok, tile_expert, n_valid.reshape(1), n_tiles


def _row_copy(src_tiles, row, dst_tiles, tile, sub, sem):
    src = src_tiles.at[lax.shift_right_logical(row, 3), pl.ds(row & (SUBLANES - 1), 1), :]
    return pltpu.make_async_copy(src, dst_tiles.at[tile, pl.ds(sub, 1), :], sem)


def _all_rows_wait(src_tiles, dst_tiles, sem):
    pltpu.make_async_copy(src_tiles.at[pl.ds(0, dst_tiles.shape[0])], dst_tiles, sem).wait()


def _issue_rows(src_tiles, idx_ref, r0, count, dst_tiles, sem):
    tile0 = lax.shift_right_logical(r0, 3)
    for k in range(count):
        _row_copy(src_tiles, idx_ref[0, r0 + k], dst_tiles, tile0 + k // SUBLANES, k % SUBLANES,
                  sem).start(priority=k % 2)


def _load_rows(buf_ref, slot, r0, count, cols=slice(None)):
    w = buf_ref[slot, pl.ds(lax.shift_right_logical(r0, 3), count // SUBLANES), :, cols]
    return w.reshape(count, w.shape[-1])


def _gather_kernel(nrows_ref, src0_ref, srcn_ref, hp_hbm, o_ref, buf_ref, sem):
    i = pl.program_id(0)
    tg, half = o_ref.shape[0], buf_ref.shape[-1]
    rc = DMA_CHUNK_ROWS
    slot = i % 2
    cur_valid = i * tg < nrows_ref[0]
    nxt_valid = (i + 1) * tg < nrows_ref[0]

    @pl.when(i == 0)
    def _():
        def body(c, carry):
            _issue_rows(hp_hbm, src0_ref, pl.multiple_of(c * rc, rc), rc, buf_ref.at[0], sem.at[0])
            return carry
        lax.fori_loop(0, tg // rc, body, 0)

    @pl.when(cur_valid)
    def _():
        _all_rows_wait(hp_hbm, buf_ref.at[slot], sem.at[slot])

    def unpack(r0):
        rows = pl.ds(r0, rc)
        w = _load_rows(buf_ref, slot, r0, rc)
        o_ref[rows, :half] = _unpack_lo(w).astype(BF16)
        o_ref[rows, half:] = _unpack_hi(w).astype(BF16)

    @pl.when(jnp.logical_and(cur_valid, nxt_valid))
    def _():
        def body(c, carry):
            r0 = pl.multiple_of(c * rc, rc)
            unpack(r0)
            _issue_rows(hp_hbm, srcn_ref, r0, rc, buf_ref.at[1 - slot], sem.at[1 - slot])
            return carry
        lax.fori_loop(0, tg // rc, body, 0)

    @pl.when(jnp.logical_and(cur_valid, jnp.logical_not(nxt_valid)))
    def _():
        def body(c, carry):
            unpack(pl.multiple_of(c * rc, rc))
            return carry
        lax.fori_loop(0, tg // rc, body, 0)

    @pl.when(jnp.logical_not(cur_valid))
    def _():
        o_ref[...] = jnp.zeros_like(o_ref)


def _gather_rows(hp, src_tok, n_valid_rows):
    n_rows = src_tok.shape[0]
    half = hp.shape[1]
    tg = GATHER_ROWS
    n_t = n_rows // tg
    src_tiles = src_tok.reshape(n_t, 1, tg)
    return pl.pallas_call(
        _gather_kernel,
        grid_spec=pltpu.PrefetchScalarGridSpec(
            num_scalar_prefetch=1,
            grid=(n_t,),
            in_specs=[pl.BlockSpec((None, 1, tg), lambda i, n: (0, 0, 0), memory_space=pltpu.SMEM),
                      pl.BlockSpec((None, 1, tg), lambda i, n: (jnp.minimum(i + 1, n_t - 1), 0, 0),
                                   memory_space=pltpu.SMEM),
                      pl.BlockSpec(memory_space=pl.ANY)],
            out_specs=pl.BlockSpec((tg, 2 * half), lambda i, n: (i, 0)),
            scratch_shapes=[pltpu.VMEM((2, tg // SUBLANES, SUBLANES, half), U32),
                            pltpu.SemaphoreType.DMA((2,))]),
        out_shape=jax.ShapeDtypeStruct((n_rows, 2 * half), BF16),
        compiler_params=_cparams("arbitrary"),
        name="moe_gather",
    )(n_valid_rows, src_tiles, src_tiles, hp.reshape(hp.shape[0] // SUBLANES, SUBLANES, half))


def _expert_changed(te_ref):
    i = pl.program_id(1)
    return jnp.logical_or(i == 0, te_ref[i] != te_ref[jnp.maximum(i - 1, 0)])


def _moe_a_kernel(te_ref, nv_ref, x_ref, wg_ref, wu_ref, o_ref, wgb_ref, wub_ref):
    @pl.when(_expert_changed(te_ref))
    def _():
        wgb_ref[...] = wg_ref[...].astype(BF16)
        wub_ref[...] = wu_ref[...].astype(BF16)

    @pl.when(pl.program_id(1) < nv_ref[0])
    def _():
        x = x_ref[...]
        gate = _dot(x, wgb_ref[...])
        up = _dot(x, wub_ref[...])
        o_ref[...] = (gate * jax.nn.sigmoid(gate) * up).astype(BF16)

    @pl.when(pl.program_id(1) >= nv_ref[0])
    def _():
        o_ref[...] = jnp.zeros_like(o_ref)


def _moe_b_kernel(te_ref, nv_ref, a_ref, wd_ref, o_ref, wdb_ref):
    @pl.when(_expert_changed(te_ref))
    def _():
        wdb_ref[...] = wd_ref[...].astype(BF16)

    @pl.when(pl.program_id(1) < nv_ref[0])
    def _():
        y = _dot(a_ref[...], wdb_ref[...])
        half = y.shape[1] // 2
        o_ref[...] = _pack_bf16_pair(y[:, :half], y[:, half:])

    @pl.when(pl.program_id(1) >= nv_ref[0])
    def _():
        o_ref[...] = jnp.zeros_like(o_ref)


def _moe_experts(xs, tile_expert, n_valid, w_gate, w_up, w_down, layer):
    n_rows, d = xs.shape
    f = w_gate.shape[3]
    n_tiles = n_rows // MOE_TM
    row_idx = lambda j, i, te, nv: (jnp.minimum(i, nv[0] - 1), 0)
    w_idx = lambda j, i, te, nv: (layer, te[i], 0, j)
    a = pl.pallas_call(
        _moe_a_kernel,
        grid_spec=pltpu.PrefetchScalarGridSpec(
            num_scalar_prefetch=2,
            grid=(f // MOE_A_TN, n_tiles),
            in_specs=[pl.BlockSpec((MOE_TM, d), row_idx),
                      pl.BlockSpec((None, None, d, MOE_A_TN), w_idx),
                      pl.BlockSpec((None, None, d, MOE_A_TN), w_idx)],
            out_specs=pl.BlockSpec((MOE_TM, MOE_A_TN), lambda j, i, te, nv: (i, j)),
            scratch_shapes=[pltpu.VMEM((d, MOE_A_TN), BF16)] * 2),
        out_shape=jax.ShapeDtypeStruct((n_rows, f), BF16),
        compiler_params=_cparams("arbitrary", "arbitrary"),
        name="moe_gate_up",
    )(tile_expert, n_valid, xs, w_gate, w_up)
    ys = pl.pallas_call(
        _moe_b_kernel,
        grid_spec=pltpu.PrefetchScalarGridSpec(
            num_scalar_prefetch=2,
            grid=(d // MOE_B_TN, n_tiles),
            in_specs=[pl.BlockSpec((MOE_TM, f), row_idx),
                      pl.BlockSpec((None, None, f, MOE_B_TN), w_idx)],
            out_specs=pl.BlockSpec((MOE_TM, MOE_B_TN // 2), lambda j, i, te, nv: (i, j)),
            scratch_shapes=[pltpu.VMEM((f, MOE_B_TN), BF16)]),
        out_shape=jax.ShapeDtypeStruct((n_rows, d // 2), U32),
        compiler_params=_cparams("arbitrary", "arbitrary"),
        name="moe_down",
    )(tile_expert, n_valid, a, w_down)
    return ys


def _combine_kernel(dest0_ref, destn_ref, ys_hbm, x_ref, ew_ref, gate_ref, *rest, with_norm):
    if with_norm:
        g_ref, sh_ref, sc_ref, xo_ref, ho_ref, buf_ref, sem = rest
    else:
        xo_ref, buf_ref, sem = rest
    i = pl.program_id(0)
    slot = i % 2
    tc, d = x_ref.shape
    n = TOP_K * tc
    rc, cw = COMBINE_CHUNK_ROWS, COMBINE_CHUNK_COLS
    n_chunks = tc // rc
    per_chunk = n // n_chunks
    q = MOE_B_TN // 2

    @pl.when(i == 0)
    def _():
        def body(c, carry):
            r0 = pl.multiple_of(c * per_chunk, per_chunk)
            _issue_rows(ys_hbm, dest0_ref, r0, per_chunk, buf_ref.at[0], sem.at[0])
            return carry
        lax.fori_loop(0, n_chunks, body, 0)

    _all_rows_wait(ys_hbm, buf_ref.at[slot], sem.at[slot])

    def chunk(ci, carry, prefetch):
        r0 = pl.multiple_of(ci * rc, rc)
        rows, rows2 = pl.ds(r0, rc), pl.ds(r0 + tc, rc)
        a1, a2 = ew_ref[rows, 0:1], ew_ref[rows, 1:2]
        ss = jnp.zeros((rc, cw), F32)
        for j in range(d // (2 * q)):
            for cc in range(q // cw):
                pcols = slice(j * q + cc * cw, j * q + (cc + 1) * cw)
                w1 = _load_rows(buf_ref, slot, r0, rc, pcols)
                w2 = _load_rows(buf_ref, slot, r0 + tc, rc, pcols)
                for part, unpack in ((0, _unpack_lo), (1, _unpack_hi)):
                    c0 = j * 2 * q + part * q + cc * cw
                    cols = slice(c0, c0 + cw)
                    xo = x_ref[rows, cols] + gate_ref[:, cols] * (a1 * unpack(w1) + a2 * unpack(w2))
                    xo_ref[rows, cols] = xo
                    if with_norm:
                        ss = ss + xo * xo
        if with_norm:
            inv = lax.rsqrt(jnp.sum(ss, axis=-1, keepdims=True) / d + NORM_EPS)
            for cc in range(d // cw):
                cols = slice(cc * cw, (cc + 1) * cw)
                y = xo_ref[rows, cols] * inv * g_ref[:, cols]
                ho_ref[rows, cols] = (y * (1.0 + sc_ref[:, cols]) + sh_ref[:, cols]).astype(BF16)
        if prefetch:
            p0 = pl.multiple_of(ci * per_chunk, per_chunk)
            _issue_rows(ys_hbm, destn_ref, p0, per_chunk, buf_ref.at[1 - slot], sem.at[1 - slot])
        return carry

    has_next = i + 1 < pl.num_programs(0)

    @pl.when(has_next)
    def _():
        lax.fori_loop(0, n_chunks, functools.partial(chunk, prefetch=True), 0, unroll=2)

    @pl.when(jnp.logical_not(has_next))
    def _():
        lax.fori_loop(0, n_chunks, functools.partial(chunk, prefetch=False), 0, unroll=2)


def _combine(x2, ys, dest, ew, mods, layer, seq, norm=None):
    t, d = x2.shape
    tc = COMBINE_ROWS
    n_t = t // tc
    dest_tiles = dest.reshape(TOP_K, n_t, tc).transpose(1, 0, 2).reshape(n_t, 1, TOP_K * tc)
    row = lambda i: (i * tc) // seq
    mspec = lambda lay, chunk: pl.BlockSpec((None, None, None, 1, d), _mod_spec(lay, chunk, row))
    in_specs = [pl.BlockSpec((None, 1, TOP_K * tc), lambda i: (0, 0, 0), memory_space=pltpu.SMEM),
                pl.BlockSpec((None, 1, TOP_K * tc), lambda i: (jnp.minimum(i + 1, n_t - 1), 0, 0),
                             memory_space=pltpu.SMEM),
                pl.BlockSpec(memory_space=pl.ANY),
                pl.BlockSpec((tc, d), lambda i: (i, 0)),
                pl.BlockSpec((tc, TOP_K), lambda i: (i, 0)),
                mspec(layer, 5)]
    args = [dest_tiles, dest_tiles, ys.reshape(ys.shape[0] // SUBLANES, SUBLANES, d // 2), x2, ew.T, mods]
    out_shapes = [jax.ShapeDtypeStruct((t, d), F32)]
    out_specs = [pl.BlockSpec((tc, d), lambda i: (i, 0))]
    if norm is not None:
        next_layer, g = norm
        in_specs += [pl.BlockSpec((1, d), lambda i: (0, 0)), mspec(next_layer, 0), mspec(next_layer, 1)]
        args += [g.reshape(1, d), mods, mods]
        out_shapes.append(jax.ShapeDtypeStruct((t, d), BF16))
        out_specs.append(pl.BlockSpec((tc, d), lambda i: (i, 0)))
    return pl.pallas_call(
        functools.partial(_combine_kernel, with_norm=norm is not None),
        grid=(n_t,),
        in_specs=in_specs,
        out_specs=out_specs,
        out_shape=out_shapes,
        scratch_shapes=[pltpu.VMEM((2, TOP_K * tc // SUBLANES, SUBLANES, d // 2), U32),
                        pltpu.SemaphoreType.DMA((2,))],
        compiler_params=_cparams("arbitrary"),
        name="moe_combine",
    )(*args)


def _moe_layer(x2, norm_g, mods, layer, seq, router_w, router_b, w_gate, w_up, w_down, next_norm):
    t = x2.shape[0]
    hp, eidx, ew, rank, cnt = _prenorm_router(x2, norm_g, mods, layer, seq, router_w, router_b)
    dest, src_tok, tile_expert, n_valid, _ = _routing_plan(eidx, rank, cnt, t)
    xs = _gather_rows(hp, src_tok, n_valid * MOE_TM)
    ys = _moe_experts(xs, tile_expert, n_valid, w_gate, w_up, w_down, layer)
    return _combine(x2, ys, dest, ew, mods, layer, seq, next_norm)


def _sgu_kernel(z_ref, ws_ref, bs_ref, g_ref, b_ref, o_ref):
    width = o_ref.shape[1]
    gdim = width // SGU_GROUPS
    for c in range(z_ref.shape[0] // SGU_CHUNK):
        rows = slice(c * SGU_CHUNK, (c + 1) * SGU_CHUNK)
        v = z_ref[rows, width:].astype(F32)
        mu = jnp.mean(v, axis=-1, keepdims=True)
        vc = v - mu
        var = jnp.mean(vc * vc, axis=-1, keepdims=True)
        vn = (vc * lax.rsqrt(var + NORM_EPS) * g_ref[...] + b_ref[...]).astype(BF16)
        for g in range(SGU_GROUPS):
            cols = slice(g * gdim, (g + 1) * gdim)
            s = _dot(ws_ref[g].astype(BF16), vn[:, cols]) + bs_ref[:, g:g + 1]
            o_ref[rows, cols] = (z_ref[rows, cols].astype(F32) * s).astype(BF16)


def _sgu_gate(z, w_s, b_s, ln_g, ln_b):
    t, two_w = z.shape
    width = two_w // 2
    tm = ROW_TILE
    return pl.pallas_call(
        _sgu_kernel,
        grid=(t // tm,),
        in_specs=[pl.BlockSpec((tm, two_w), lambda i: (i, 0)),
                  pl.BlockSpec((SGU_GROUPS, SGU_CHUNK, SGU_CHUNK), lambda i: (0, 0, 0)),
                  pl.BlockSpec((SGU_CHUNK, SGU_GROUPS), lambda i: (0, 0)),
                  pl.BlockSpec((1, width), lambda i: (0, 0)),
                  pl.BlockSpec((1, width), lambda i: (0, 0))],
        out_specs=pl.BlockSpec((tm, width), lambda i: (i, 0)),
        out_shape=jax.ShapeDtypeStruct((t, width), BF16),
        compiler_params=_cparams("arbitrary"),
        name="sgu_gate",
    )(z, w_s, b_s.T, ln_g.reshape(1, width), ln_b.reshape(1, width))


def kernel(x, c, ctx, c_ctx, ada_w, ada_b, norm1_g, norm2_g, na_w_qkv, na_q_g, na_k_g, na_rpb, na_w_o,
           sgu_w_uv, sgu_b_uv, sgu_ln_g, sgu_ln_b, sgu_w_s, sgu_b_s, sgu_w_out, sgu_b_out,
           router_w, router_b, moe_w_gate, moe_w_up, moe_w_down):
    batch, seq, d = x.shape
    ctx_len = ctx.shape[1]
    t = batch * seq
    x2 = x.reshape(t, d)
    ctx2 = ctx.reshape(batch * ctx_len, d)
    tm, tn = MM_TM, MM_TN

    mod_rows = 16
    cpad = jnp.concatenate([c, c_ctx[None, :], jnp.zeros((mod_rows - batch - 1, d), F32)], axis=0)
    mods = _ada_mods(cpad, ada_w, ada_b).reshape(ada_w.shape[0], mod_rows, 6, 1, d)
    gate_spec = lambda layer, chunk: pl.BlockSpec(
        (None, None, None, 1, tn), lambda j, i: (layer, (i * tm) // seq, chunk, 0, j))
    res_spec = pl.BlockSpec((tm, tn), lambda j, i: (i, j))
    bias_spec = pl.BlockSpec((1, tn), lambda j, i: (0, j))

    h_all = _prenorm0(x2, ctx2, norm1_g[0], mods, 0, seq, batch)
    q_rot, q_pl, k_rot, v = _qkv(h_all, na_w_qkv[0], na_q_g[0], na_k_g[0], t, seq, d)
    att = _attention(q_rot, q_pl, k_rot, v, na_rpb[0], batch, seq, ctx_len)
    (x2,) = _ws_matmul(att, na_w_o[0], 0, d, _ep_residual, [x2, mods], [res_spec, gate_spec(0, 2)],
                       [jax.ShapeDtypeStruct((t, d), F32)], [res_spec], "attn_out")
    x2, h = _moe_layer(x2, norm2_g[0], mods, 0, seq, router_w, router_b,
                       moe_w_gate, moe_w_up, moe_w_down, (1, norm1_g[1]))

    width = sgu_w_uv.shape[2] // 2
    (z,) = _ws_matmul(h, sgu_w_uv[0], 0, 2 * width, _ep_bias_gelu, [sgu_b_uv[0].reshape(1, -1)], [bias_spec],
                      [jax.ShapeDtypeStruct((t, 2 * width), BF16)], [res_spec], "sgu_uv")
    gated = _sgu_gate(z, sgu_w_s[0], sgu_b_s[0], sgu_ln_g[0], sgu_ln_b[0])
    (x2,) = _ws_matmul(gated, sgu_w_out[0], 0, d, _ep_bias_residual,
                       [x2, mods, sgu_b_out[0].reshape(1, -1)], [res_spec, gate_spec(1, 2), bias_spec],
                       [jax.ShapeDtypeStruct((t, d), F32)], [res_spec], "sgu_out")
    (x2,) = _moe_layer(x2, norm2_g[1], mods, 1, seq, router_w, router_b,
                       moe_w_gate, moe_w_up, moe_w_down, None)
    return x2.reshape(batch, seq, d)
```

```python
import functools

import jax
import jax.numpy as jnp
from jax import lax
from jax.experimental import pallas as pl
from jax.experimental.pallas import tpu as pltpu

F32, BF16, I32, U32 = jnp.float32, jnp.bfloat16, jnp.int32, jnp.uint32

GRID_W = 64
NORM_EPS = 1e-6
NA_HEAD_DIM = 128
NA_KH = 8
NA_KW = 16
LOG2_E = 1.4426950408889634
NA_QSCALE = NA_HEAD_DIM ** -0.5 * LOG2_E
ROPE_BASE = 10000.0
ROPE_FREQS = NA_HEAD_DIM // 4
SGU_CHUNK = 128
SGU_GROUPS = 16
N_EXPERTS = 16
N_GROUPS = 4
EXPERTS_PER_GROUP = N_EXPERTS // N_GROUPS
TOP_K = 2
MASK_VALUE = -1e30

LANES = 128
SUBLANES = 8
VMEM_LIMIT_BYTES = 56 * 1024 * 1024
MM_TM = 1024
MM_TN = 512
MM_RC = 128
ROW_TILE = 512
MOE_TM = 512
MOE_A_TN = 512
MOE_B_TN = 4096
GATHER_ROWS = 512
COMBINE_ROWS = 256
COMBINE_CHUNK_ROWS = 16
COMBINE_CHUNK_COLS = 512
NORM_CHUNK_ROWS = 16
NORM_CHUNK_COLS = 512
DMA_CHUNK_ROWS = 16
ATTN_CTX_CHUNK = 256
ATTN_DEPTH = 4


def _cparams(*sem):
    return pltpu.CompilerParams(dimension_semantics=sem, vmem_limit_bytes=VMEM_LIMIT_BYTES)


def _dot(a, b):
    return jnp.dot(a, b, preferred_element_type=F32)


def _dot_nt(a, b, precision=None):
    return lax.dot_general(a, b, (((1,), (1,)), ((), ())), precision=precision,
                           preferred_element_type=F32)


def _rms(x, g):
    return x * lax.rsqrt(jnp.mean(x * x, axis=-1, keepdims=True) + NORM_EPS) * g


def _pack_bf16_pair(lo, hi):
    lo_b = lax.bitcast_convert_type(lo.astype(BF16).astype(F32), U32) >> 16
    hi_b = lax.bitcast_convert_type(hi.astype(BF16).astype(F32), U32) & jnp.uint32(0xFFFF0000)
    return lo_b | hi_b


def _unpack_lo(w):
    return lax.bitcast_convert_type(w << 16, F32)


def _unpack_hi(w):
    return lax.bitcast_convert_type(w & jnp.uint32(0xFFFF0000), F32)


def _ada_kernel(c_ref, w_ref, b_ref, o_ref):
    c = c_ref[...]
    a = (c * jax.nn.sigmoid(c)).astype(BF16)
    o_ref[...] = _dot(a, w_ref[...].astype(BF16)) + b_ref[...]


def _ada_mods(cpad, ada_w, ada_b):
    n_layers, d, n = ada_w.shape
    rows = cpad.shape[0]
    tn = MM_TN
    return pl.pallas_call(
        _ada_kernel,
        grid=(n_layers, n // tn),
        in_specs=[pl.BlockSpec((rows, d), lambda l, j: (0, 0)),
                  pl.BlockSpec((None, d, tn), lambda l, j: (l, 0, j)),
                  pl.BlockSpec((None, 1, tn), lambda l, j: (l, 0, j))],
        out_specs=pl.BlockSpec((None, rows, tn), lambda l, j: (l, 0, j)),
        out_shape=jax.ShapeDtypeStruct((n_layers, rows, n), F32),
        compiler_params=_cparams("arbitrary", "arbitrary"),
        name="ada_mods",
    )(cpad, ada_w, ada_b.reshape(n_layers, 1, n))


def _mod_spec(layer, chunk, row_fn):
    def idx(*g):
        return (layer, row_fn(*g), chunk, 0, 0)
    return idx


def _prenorm0_kernel(x_ref, ctx_ref, g_ref, sh_ref, sc_ref, o_ref, *, n_lat):
    i = pl.program_id(0)

    def emit(v):
        o_ref[...] = (_rms(v, g_ref[...]) * (1.0 + sc_ref[...]) + sh_ref[...]).astype(BF16)

    @pl.when(i < n_lat)
    def _():
        emit(x_ref[...])

    @pl.when(i >= n_lat)
    def _():
        emit(ctx_ref[...])


def _prenorm0(x2, ctx2, g, mods, layer, seq, ctx_row):
    t, d = x2.shape
    tc = ctx2.shape[0]
    tm = ROW_TILE
    n_lat, n_ctx = t // tm, tc // tm
    row = lambda i: jnp.where(i < n_lat, (i * tm) // seq, ctx_row)
    mspec = lambda chunk: pl.BlockSpec((None, None, None, 1, d), _mod_spec(layer, chunk, row))
    return pl.pallas_call(
        functools.partial(_prenorm0_kernel, n_lat=n_lat),
        grid=(n_lat + n_ctx,),
        in_specs=[pl.BlockSpec((tm, d), lambda i: (jnp.minimum(i, n_lat - 1), 0)),
                  pl.BlockSpec((tm, d), lambda i: (jnp.maximum(i - n_lat, 0), 0)),
                  pl.BlockSpec((1, d), lambda i: (0, 0)),
                  mspec(0), mspec(1)],
        out_specs=pl.BlockSpec((tm, d), lambda i: (i, 0)),
        out_shape=jax.ShapeDtypeStruct((t + tc, d), BF16),
        compiler_params=_cparams("arbitrary"),
        name="prenorm0",
    )(x2, ctx2, g.reshape(1, d), mods, mods)


def _ws_body(x_ref, w_ref, *refs, n_extra, n_out, epilogue, tm, rc):
    extra, outs, wb_ref = refs[:n_extra], refs[n_extra:n_extra + n_out], refs[-1]

    @pl.when(pl.program_id(1) == 0)
    def _():
        wb_ref[...] = w_ref[...].astype(BF16)

    for c in range(tm // rc):
        rows = slice(c * rc, (c + 1) * rc)
        epilogue(_dot(x_ref[rows, :], wb_ref[...]), rows, extra, outs)


def _ws_matmul(x, w, col0, n_cols, epilogue, extras, extra_specs, out_shapes, out_specs, name,
               tm=MM_TM, tn=MM_TN, m=None):
    k = x.shape[1]
    m = x.shape[0] if m is None else m
    assert m % tm == 0 and n_cols % tn == 0 and col0 % tn == 0
    jb = col0 // tn
    body = functools.partial(_ws_body, n_extra=len(extras), n_out=len(out_shapes),
                             epilogue=epilogue, tm=tm, rc=MM_RC)
    return pl.pallas_call(
        body,
        grid=(n_cols // tn, m // tm),
        in_specs=[pl.BlockSpec((tm, k), lambda j, i: (i, 0)),
                  pl.BlockSpec((k, tn), lambda j, i: (0, j + jb))] + list(extra_specs),
        out_specs=out_specs,
        out_shape=out_shapes,
        scratch_shapes=[pltpu.VMEM((k, tn), BF16)],
        compiler_params=_cparams("arbitrary", "arbitrary"),
        name=name,
    )(x, w, *extras)


def _swap32(y):
    lane = lax.broadcasted_iota(I32, y.shape, 1)
    return jnp.where((lane & 32) != 0, pltpu.roll(y, 32, 1), pltpu.roll(y, 96, 1))


def _ep_q(acc, rows, extra, outs):
    g_ref, cos_ref, sin_ref = extra
    qrot_ref, qpl_ref = outs
    cos, sin = cos_ref[rows, :], sin_ref[rows, :]
    for h in range(acc.shape[1] // NA_HEAD_DIM):
        cols = slice(h * NA_HEAD_DIM, (h + 1) * NA_HEAD_DIM)
        y = _rms(acc[:, cols], g_ref[...]) * NA_QSCALE
        qpl_ref[h, rows, :] = y.astype(BF16)
        qrot_ref[h, rows, :] = (y * cos + _swap32(y) * sin).astype(BF16)


def _ep_k(acc, rows, extra, outs):
    g_ref, cos_ref, sin_ref = extra
    (krot_ref,) = outs
    cos, sin = cos_ref[rows, :], sin_ref[rows, :]
    for h in range(acc.shape[1] // NA_HEAD_DIM):
        cols = slice(h * NA_HEAD_DIM, (h + 1) * NA_HEAD_DIM)
        y = _rms(acc[:, cols], g_ref[...])
        krot_ref[h, rows, :] = (y * cos + _swap32(y) * sin).astype(BF16)


def _ep_v(acc, rows, extra, outs):
    for h in range(acc.shape[1] // NA_HEAD_DIM):
        outs[0][h, rows, :] = acc[:, h * NA_HEAD_DIM:(h + 1) * NA_HEAD_DIM].astype(BF16)


def _ep_residual(acc, rows, extra, outs):
    x_ref, gate_ref = extra
    outs[0][rows, :] = x_ref[rows, :] + gate_ref[...] * acc


def _ep_bias_residual(acc, rows, extra, outs):
    x_ref, gate_ref, b_ref = extra
    outs[0][rows, :] = x_ref[rows, :] + gate_ref[...] * (acc + b_ref[...])


def _ep_bias_gelu(acc, rows, extra, outs):
    (b_ref,) = extra
    a = acc + b_ref[...]
    outs[0][rows, :] = (0.5 * a * (1.0 + lax.erf(a * (2.0 ** -0.5)))).astype(BF16)


def _rope_tables(seq, extra_rows):
    t = jnp.arange(seq, dtype=I32)
    pos = jnp.stack([t // GRID_W, t % GRID_W], axis=-1).astype(F32)
    inv_freq = ROPE_BASE ** (-jnp.arange(ROPE_FREQS, dtype=F32) / ROPE_FREQS)
    ang = pos[:, :, None] * inv_freq
    cos, sin = jnp.cos(ang), jnp.sin(ang)
    cos = jnp.stack([cos, cos], axis=2).reshape(seq, NA_HEAD_DIM)
    sin = jnp.stack([-sin, sin], axis=2).reshape(seq, NA_HEAD_DIM)
    cos = jnp.concatenate([cos, jnp.ones((extra_rows, NA_HEAD_DIM), F32)], axis=0)
    sin = jnp.concatenate([sin, jnp.zeros((extra_rows, NA_HEAD_DIM), F32)], axis=0)
    return cos, sin


def _qkv(h_all, w_qkv, q_g, k_g, t, seq, d):
    tm, tn = MM_TM, MM_TN
    hd = NA_HEAD_DIM
    m_all = h_all.shape[0]
    n_lat, per_seq = t // tm, seq // tm
    cos, sin = _rope_tables(seq, tm)
    g_spec = pl.BlockSpec((1, hd), lambda j, i: (0, 0))
    tab_idx = lambda j, i: (jnp.where(i < n_lat, i % per_seq, per_seq), 0)
    tab_spec = pl.BlockSpec((tm, hd), tab_idx)
    out_spec = pl.BlockSpec((tn // hd, tm, hd), lambda j, i: (j, i, 0))
    q_rot, q_pl = _ws_matmul(
        h_all, w_qkv, 0, d, _ep_q, [q_g.reshape(1, -1), cos, sin], [g_spec, tab_spec, tab_spec],
        [jax.ShapeDtypeStruct((d // hd, t, hd), BF16)] * 2, [out_spec, out_spec], "qkv_q", m=t)
    (k_rot,) = _ws_matmul(
        h_all, w_qkv, d, d, _ep_k, [k_g.reshape(1, -1), cos, sin], [g_spec, tab_spec, tab_spec],
        [jax.ShapeDtypeStruct((d // hd, m_all, hd), BF16)], [out_spec], "qkv_k")
    (v,) = _ws_matmul(
        h_all, w_qkv, 2 * d, d, _ep_v, [], [],
        [jax.ShapeDtypeStruct((d // hd, m_all, hd), BF16)], [out_spec], "qkv_v")
    return q_rot, q_pl, k_rot, v


def _bias_tables(rpb):
    n_heads, n_dr, _ = rpb.shape
    q = jnp.arange(GRID_W, dtype=I32)[:, None]
    kc = jnp.arange(GRID_W, dtype=I32)[None, :]
    dc = jnp.clip(kc - q + NA_KW - 1, 0, 2 * NA_KW - 2)
    cs = jnp.clip(q - NA_KW // 2, 0, GRID_W - NA_KW)
    in_win = (kc >= cs) & (kc < cs + NA_KW)
    c = jnp.where(in_win[None, None], rpb[:, :, dc] * LOG2_E, MASK_VALUE)
    c = c.transpose(0, 2, 1, 3).reshape(n_heads, GRID_W, n_dr * GRID_W).astype(F32)
    width = (n_dr + 2) * GRID_W
    c = jnp.pad(c, ((0, 0), (0, 0), (0, width - n_dr * GRID_W)))
    return c[:, :, :width - GRID_W], c[:, :, GRID_W:]


def _pipelined(n, start, finish, depth):
    pending = [start(i) for i in range(min(depth, n))]
    for i in range(n):
        s = pending.pop(0)
        if i + depth < n:
            pending.append(start(i + depth))
        finish(i, s)


def _attn_kernel(q_ref, qp_ref, k_ref, v_ref, kc_ref, vc_ref, c0_ref, c1_ref, o_ref,
                 mc_ref, lc_ref, oc_ref, *, rows):
    win = NA_KH * GRID_W
    half = NA_KH // 2
    seq = rows * GRID_W
    hd = q_ref.shape[1]

    def ctx_scores(c):
        return _dot_nt(qp_ref[c * ATTN_CTX_CHUNK:(c + 1) * ATTN_CTX_CHUNK, :], kc_ref[...])

    def ctx_finish(c, s):
        sl = slice(c * ATTN_CTX_CHUNK, (c + 1) * ATTN_CTX_CHUNK)
        m = jnp.max(s, axis=-1, keepdims=True)
        p = jnp.exp2(s - m)
        mc_ref[sl, :] = jnp.broadcast_to(m, (ATTN_CTX_CHUNK, hd))
        lc_ref[sl, :] = jnp.broadcast_to(jnp.sum(p, axis=-1, keepdims=True), (ATTN_CTX_CHUNK, hd))
        oc_ref[sl, :] = _dot(p.astype(BF16), vc_ref[...])

    _pipelined(seq // ATTN_CTX_CHUNK, ctx_scores, ctx_finish, 2)

    def key_start(r):
        return min(max(r - half, 0), rows - NA_KH)

    def loc_scores(r):
        k0 = key_start(r) * GRID_W
        off = (NA_KH - 1 - (r - key_start(r))) * GRID_W
        if off % LANES == 0:
            bias = c0_ref[:, off:off + win]
        else:
            bias = c1_ref[:, off - GRID_W:off - GRID_W + win]
        return _dot_nt(q_ref[r * GRID_W:(r + 1) * GRID_W, :], k_ref[k0:k0 + win, :]) + bias

    def loc_finish(r, s):
        sl = slice(r * GRID_W, (r + 1) * GRID_W)
        k0 = key_start(r) * GRID_W
        m_loc = jnp.max(s, axis=-1, keepdims=True)
        p = jnp.exp2(s - m_loc)
        l_loc = jnp.sum(p, axis=-1, keepdims=True)
        o_loc = _dot(p.astype(BF16), v_ref[k0:k0 + win, :])
        m_ctx = mc_ref[sl, :]
        m = jnp.maximum(m_loc, m_ctx)
        a = jnp.exp2(m_loc - m)
        b = jnp.exp2(m_ctx - m)
        denom = a * l_loc + b * lc_ref[sl, :]
        o_ref[sl, :] = ((a * o_loc + b * oc_ref[sl, :]) / denom).astype(BF16)

    _pipelined(rows, loc_scores, loc_finish, ATTN_DEPTH)


def _attention(q_rot, q_pl, k_rot, v, rpb, batch, seq, ctx_len):
    n_heads, t, hd = q_rot.shape
    rows = seq // GRID_W
    c0, c1 = _bias_tables(rpb)
    ctx_blk0 = t // ctx_len
    lat = pl.BlockSpec((None, seq, hd), lambda h, b: (h, b, 0))
    cx = pl.BlockSpec((None, ctx_len, hd), lambda h, b: (h, ctx_blk0 + b, 0))
    tab = pl.BlockSpec((None, GRID_W, c0.shape[2]), lambda h, b: (h, 0, 0))
    return pl.pallas_call(
        functools.partial(_attn_kernel, rows=rows),
        grid=(n_heads, batch),
        in_specs=[lat, lat, lat, lat, cx, cx, tab, tab],
        out_specs=pl.BlockSpec((seq, hd), lambda h, b: (b, h)),
        out_shape=jax.ShapeDtypeStruct((t, n_heads * hd), BF16),
        scratch_shapes=[pltpu.VMEM((seq, hd), F32)] * 3,
        compiler_params=_cparams("arbitrary", "arbitrary"),
        name="attention",
    )(q_rot, q_pl, k_rot, v, k_rot, v, c0, c1)


def _router_logits(wrt_ref, h1_ref, h2_ref):
    w = wrt_ref[...]
    w1 = w.astype(BF16)
    w2 = (w - w1.astype(F32)).astype(BF16)
    a = _dot_nt(jnp.concatenate([w1, w2], axis=0), h1_ref[...])
    b = _dot_nt(w1, h2_ref[...])
    return a[:N_EXPERTS] + (a[N_EXPERTS:] + b)


def _route(logits, rb_ref, carry_ref, eidx_ref, ew_ref, rank_ref, cnt_ref):
    tm = logits.shape[1]
    scores = jax.nn.sigmoid(logits)
    sel = scores + rb_ref[...]

    def top2(vals):
        def first_max(vs):
            m = functools.reduce(jnp.maximum, vs)
            idx = jnp.full(m.shape, len(vs) - 1, I32)
            for k in range(len(vs) - 2, -1, -1):
                idx = jnp.where(vs[k] == m, k, idx)
            return m, idx
        m1, i1 = first_max(vals)
        m2, i2 = first_max([jnp.where(i1 == k, -jnp.inf, v) for k, v in enumerate(vals)])
        return m1, i1, m2, i2

    grp = []
    for g in range(N_GROUPS):
        vals = [sel[g * EXPERTS_PER_GROUP + k:g * EXPERTS_PER_GROUP + k + 1, :]
                for k in range(EXPERTS_PER_GROUP)]
        grp.append(top2(vals))
    gsum = [m1 + m2 for m1, _, m2, _ in grp]
    gmax = functools.reduce(jnp.maximum, gsum)
    g_idx = jnp.full(gmax.shape, N_GROUPS - 1, I32)
    for g in range(N_GROUPS - 2, -1, -1):
        g_idx = jnp.where(gsum[g] == gmax, g, g_idx)
    i1 = grp[N_GROUPS - 1][1]
    i2 = grp[N_GROUPS - 1][3]
    for g in range(N_GROUPS - 2, -1, -1):
        i1 = jnp.where(g_idx == g, grp[g][1], i1)
        i2 = jnp.where(g_idx == g, grp[g][3], i2)
    e1 = g_idx * EXPERTS_PER_GROUP + i1
    e2 = g_idx * EXPERTS_PER_GROUP + i2
    e_iota = lax.broadcasted_iota(I32, (N_EXPERTS, tm), 0)
    hit1, hit2 = e_iota == e1, e_iota == e2
    s1 = jnp.sum(jnp.where(hit1, scores, 0.0), axis=0, keepdims=True)
    s2 = jnp.sum(jnp.where(hit2, scores, 0.0), axis=0, keepdims=True)
    tot = s1 + s2
    eidx_ref[0:1, :] = e1
    eidx_ref[1:2, :] = e2
    ew_ref[0:1, :] = s1 / tot
    ew_ref[1:2, :] = s2 / tot
    onehot = (hit1 | hit2).astype(BF16)
    upper = (lax.broadcasted_iota(I32, (tm, tm), 0) < lax.broadcasted_iota(I32, (tm, tm), 1)).astype(BF16)
    before = _dot(onehot, upper) + carry_ref[...]
    rank_ref[0:1, :] = jnp.sum(jnp.where(hit1, before, 0.0), axis=0, keepdims=True).astype(I32)
    rank_ref[1:2, :] = jnp.sum(jnp.where(hit2, before, 0.0), axis=0, keepdims=True).astype(I32)
    carry_ref[...] += jnp.sum(onehot.astype(F32), axis=1, keepdims=True)
    cnt_ref[...] = jnp.broadcast_to(carry_ref[...], cnt_ref.shape)


def _prenorm_router_kernel(x_ref, g_ref, sh_ref, sc_ref, wrt_ref, rb_ref,
                           hp_ref, eidx_ref, ew_ref, rank_ref, cnt_ref, carry_ref, h1_ref, h2_ref, gs_ref):
    @pl.when(pl.program_id(0) == 0)
    def _():
        carry_ref[...] = jnp.zeros_like(carry_ref)

    tm, d = x_ref.shape
    half = d // 2
    rc, cw = NORM_CHUNK_ROWS, NORM_CHUNK_COLS
    gs_ref[...] = g_ref[...] * (1.0 + sc_ref[...])

    def chunk(ci, carry):
        rows = pl.ds(pl.multiple_of(ci * rc, rc), rc)
        ss = jnp.zeros((rc, cw), F32)
        for cc in range(d // cw):
            xv = x_ref[rows, cc * cw:(cc + 1) * cw]
            ss = ss + xv * xv
        inv = lax.rsqrt(jnp.sum(ss, axis=-1, keepdims=True) / d + NORM_EPS)

        def modulated_bits(cols):
            h = x_ref[rows, cols] * inv * gs_ref[:, cols] + sh_ref[:, cols]
            h1 = h.astype(BF16)
            hb = h1.astype(F32)
            h1_ref[rows, cols] = h1
            h2_ref[rows, cols] = (h - hb).astype(BF16)
            return lax.bitcast_convert_type(hb, U32)

        for cc in range(half // cw):
            lo = modulated_bits(slice(cc * cw, (cc + 1) * cw))
            hi = modulated_bits(slice(half + cc * cw, half + (cc + 1) * cw))
            hp_ref[rows, cc * cw:(cc + 1) * cw] = (lo >> 16) | hi
        return carry

    lax.fori_loop(0, tm // rc, chunk, 0, unroll=4)
    _route(_router_logits(wrt_ref, h1_ref, h2_ref), rb_ref, carry_ref, eidx_ref, ew_ref, rank_ref, cnt_ref)


def _router_out(t, d, tm):
    shapes = [jax.ShapeDtypeStruct((t, d // 2), U32),
              jax.ShapeDtypeStruct((TOP_K, t), I32),
              jax.ShapeDtypeStruct((TOP_K, t), F32),
              jax.ShapeDtypeStruct((TOP_K, t), I32),
              jax.ShapeDtypeStruct((N_EXPERTS, LANES), F32)]
    specs = [pl.BlockSpec((tm, d // 2), lambda i: (i, 0)),
             pl.BlockSpec((TOP_K, tm), lambda i: (0, i)),
             pl.BlockSpec((TOP_K, tm), lambda i: (0, i)),
             pl.BlockSpec((TOP_K, tm), lambda i: (0, i)),
             pl.BlockSpec((N_EXPERTS, LANES), lambda i: (0, 0))]
    return shapes, specs


def _prenorm_router(x2, g, mods, layer, seq, router_w, router_b):
    t, d = x2.shape
    tm = ROW_TILE
    row = lambda i: (i * tm) // seq
    mspec = lambda chunk: pl.BlockSpec((None, None, None, 1, d), _mod_spec(layer, chunk, row))
    shapes, specs = _router_out(t, d, tm)
    return pl.pallas_call(
        _prenorm_router_kernel,
        grid=(t // tm,),
        in_specs=[pl.BlockSpec((tm, d), lambda i: (i, 0)),
                  pl.BlockSpec((1, d), lambda i: (0, 0)),
                  mspec(3), mspec(4),
                  pl.BlockSpec((N_EXPERTS, d), lambda i: (0, 0)),
                  pl.BlockSpec((N_EXPERTS, 1), lambda i: (0, 0))],
        out_specs=specs,
        out_shape=shapes,
        scratch_shapes=[pltpu.VMEM((N_EXPERTS, 1), F32), pltpu.VMEM((tm, d), BF16), pltpu.VMEM((tm, d), BF16),
                        pltpu.VMEM((1, d), F32)],
        compiler_params=_cparams("arbitrary"),
        name="prenorm_router",
    )(x2, g.reshape(1, d), mods, mods, router_w.T, router_b.reshape(N_EXPERTS, 1))


def _routing_plan(eidx, rank, cnt, t):
    assert GATHER_ROWS % MOE_TM == 0
    max_rows = (TOP_K * t + N_EXPERTS * (MOE_TM - 1)) // MOE_TM * MOE_TM
    n_rows = -(-max_rows // GATHER_ROWS) * GATHER_ROWS
    n_tiles = n_rows // MOE_TM
    counts = cnt[:, 0].astype(I32)
    padded = ((counts + MOE_TM - 1) // MOE_TM) * MOE_TM
    ends = jnp.cumsum(padded)
    offs = ends - padded
    hit = eidx[None] == jnp.arange(N_EXPERTS, dtype=I32)[:, None, None]
    dest = jnp.sum(jnp.where(hit, offs[:, None, None], 0), axis=0) + rank
    n_valid = (ends[-1] // MOE_TM).astype(I32)
    tile_start = jnp.arange(n_tiles, dtype=I32) * MOE_TM
    tile_expert = jnp.sum((tile_start[:, None] >= ends[None, :]).astype(I32), axis=1)
    last = jnp.minimum(jnp.maximum(n_valid - 1, 0), n_tiles - 1)
    tile_expert = jnp.where(jnp.arange(n_tiles) < n_valid, tile_expert, tile_expert[last])
    tile_expert = jnp.minimum(tile_expert, N_EXPERTS - 1).astype(I32)
    tok = jnp.tile(jnp.arange(t, dtype=I32), TOP_K)
    src_tok = (jnp.arange(n_rows, dtype=I32) % t).at[dest.reshape(-1)].set(tok)
    return dest, src_tok, tile_expert, n_valid.reshape(1), n_tiles


def _row_copy(src_tiles, row, dst_tiles, tile, sub, sem):
    src = src_tiles.at[lax.shift_right_logical(row, 3), pl.ds(row & (SUBLANES - 1), 1), :]
    return pltpu.make_async_copy(src, dst_tiles.at[tile, pl.ds(sub, 1), :], sem)


def _all_rows_wait(src_tiles, dst_tiles, sem):
    pltpu.make_async_copy(src_tiles.at[pl.ds(0, dst_tiles.shape[0])], dst_tiles, sem).wait()


def _issue_rows(src_tiles, idx_ref, r0, count, dst_tiles, sem):
    tile0 = lax.shift_right_logical(r0, 3)
    for k in range(count):
        _row_copy(src_tiles, idx_ref[0, r0 + k], dst_tiles, tile0 + k // SUBLANES, k % SUBLANES,
                  sem).start(priority=k % 2)


def _load_rows(buf_ref, slot, r0, count, cols=slice(None)):
    w = buf_ref[slot, pl.ds(lax.shift_right_logical(r0, 3), count // SUBLANES), :, cols]
    return w.reshape(count, w.shape[-1])


def _gather_kernel(nrows_ref, src0_ref, srcn_ref, hp_hbm, o_ref, buf_ref, sem):
    i = pl.program_id(0)
    tg, half = o_ref.shape[0], buf_ref.shape[-1]
    rc = DMA_CHUNK_ROWS
    slot = i % 2
    cur_valid = i * tg < nrows_ref[0]
    nxt_valid = (i + 1) * tg < nrows_ref[0]

    @pl.when(i == 0)
    def _():
        def body(c, carry):
            _issue_rows(hp_hbm, src0_ref, pl.multiple_of(c * rc, rc), rc, buf_ref.at[0], sem.at[0])
            return carry
        lax.fori_loop(0, tg // rc, body, 0)

    @pl.when(cur_valid)
    def _():
        _all_rows_wait(hp_hbm, buf_ref.at[slot], sem.at[slot])

    def unpack(r0):
        rows = pl.ds(r0, rc)
        w = _load_rows(buf_ref, slot, r0, rc)
        o_ref[rows, :half] = _unpack_lo(w).astype(BF16)
        o_ref[rows, half:] = _unpack_hi(w).astype(BF16)

    @pl.when(jnp.logical_and(cur_valid, nxt_valid))
    def _():
        def body(c, carry):
            r0 = pl.multiple_of(c * rc, rc)
            unpack(r0)
            _issue_rows(hp_hbm, srcn_ref, r0, rc, buf_ref.at[1 - slot], sem.at[1 - slot])
            return carry
        lax.fori_loop(0, tg // rc, body, 0)

    @pl.when(jnp.logical_and(cur_valid, jnp.logical_not(nxt_valid)))
    def _():
        def body(c, carry):
            unpack(pl.multiple_of(c * rc, rc))
            return carry
        lax.fori_loop(0, tg // rc, body, 0)

    @pl.when(jnp.logical_not(cur_valid))
    def _():
        o_ref[...] = jnp.zeros_like(o_ref)


def _gather_rows(hp, src_tok, n_valid_rows):
    n_rows = src_tok.shape[0]
    half = hp.shape[1]
    tg = GATHER_ROWS
    n_t = n_rows // tg
    src_tiles = src_tok.reshape(n_t, 1, tg)
    return pl.pallas_call(
        _gather_kernel,
        grid_spec=pltpu.PrefetchScalarGridSpec(
            num_scalar_prefetch=1,
            grid=(n_t,),
            in_specs=[pl.BlockSpec((None, 1, tg), lambda i, n: (0, 0, 0), memory_space=pltpu.SMEM),
                      pl.BlockSpec((None, 1, tg), lambda i, n: (jnp.minimum(i + 1, n_t - 1), 0, 0),
                                   memory_space=pltpu.SMEM),
                      pl.BlockSpec(memory_space=pl.ANY)],
            out_specs=pl.BlockSpec((tg, 2 * half), lambda i, n: (i, 0)),
            scratch_shapes=[pltpu.VMEM((2, tg // SUBLANES, SUBLANES, half), U32),
                            pltpu.SemaphoreType.DMA((2,))]),
        out_shape=jax.ShapeDtypeStruct((n_rows, 2 * half), BF16),
        compiler_params=_cparams("arbitrary"),
        name="moe_gather",
    )(n_valid_rows, src_tiles, src_tiles, hp.reshape(hp.shape[0] // SUBLANES, SUBLANES, half))


def _expert_changed(te_ref):
    i = pl.program_id(1)
    return jnp.logical_or(i == 0, te_ref[i] != te_ref[jnp.maximum(i - 1, 0)])


def _moe_a_kernel(te_ref, nv_ref, x_ref, wg_ref, wu_ref, o_ref, wgb_ref, wub_ref):
    @pl.when(_expert_changed(te_ref))
    def _():
        wgb_ref[...] = wg_ref[...].astype(BF16)
        wub_ref[...] = wu_ref[...].astype(BF16)

    @pl.when(pl.program_id(1) < nv_ref[0])
    def _():
        x = x_ref[...]
        gate = _dot(x, wgb_ref[...])
        up = _dot(x, wub_ref[...])
        o_ref[...] = (gate * jax.nn.sigmoid(gate) * up).astype(BF16)

    @pl.when(pl.program_id(1) >= nv_ref[0])
    def _():
        o_ref[...] = jnp.zeros_like(o_ref)


def _moe_b_kernel(te_ref, nv_ref, a_ref, wd_ref, o_ref, wdb_ref):
    @pl.when(_expert_changed(te_ref))
    def _():
        wdb_ref[...] = wd_ref[...].astype(BF16)

    @pl.when(pl.program_id(1) < nv_ref[0])
    def _():
        y = _dot(a_ref[...], wdb_ref[...])
        half = y.shape[1] // 2
        o_ref[...] = _pack_bf16_pair(y[:, :half], y[:, half:])

    @pl.when(pl.program_id(1) >= nv_ref[0])
    def _():
        o_ref[...] = jnp.zeros_like(o_ref)


def _moe_experts(xs, tile_expert, n_valid, w_gate, w_up, w_down, layer):
    n_rows, d = xs.shape
    f = w_gate.shape[3]
    n_tiles = n_rows // MOE_TM
    row_idx = lambda j, i, te, nv: (jnp.minimum(i, nv[0] - 1), 0)
    w_idx = lambda j, i, te, nv: (layer, te[i], 0, j)
    a = pl.pallas_call(
        _moe_a_kernel,
        grid_spec=pltpu.PrefetchScalarGridSpec(
            num_scalar_prefetch=2,
            grid=(f // MOE_A_TN, n_tiles),
            in_specs=[pl.BlockSpec((MOE_TM, d), row_idx),
                      pl.BlockSpec((None, None, d, MOE_A_TN), w_idx),
                      pl.BlockSpec((None, None, d, MOE_A_TN), w_idx)],
            out_specs=pl.BlockSpec((MOE_TM, MOE_A_TN), lambda j, i, te, nv: (i, j)),
            scratch_shapes=[pltpu.VMEM((d, MOE_A_TN), BF16)] * 2),
        out_shape=jax.ShapeDtypeStruct((n_rows, f), BF16),
        compiler_params=_cparams("arbitrary", "arbitrary"),
        name="moe_gate_up",
    )(tile_expert, n_valid, xs, w_gate, w_up)
    ys = pl.pallas_call(
        _moe_b_kernel,
        grid_spec=pltpu.PrefetchScalarGridSpec(
            num_scalar_prefetch=2,
            grid=(d // MOE_B_TN, n_tiles),
            in_specs=[pl.BlockSpec((MOE_TM, f), row_idx),
                      pl.BlockSpec((None, None, f, MOE_B_TN), w_idx)],
            out_specs=pl.BlockSpec((MOE_TM, MOE_B_TN // 2), lambda j, i, te, nv: (i, j)),
            scratch_shapes=[pltpu.VMEM((f, MOE_B_TN), BF16)]),
        out_shape=jax.ShapeDtypeStruct((n_rows, d // 2), U32),
        compiler_params=_cparams("arbitrary", "arbitrary"),
        name="moe_down",
    )(tile_expert, n_valid, a, w_down)
    return ys


def _combine_kernel(dest0_ref, destn_ref, ys_hbm, x_ref, ew_ref, gate_ref, *rest, with_norm):
    if with_norm:
        g_ref, sh_ref, sc_ref, xo_ref, ho_ref, buf_ref, sem = rest
    else:
        xo_ref, buf_ref, sem = rest
    i = pl.program_id(0)
    slot = i % 2
    tc, d = x_ref.shape
    n = TOP_K * tc
    rc, cw = COMBINE_CHUNK_ROWS, COMBINE_CHUNK_COLS
    n_chunks = tc // rc
    per_chunk = n // n_chunks
    q = MOE_B_TN // 2

    @pl.when(i == 0)
    def _():
        def body(c, carry):
            r0 = pl.multiple_of(c * per_chunk, per_chunk)
            _issue_rows(ys_hbm, dest0_ref, r0, per_chunk, buf_ref.at[0], sem.at[0])
            return carry
        lax.fori_loop(0, n_chunks, body, 0)

    _all_rows_wait(ys_hbm, buf_ref.at[slot], sem.at[slot])

    def chunk(ci, carry, prefetch):
        r0 = pl.multiple_of(ci * rc, rc)
        rows, rows2 = pl.ds(r0, rc), pl.ds(r0 + tc, rc)
        a1, a2 = ew_ref[rows, 0:1], ew_ref[rows, 1:2]
        ss = jnp.zeros((rc, cw), F32)
        for j in range(d // (2 * q)):
            for cc in range(q // cw):
                pcols = slice(j * q + cc * cw, j * q + (cc + 1) * cw)
                w1 = _load_rows(buf_ref, slot, r0, rc, pcols)
                w2 = _load_rows(buf_ref, slot, r0 + tc, rc, pcols)
                for part, unpack in ((0, _unpack_lo), (1, _unpack_hi)):
                    c0 = j * 2 * q + part * q + cc * cw
                    cols = slice(c0, c0 + cw)
                    xo = x_ref[rows, cols] + gate_ref[:, cols] * (a1 * unpack(w1) + a2 * unpack(w2))
                    xo_ref[rows, cols] = xo
                    if with_norm:
                        ss = ss + xo * xo
        if with_norm:
            inv = lax.rsqrt(jnp.sum(ss, axis=-1, keepdims=True) / d + NORM_EPS)
            for cc in range(d // cw):
                cols = slice(cc * cw, (cc + 1) * cw)
                y = xo_ref[rows, cols] * inv * g_ref[:, cols]
                ho_ref[rows, cols] = (y * (1.0 + sc_ref[:, cols]) + sh_ref[:, cols]).astype(BF16)
        if prefetch:
            p0 = pl.multiple_of(ci * per_chunk, per_chunk)
            _issue_rows(ys_hbm, destn_ref, p0, per_chunk, buf_ref.at[1 - slot], sem.at[1 - slot])
        return carry

    has_next = i + 1 < pl.num_programs(0)

    @pl.when(has_next)
    def _():
        lax.fori_loop(0, n_chunks, functools.partial(chunk, prefetch=True), 0, unroll=2)

    @pl.when(jnp.logical_not(has_next))
    def _():
        lax.fori_loop(0, n_chunks, functools.partial(chunk, prefetch=False), 0, unroll=2)


def _combine(x2, ys, dest, ew, mods, layer, seq, norm=None):
    t, d = x2.shape
    tc = COMBINE_ROWS
    n_t = t // tc
    dest_tiles = dest.reshape(TOP_K, n_t, tc).transpose(1, 0, 2).reshape(n_t, 1, TOP_K * tc)
    row = lambda i: (i * tc) // seq
    mspec = lambda lay, chunk: pl.BlockSpec((None, None, None, 1, d), _mod_spec(lay, chunk, row))
    in_specs = [pl.BlockSpec((None, 1, TOP_K * tc), lambda i: (0, 0, 0), memory_space=pltpu.SMEM),
                pl.BlockSpec((None, 1, TOP_K * tc), lambda i: (jnp.minimum(i + 1, n_t - 1), 0, 0),
                             memory_space=pltpu.SMEM),
                pl.BlockSpec(memory_space=pl.ANY),
                pl.BlockSpec((tc, d), lambda i: (i, 0)),
                pl.BlockSpec((tc, TOP_K), lambda i: (i, 0)),
                mspec(layer, 5)]
    args = [dest_tiles, dest_tiles, ys.reshape(ys.shape[0] // SUBLANES, SUBLANES, d // 2), x2, ew.T, mods]
    out_shapes = [jax.ShapeDtypeStruct((t, d), F32)]
    out_specs = [pl.BlockSpec((tc, d), lambda i: (i, 0))]
    if norm is not None:
        next_layer, g = norm
        in_specs += [pl.BlockSpec((1, d), lambda i: (0, 0)), mspec(next_layer, 0), mspec(next_layer, 1)]
        args += [g.reshape(1, d), mods, mods]
        out_shapes.append(jax.ShapeDtypeStruct((t, d), BF16))
        out_specs.append(pl.BlockSpec((tc, d), lambda i: (i, 0)))
    return pl.pallas_call(
        functools.partial(_combine_kernel, with_norm=norm is not None),
        grid=(n_t,),
        in_specs=in_specs,
        out_specs=out_specs,
        out_shape=out_shapes,
        scratch_shapes=[pltpu.VMEM((2, TOP_K * tc // SUBLANES, SUBLANES, d // 2), U32),
                        pltpu.SemaphoreType.DMA((2,))],
        compiler_params=_cparams("arbitrary"),
        name="moe_combine",
    )(*args)


def _moe_layer(x2, norm_g, mods, layer, seq, router_w, router_b, w_gate, w_up, w_down, next_norm):
    t = x2.shape[0]
    hp, eidx, ew, rank, cnt = _prenorm_router(x2, norm_g, mods, layer, seq, router_w, router_b)
    dest, src_tok, tile_expert, n_valid, _ = _routing_plan(eidx, rank, cnt, t)
    xs = _gather_rows(hp, src_tok, n_valid * MOE_TM)
    ys = _moe_experts(xs, tile_expert, n_valid, w_gate, w_up, w_down, layer)
    return _combine(x2, ys, dest, ew, mods, layer, seq, next_norm)


def _sgu_kernel(z_ref, ws_ref, bs_ref, g_ref, b_ref, o_ref):
    width = o_ref.shape[1]
    gdim = width // SGU_GROUPS
    for c in range(z_ref.shape[0] // SGU_CHUNK):
        rows = slice(c * SGU_CHUNK, (c + 1) * SGU_CHUNK)
        v = z_ref[rows, width:].astype(F32)
        mu = jnp.mean(v, axis=-1, keepdims=True)
        vc = v - mu
        var = jnp.mean(vc * vc, axis=-1, keepdims=True)
        vn = (vc * lax.rsqrt(var + NORM_EPS) * g_ref[...] + b_ref[...]).astype(BF16)
        for g in range(SGU_GROUPS):
            cols = slice(g * gdim, (g + 1) * gdim)
            s = _dot(ws_ref[g].astype(BF16), vn[:, cols]) + bs_ref[:, g:g + 1]
            o_ref[rows, cols] = (z_ref[rows, cols].astype(F32) * s).astype(BF16)


def _sgu_gate(z, w_s, b_s, ln_g, ln_b):
    t, two_w = z.shape
    width = two_w // 2
    tm = ROW_TILE
    return pl.pallas_call(
        _sgu_kernel,
        grid=(t // tm,),
        in_specs=[pl.BlockSpec((tm, two_w), lambda i: (i, 0)),
                  pl.BlockSpec((SGU_GROUPS, SGU_CHUNK, SGU_CHUNK), lambda i: (0, 0, 0)),
                  pl.BlockSpec((SGU_CHUNK, SGU_GROUPS), lambda i: (0, 0)),
                  pl.BlockSpec((1, width), lambda i: (0, 0)),
                  pl.BlockSpec((1, width), lambda i: (0, 0))],
        out_specs=pl.BlockSpec((tm, width), lambda i: (i, 0)),
        out_shape=jax.ShapeDtypeStruct((t, width), BF16),
        compiler_params=_cparams("arbitrary"),
        name="sgu_gate",
    )(z, w_s, b_s.T, ln_g.reshape(1, width), ln_b.reshape(1, width))


def kernel(x, c, ctx, c_ctx, ada_w, ada_b, norm1_g, norm2_g, na_w_qkv, na_q_g, na_k_g, na_rpb, na_w_o,
           sgu_w_uv, sgu_b_uv, sgu_ln_g, sgu_ln_b, sgu_w_s, sgu_b_s, sgu_w_out, sgu_b_out,
           router_w, router_b, moe_w_gate, moe_w_up, moe_w_down):
    batch, seq, d = x.shape
    ctx_len = ctx.shape[1]
    t = batch * seq
    x2 = x.reshape(t, d)
    ctx2 = ctx.reshape(batch * ctx_len, d)
    tm, tn = MM_TM, MM_TN

    mod_rows = 16
    cpad = jnp.concatenate([c, c_ctx[None, :], jnp.zeros((mod_rows - batch - 1, d), F32)], axis=0)
    mods = _ada_mods(cpad, ada_w, ada_b).reshape(ada_w.shape[0], mod_rows, 6, 1, d)
    gate_spec = lambda layer, chunk: pl.BlockSpec(
        (None, None, None, 1, tn), lambda j, i: (layer, (i * tm) // seq, chunk, 0, j))
    res_spec = pl.BlockSpec((tm, tn), lambda j, i: (i, j))
    bias_spec = pl.BlockSpec((1, tn), lambda j, i: (0, j))

    h_all = _prenorm0(x2, ctx2, norm1_g[0], mods, 0, seq, batch)
    q_rot, q_pl, k_rot, v = _qkv(h_all, na_w_qkv[0], na_q_g[0], na_k_g[0], t, seq, d)
    att = _attention(q_rot, q_pl, k_rot, v, na_rpb[0], batch, seq, ctx_len)
    (x2,) = _ws_matmul(att, na_w_o[0], 0, d, _ep_residual, [x2, mods], [res_spec, gate_spec(0, 2)],
                       [jax.ShapeDtypeStruct((t, d), F32)], [res_spec], "attn_out")
    x2, h = _moe_layer(x2, norm2_g[0], mods, 0, seq, router_w, router_b,
                       moe_w_gate, moe_w_up, moe_w_down, (1, norm1_g[1]))

    width = sgu_w_uv.shape[2] // 2
    (z,) = _ws_matmul(h, sgu_w_uv[0], 0, 2 * width, _ep_bias_gelu, [sgu_b_uv[0].reshape(1, -1)], [bias_spec],
                      [jax.ShapeDtypeStruct((t, 2 * width), BF16)], [res_spec], "sgu_uv")
    gated = _sgu_gate(z, sgu_w_s[0], sgu_b_s[0], sgu_ln_g[0], sgu_ln_b[0])
    (x2,) = _ws_matmul(gated, sgu_w_out[0], 0, d, _ep_bias_residual,
                       [x2, mods, sgu_b_out[0].reshape(1, -1)], [res_spec, gate_spec(1, 2), bias_spec],
                       [jax.ShapeDtypeStruct((t, d), F32)], [res_spec], "sgu_out")
    (x2,) = _moe_layer(x2, norm2_g[1], mods, 1, seq, router_w, router_b,
                       moe_w_gate, moe_w_up, moe_w_down, None)
    return x2.reshape(batch, seq, d)
```

```python
import functools

import jax
import jax.numpy as jnp
from jax import lax
from jax.experimental import pallas as pl
from jax.experimental.pallas import tpu as pltpu

F32, BF16, I32, U32 = jnp.float32, jnp.bfloat16, jnp.int32, jnp.uint32

GRID_W = 64
NORM_EPS = 1e-6
NA_HEAD_DIM = 128
NA_KH = 8
NA_KW = 16
LOG2_E = 1.4426950408889634
NA_QSCALE = NA_HEAD_DIM ** -0.5 * LOG2_E
ROPE_BASE = 10000.0
ROPE_FREQS = NA_HEAD_DIM // 4
SGU_CHUNK = 128
SGU_GROUPS = 16
N_EXPERTS = 16
N_GROUPS = 4
EXPERTS_PER_GROUP = N_EXPERTS // N_GROUPS
TOP_K = 2
MASK_VALUE = -1e30

LANES = 128
SUBLANES = 8
VMEM_LIMIT_BYTES = 56 * 1024 * 1024
MM_TM = 1024
MM_TN = 512
MM_RC = 128
ROW_TILE = 512
MOE_TM = 512
MOE_A_TN = 512
MOE_B_TN = 4096
CAST_CHUNK_ELEMS = 128 * 1024
GATHER_ROWS = 512
COMBINE_ROWS = 256
COMBINE_CHUNK_ROWS = 16
COMBINE_CHUNK_COLS = 512
NORM_CHUNK_ROWS = 16
NORM_CHUNK_COLS = 512
DMA_CHUNK_ROWS = 16
ATTN_CTX_CHUNK = 256
ATTN_DEPTH = 4


def _cparams(*sem):
    return pltpu.CompilerParams(dimension_semantics=sem, vmem_limit_bytes=VMEM_LIMIT_BYTES)


def _dot(a, b):
    return jnp.dot(a, b, preferred_element_type=F32)


def _dot_nt(a, b, precision=None):
    return lax.dot_general(a, b, (((1,), (1,)), ((), ())), precision=precision,
                           preferred_element_type=F32)


def _rms(x, g):
    return x * lax.rsqrt(jnp.mean(x * x, axis=-1, keepdims=True) + NORM_EPS) * g


def _pack_bf16_pair(lo, hi):
    lo_b = lax.bitcast_convert_type(lo.astype(BF16).astype(F32), U32) >> 16
    hi_b = lax.bitcast_convert_type(hi.astype(BF16).astype(F32), U32) & jnp.uint32(0xFFFF0000)
    return lo_b | hi_b


def _unpack_lo(w):
    return lax.bitcast_convert_type(w << 16, F32)


def _unpack_hi(w):
    return lax.bitcast_convert_type(w & jnp.uint32(0xFFFF0000), F32)


def _ada_kernel(c_ref, w_ref, b_ref, o_ref):
    c = c_ref[...]
    a = (c * jax.nn.sigmoid(c)).astype(BF16)
    o_ref[...] = _dot(a, w_ref[...].astype(BF16)) + b_ref[...]


def _ada_mods(cpad, ada_w, ada_b):
    n_layers, d, n = ada_w.shape
    rows = cpad.shape[0]
    tn = MM_TN
    return pl.pallas_call(
        _ada_kernel,
        grid=(n_layers, n // tn),
        in_specs=[pl.BlockSpec((rows, d), lambda l, j: (0, 0)),
                  pl.BlockSpec((None, d, tn), lambda l, j: (l, 0, j)),
                  pl.BlockSpec((None, 1, tn), lambda l, j: (l, 0, j))],
        out_specs=pl.BlockSpec((None, rows, tn), lambda l, j: (l, 0, j)),
        out_shape=jax.ShapeDtypeStruct((n_layers, rows, n), F32),
        compiler_params=_cparams("arbitrary", "arbitrary"),
        name="ada_mods",
    )(cpad, ada_w, ada_b.reshape(n_layers, 1, n))


def _mod_spec(layer, chunk, row_fn):
    def idx(*g):
        return (layer, row_fn(*g), chunk, 0, 0)
    return idx


def _prenorm0_kernel(x_ref, ctx_ref, g_ref, sh_ref, sc_ref, o_ref, *, n_lat):
    i = pl.program_id(0)

    def emit(v):
        o_ref[...] = (_rms(v, g_ref[...]) * (1.0 + sc_ref[...]) + sh_ref[...]).astype(BF16)

    @pl.when(i < n_lat)
    def _():
        emit(x_ref[...])

    @pl.when(i >= n_lat)
    def _():
        emit(ctx_ref[...])


def _prenorm0(x2, ctx2, g, mods, layer, seq, ctx_row):
    t, d = x2.shape
    tc = ctx2.shape[0]
    tm = ROW_TILE
    n_lat, n_ctx = t // tm, tc // tm
    row = lambda i: jnp.where(i < n_lat, (i * tm) // seq, ctx_row)
    mspec = lambda chunk: pl.BlockSpec((None, None, None, 1, d), _mod_spec(layer, chunk, row))
    return pl.pallas_call(
        functools.partial(_prenorm0_kernel, n_lat=n_lat),
        grid=(n_lat + n_ctx,),
        in_specs=[pl.BlockSpec((tm, d), lambda i: (jnp.minimum(i, n_lat - 1), 0)),
                  pl.BlockSpec((tm, d), lambda i: (jnp.maximum(i - n_lat, 0), 0)),
                  pl.BlockSpec((1, d), lambda i: (0, 0)),
                  mspec(0), mspec(1)],
        out_specs=pl.BlockSpec((tm, d), lambda i: (i, 0)),
        out_shape=jax.ShapeDtypeStruct((t + tc, d), BF16),
        compiler_params=_cparams("arbitrary"),
        name="prenorm0",
    )(x2, ctx2, g.reshape(1, d), mods, mods)


def _ws_body(x_ref, w_ref, *refs, n_extra, n_out, epilogue, tm, rc):
    extra, outs, wb_ref = refs[:n_extra], refs[n_extra:n_extra + n_out], refs[-1]

    @pl.when(pl.program_id(1) == 0)
    def _():
        wb_ref[...] = w_ref[...].astype(BF16)

    for c in range(tm // rc):
        rows = slice(c * rc, (c + 1) * rc)
        epilogue(_dot(x_ref[rows, :], wb_ref[...]), rows, extra, outs)


def _ws_matmul(x, w, col0, n_cols, epilogue, extras, extra_specs, out_shapes, out_specs, name,
               tm=MM_TM, tn=MM_TN, m=None):
    k = x.shape[1]
    m = x.shape[0] if m is None else m
    assert m % tm == 0 and n_cols % tn == 0 and col0 % tn == 0
    jb = col0 // tn
    body = functools.partial(_ws_body, n_extra=len(extras), n_out=len(out_shapes),
                             epilogue=epilogue, tm=tm, rc=MM_RC)
    return pl.pallas_call(
        body,
        grid=(n_cols // tn, m // tm),
        in_specs=[pl.BlockSpec((tm, k), lambda j, i: (i, 0)),
                  pl.BlockSpec((k, tn), lambda j, i: (0, j + jb))] + list(extra_specs),
        out_specs=out_specs,
        out_shape=out_shapes,
        scratch_shapes=[pltpu.VMEM((k, tn), BF16)],
        compiler_params=_cparams("arbitrary", "arbitrary"),
        name=name,
    )(x, w, *extras)


def _swap32(y):
    lane = lax.broadcasted_iota(I32, y.shape, 1)
    return jnp.where((lane & 32) != 0, pltpu.roll(y, 32, 1), pltpu.roll(y, 96, 1))


def _ep_q(acc, rows, extra, outs):
    g_ref, cos_ref, sin_ref = extra
    qrot_ref, qpl_ref = outs
    cos, sin = cos_ref[rows, :], sin_ref[rows, :]
    for h in range(acc.shape[1] // NA_HEAD_DIM):
        cols = slice(h * NA_HEAD_DIM, (h + 1) * NA_HEAD_DIM)
        y = _rms(acc[:, cols], g_ref[...]) * NA_QSCALE
        qpl_ref[h, rows, :] = y.astype(BF16)
        qrot_ref[h, rows, :] = (y * cos + _swap32(y) * sin).astype(BF16)


def _ep_k(acc, rows, extra, outs):
    g_ref, cos_ref, sin_ref = extra
    (krot_ref,) = outs
    cos, sin = cos_ref[rows, :], sin_ref[rows, :]
    for h in range(acc.shape[1] // NA_HEAD_DIM):
        cols = slice(h * NA_HEAD_DIM, (h + 1) * NA_HEAD_DIM)
        y = _rms(acc[:, cols], g_ref[...])
        krot_ref[h, rows, :] = (y * cos + _swap32(y) * sin).astype(BF16)


def _ep_v(acc, rows, extra, outs):
    for h in range(acc.shape[1] // NA_HEAD_DIM):
        outs[0][h, rows, :] = acc[:, h * NA_HEAD_DIM:(h + 1) * NA_HEAD_DIM].astype(BF16)


def _ep_residual(acc, rows, extra, outs):
    x_ref, gate_ref = extra
    outs[0][rows, :] = x_ref[rows, :] + gate_ref[...] * acc


def _ep_bias_residual(acc, rows, extra, outs):
    x_ref, gate_ref, b_ref = extra
    outs[0][rows, :] = x_ref[rows, :] + gate_ref[...] * (acc + b_ref[...])


def _ep_bias_gelu(acc, rows, extra, outs):
    (b_ref,) = extra
    a = acc + b_ref[...]
    outs[0][rows, :] = (0.5 * a * (1.0 + lax.erf(a * (2.0 ** -0.5)))).astype(BF16)


def _rope_tables(seq, extra_rows):
    t = jnp.arange(seq, dtype=I32)
    pos = jnp.stack([t // GRID_W, t % GRID_W], axis=-1).astype(F32)
    inv_freq = ROPE_BASE ** (-jnp.arange(ROPE_FREQS, dtype=F32) / ROPE_FREQS)
    ang = pos[:, :, None] * inv_freq
    cos, sin = jnp.cos(ang), jnp.sin(ang)
    cos = jnp.stack([cos, cos], axis=2).reshape(seq, NA_HEAD_DIM)
    sin = jnp.stack([-sin, sin], axis=2).reshape(seq, NA_HEAD_DIM)
    cos = jnp.concatenate([cos, jnp.ones((extra_rows, NA_HEAD_DIM), F32)], axis=0)
    sin = jnp.concatenate([sin, jnp.zeros((extra_rows, NA_HEAD_DIM), F32)], axis=0)
    return cos, sin


def _qkv(h_all, w_qkv, q_g, k_g, t, seq, d):
    tm, tn = MM_TM, MM_TN
    hd = NA_HEAD_DIM
    m_all = h_all.shape[0]
    n_lat, per_seq = t // tm, seq // tm
    cos, sin = _rope_tables(seq, tm)
    g_spec = pl.BlockSpec((1, hd), lambda j, i: (0, 0))
    tab_idx = lambda j, i: (jnp.where(i < n_lat, i % per_seq, per_seq), 0)
    tab_spec = pl.BlockSpec((tm, hd), tab_idx)
    out_spec = pl.BlockSpec((tn // hd, tm, hd), lambda j, i: (j, i, 0))
    q_rot, q_pl = _ws_matmul(
        h_all, w_qkv, 0, d, _ep_q, [q_g.reshape(1, -1), cos, sin], [g_spec, tab_spec, tab_spec],
        [jax.ShapeDtypeStruct((d // hd, t, hd), BF16)] * 2, [out_spec, out_spec], "qkv_q", m=t)
    (k_rot,) = _ws_matmul(
        h_all, w_qkv, d, d, _ep_k, [k_g.reshape(1, -1), cos, sin], [g_spec, tab_spec, tab_spec],
        [jax.ShapeDtypeStruct((d // hd, m_all, hd), BF16)], [out_spec], "qkv_k")
    (v,) = _ws_matmul(
        h_all, w_qkv, 2 * d, d, _ep_v, [], [],
        [jax.ShapeDtypeStruct((d // hd, m_all, hd), BF16)], [out_spec], "qkv_v")
    return q_rot, q_pl, k_rot, v


def _bias_tables(rpb):
    n_heads, n_dr, _ = rpb.shape
    q = jnp.arange(GRID_W, dtype=I32)[:, None]
    kc = jnp.arange(GRID_W, dtype=I32)[None, :]
    dc = jnp.clip(kc - q + NA_KW - 1, 0, 2 * NA_KW - 2)
    cs = jnp.clip(q - NA_KW // 2, 0, GRID_W - NA_KW)
    in_win = (kc >= cs) & (kc < cs + NA_KW)
    c = jnp.where(in_win[None, None], rpb[:, :, dc] * LOG2_E, MASK_VALUE)
    c = c.transpose(0, 2, 1, 3).reshape(n_heads, GRID_W, n_dr * GRID_W).astype(F32)
    width = (n_dr + 2) * GRID_W
    c = jnp.pad(c, ((0, 0), (0, 0), (0, width - n_dr * GRID_W)))
    return c[:, :, :width - GRID_W], c[:, :, GRID_W:]


def _pipelined(n, start, finish, depth):
    pending = [start(i) for i in range(min(depth, n))]
    for i in range(n):
        s = pending.pop(0)
        if i + depth < n:
            pending.append(start(i + depth))
        finish(i, s)


def _attn_kernel(q_ref, qp_ref, k_ref, v_ref, kc_ref, vc_ref, c0_ref, c1_ref, o_ref,
                 mc_ref, lc_ref, oc_ref, *, rows):
    win = NA_KH * GRID_W
    half = NA_KH // 2
    seq = rows * GRID_W
    hd = q_ref.shape[1]

    def ctx_scores(c):
        return _dot_nt(qp_ref[c * ATTN_CTX_CHUNK:(c + 1) * ATTN_CTX_CHUNK, :], kc_ref[...])

    def ctx_finish(c, s):
        sl = slice(c * ATTN_CTX_CHUNK, (c + 1) * ATTN_CTX_CHUNK)
        m = jnp.max(s, axis=-1, keepdims=True)
        p = jnp.exp2(s - m)
        mc_ref[sl, :] = jnp.broadcast_to(m, (ATTN_CTX_CHUNK, hd))
        lc_ref[sl, :] = jnp.broadcast_to(jnp.sum(p, axis=-1, keepdims=True), (ATTN_CTX_CHUNK, hd))
        oc_ref[sl, :] = _dot(p.astype(BF16), vc_ref[...])

    _pipelined(seq // ATTN_CTX_CHUNK, ctx_scores, ctx_finish, 2)

    def key_start(r):
        return min(max(r - half, 0), rows - NA_KH)

    def loc_scores(r):
        k0 = key_start(r) * GRID_W
        off = (NA_KH - 1 - (r - key_start(r))) * GRID_W
        if off % LANES == 0:
            bias = c0_ref[:, off:off + win]
        else:
            bias = c1_ref[:, off - GRID_W:off - GRID_W + win]
        return _dot_nt(q_ref[r * GRID_W:(r + 1) * GRID_W, :], k_ref[k0:k0 + win, :]) + bias

    def loc_finish(r, s):
        sl = slice(r * GRID_W, (r + 1) * GRID_W)
        k0 = key_start(r) * GRID_W
        m_loc = jnp.max(s, axis=-1, keepdims=True)
        p = jnp.exp2(s - m_loc)
        l_loc = jnp.sum(p, axis=-1, keepdims=True)
        o_loc = _dot(p.astype(BF16), v_ref[k0:k0 + win, :])
        m_ctx = mc_ref[sl, :]
        m = jnp.maximum(m_loc, m_ctx)
        a = jnp.exp2(m_loc - m)
        b = jnp.exp2(m_ctx - m)
        denom = a * l_loc + b * lc_ref[sl, :]
        o_ref[sl, :] = ((a * o_loc + b * oc_ref[sl, :]) / denom).astype(BF16)

    _pipelined(rows, loc_scores, loc_finish, ATTN_DEPTH)


def _attention(q_rot, q_pl, k_rot, v, rpb, batch, seq, ctx_len):
    n_heads, t, hd = q_rot.shape
    rows = seq // GRID_W
    c0, c1 = _bias_tables(rpb)
    ctx_blk0 = t // ctx_len
    lat = pl.BlockSpec((None, seq, hd), lambda h, b: (h, b, 0))
    cx = pl.BlockSpec((None, ctx_len, hd), lambda h, b: (h, ctx_blk0 + b, 0))
    tab = pl.BlockSpec((None, GRID_W, c0.shape[2]), lambda h, b: (h, 0, 0))
    return pl.pallas_call(
        functools.partial(_attn_kernel, rows=rows),
        grid=(n_heads, batch),
        in_specs=[lat, lat, lat, lat, cx, cx, tab, tab],
        out_specs=pl.BlockSpec((seq, hd), lambda h, b: (b, h)),
        out_shape=jax.ShapeDtypeStruct((t, n_heads * hd), BF16),
        scratch_shapes=[pltpu.VMEM((seq, hd), F32)] * 3,
        compiler_params=_cparams("arbitrary", "arbitrary"),
        name="attention",
    )(q_rot, q_pl, k_rot, v, k_rot, v, c0, c1)


def _router_logits(wrt_ref, h1_ref, h2_ref):
    w = wrt_ref[...]
    w1 = w.astype(BF16)
    w2 = (w - w1.astype(F32)).astype(BF16)
    a = _dot_nt(jnp.concatenate([w1, w2], axis=0), h1_ref[...])
    b = _dot_nt(w1, h2_ref[...])
    return a[:N_EXPERTS] + (a[N_EXPERTS:] + b)


def _route(logits, rb_ref, carry_ref, eidx_ref, ew_ref, rank_ref, cnt_ref):
    tm = logits.shape[1]
    scores = jax.nn.sigmoid(logits)
    sel = scores + rb_ref[...]

    def top2(vals):
        def first_max(vs):
            m = functools.reduce(jnp.maximum, vs)
            idx = jnp.full(m.shape, len(vs) - 1, I32)
            for k in range(len(vs) - 2, -1, -1):
                idx = jnp.where(vs[k] == m, k, idx)
            return m, idx
        m1, i1 = first_max(vals)
        m2, i2 = first_max([jnp.where(i1 == k, -jnp.inf, v) for k, v in enumerate(vals)])
        return m1, i1, m2, i2

    grp = []
    for g in range(N_GROUPS):
        vals = [sel[g * EXPERTS_PER_GROUP + k:g * EXPERTS_PER_GROUP + k + 1, :]
                for k in range(EXPERTS_PER_GROUP)]
        grp.append(top2(vals))
    gsum = [m1 + m2 for m1, _, m2, _ in grp]
    gmax = functools.reduce(jnp.maximum, gsum)
    g_idx = jnp.full(gmax.shape, N_GROUPS - 1, I32)
    for g in range(N_GROUPS - 2, -1, -1):
        g_idx = jnp.where(gsum[g] == gmax, g, g_idx)
    i1 = grp[N_GROUPS - 1][1]
    i2 = grp[N_GROUPS - 1][3]
    for g in range(N_GROUPS - 2, -1, -1):
        i1 = jnp.where(g_idx == g, grp[g][1], i1)
        i2 = jnp.where(g_idx == g, grp[g][3], i2)
    e1 = g_idx * EXPERTS_PER_GROUP + i1
    e2 = g_idx * EXPERTS_PER_GROUP + i2
    e_iota = lax.broadcasted_iota(I32, (N_EXPERTS, tm), 0)
    hit1, hit2 = e_iota == e1, e_iota == e2
    s1 = jnp.sum(jnp.where(hit1, scores, 0.0), axis=0, keepdims=True)
    s2 = jnp.sum(jnp.where(hit2, scores, 0.0), axis=0, keepdims=True)
    tot = s1 + s2
    eidx_ref[0:1, :] = e1
    eidx_ref[1:2, :] = e2
    ew_ref[0:1, :] = s1 / tot
    ew_ref[1:2, :] = s2 / tot
    onehot = (hit1 | hit2).astype(BF16)
    upper = (lax.broadcasted_iota(I32, (tm, tm), 0) < lax.broadcasted_iota(I32, (tm, tm), 1)).astype(BF16)
    before = _dot(onehot, upper) + carry_ref[...]
    rank_ref[0:1, :] = jnp.sum(jnp.where(hit1, before, 0.0), axis=0, keepdims=True).astype(I32)
    rank_ref[1:2, :] = jnp.sum(jnp.where(hit2, before, 0.0), axis=0, keepdims=True).astype(I32)
    carry_ref[...] += jnp.sum(onehot.astype(F32), axis=1, keepdims=True)
    cnt_ref[...] = jnp.broadcast_to(carry_ref[...], cnt_ref.shape)


def _prenorm_router_kernel(x_ref, g_ref, sh_ref, sc_ref, wrt_ref, rb_ref,
                           hp_ref, eidx_ref, ew_ref, rank_ref, cnt_ref, carry_ref, h1_ref, h2_ref, gs_ref):
    @pl.when(pl.program_id(0) == 0)
    def _():
        carry_ref[...] = jnp.zeros_like(carry_ref)

    tm, d = x_ref.shape
    half = d // 2
    rc, cw = NORM_CHUNK_ROWS, NORM_CHUNK_COLS
    gs_ref[...] = g_ref[...] * (1.0 + sc_ref[...])

    def chunk(ci, carry):
        rows = pl.ds(pl.multiple_of(ci * rc, rc), rc)
        ss = jnp.zeros((rc, cw), F32)
        for cc in range(d // cw):
            xv = x_ref[rows, cc * cw:(cc + 1) * cw]
            ss = ss + xv * xv
        inv = lax.rsqrt(jnp.sum(ss, axis=-1, keepdims=True) / d + NORM_EPS)

        def modulated_bits(cols):
            h = x_ref[rows, cols] * inv * gs_ref[:, cols] + sh_ref[:, cols]
            h1 = h.astype(BF16)
            hb = h1.astype(F32)
            h1_ref[rows, cols] = h1
            h2_ref[rows, cols] = (h - hb).astype(BF16)
            return lax.bitcast_convert_type(hb, U32)

        for cc in range(half // cw):
            lo = modulated_bits(slice(cc * cw, (cc + 1) * cw))
            hi = modulated_bits(slice(half + cc * cw, half + (cc + 1) * cw))
            hp_ref[rows, cc * cw:(cc + 1) * cw] = (lo >> 16) | hi
        return carry

    lax.fori_loop(0, tm // rc, chunk, 0, unroll=4)
    _route(_router_logits(wrt_ref, h1_ref, h2_ref), rb_ref, carry_ref, eidx_ref, ew_ref, rank_ref, cnt_ref)


def _router_out(t, d, tm):
    shapes = [jax.ShapeDtypeStruct((t, d // 2), U32),
              jax.ShapeDtypeStruct((TOP_K, t), I32),
              jax.ShapeDtypeStruct((TOP_K, t), F32),
              jax.ShapeDtypeStruct((TOP_K, t), I32),
              jax.ShapeDtypeStruct((N_EXPERTS, LANES), F32)]
    specs = [pl.BlockSpec((tm, d // 2), lambda i: (i, 0)),
             pl.BlockSpec((TOP_K, tm), lambda i: (0, i)),
             pl.BlockSpec((TOP_K, tm), lambda i: (0, i)),
             pl.BlockSpec((TOP_K, tm), lambda i: (0, i)),
             pl.BlockSpec((N_EXPERTS, LANES), lambda i: (0, 0))]
    return shapes, specs


def _prenorm_router(x2, g, mods, layer, seq, router_w, router_b):
    t, d = x2.shape
    tm = ROW_TILE
    row = lambda i: (i * tm) // seq
    mspec = lambda chunk: pl.BlockSpec((None, None, None, 1, d), _mod_spec(layer, chunk, row))
    shapes, specs = _router_out(t, d, tm)
    return pl.pallas_call(
        _prenorm_router_kernel,
        grid=(t // tm,),
        in_specs=[pl.BlockSpec((tm, d), lambda i: (i, 0)),
                  pl.BlockSpec((1, d), lambda i: (0, 0)),
                  mspec(3), mspec(4),
                  pl.BlockSpec((N_EXPERTS, d), lambda i: (0, 0)),
                  pl.BlockSpec((N_EXPERTS, 1), lambda i: (0, 0))],
        out_specs=specs,
        out_shape=shapes,
        scratch_shapes=[pltpu.VMEM((N_EXPERTS, 1), F32), pltpu.VMEM((tm, d), BF16), pltpu.VMEM((tm, d), BF16),
                        pltpu.VMEM((1, d), F32)],
        compiler_params=_cparams("arbitrary"),
        name="prenorm_router",
    )(x2, g.reshape(1, d), mods, mods, router_w.T, router_b.reshape(N_EXPERTS, 1))


def _routing_plan(eidx, rank, cnt, t):
    assert GATHER_ROWS % MOE_TM == 0
    max_rows = (TOP_K * t + N_EXPERTS * (MOE_TM - 1)) // MOE_TM * MOE_TM
    n_rows = -(-max_rows // GATHER_ROWS) * GATHER_ROWS
    n_tiles = n_rows // MOE_TM
    counts = cnt[:, 0].astype(I32)
    padded = ((counts + MOE_TM - 1) // MOE_TM) * MOE_TM
    ends = jnp.cumsum(padded)
    offs = ends - padded
    hit = eidx[None] == jnp.arange(N_EXPERTS, dtype=I32)[:, None, None]
    dest = jnp.sum(jnp.where(hit, offs[:, None, None], 0), axis=0) + rank
    n_valid = (ends[-1] // MOE_TM).astype(I32)
    tile_start = jnp.arange(n_tiles, dtype=I32) * MOE_TM
    tile_expert = jnp.sum((tile_start[:, None] >= ends[None, :]).astype(I32), axis=1)
    last = jnp.minimum(jnp.maximum(n_valid - 1, 0), n_tiles - 1)
    tile_expert = jnp.where(jnp.arange(n_tiles) < n_valid, tile_expert, tile_expert[last])
    tile_expert = jnp.minimum(tile_expert, N_EXPERTS - 1).astype(I32)
    e_ids = jnp.arange(N_EXPERTS, dtype=I32)
    owner = jnp.where(padded > 0, e_ids, N_EXPERTS)
    next_owner = jnp.concatenate([lax.cummin(owner[::-1])[::-1][1:], jnp.full((1,), N_EXPERTS, I32)])
    next_owner = jnp.where(next_owner >= N_EXPERTS, -1, next_owner)
    tile_next = jnp.sum(jnp.where(tile_expert[:, None] == e_ids[None, :], next_owner[None, :], 0), axis=1).astype(I32)
    tok = jnp.tile(jnp.arange(t, dtype=I32), TOP_K)
    src_tok = (jnp.arange(n_rows, dtype=I32) % t).at[dest.reshape(-1)].set(tok)
    return dest, src_tok, tile_expert, tile_next, n_valid.reshape(1)


def _row_copy(src_tiles, row, dst_tiles, tile, sub, sem):
    src = src_tiles.at[lax.shift_right_logical(row, 3), pl.ds(row & (SUBLANES - 1), 1), :]
    return pltpu.make_async_copy(src, dst_tiles.at[tile, pl.ds(sub, 1), :], sem)


def _all_rows_wait(src_tiles, dst_tiles, sem):
    pltpu.make_async_copy(src_tiles.at[pl.ds(0, dst_tiles.shape[0])], dst_tiles, sem).wait()


def _issue_rows(src_tiles, idx_ref, r0, count, dst_tiles, sem):
    tile0 = lax.shift_right_logical(r0, 3)
    for k in range(count):
        _row_copy(src_tiles, idx_ref[0, r0 + k], dst_tiles, tile0 + k // SUBLANES, k % SUBLANES,
                  sem).start(priority=k % 2)


def _load_rows(buf_ref, slot, r0, count, cols=slice(None)):
    w = buf_ref[slot, pl.ds(lax.shift_right_logical(r0, 3), count // SUBLANES), :, cols]
    return w.reshape(count, w.shape[-1])


def _gather_kernel(nrows_ref, src0_ref, srcn_ref, hp_hbm, o_ref, buf_ref, sem):
    i = pl.program_id(0)
    tg, half = o_ref.shape[0], buf_ref.shape[-1]
    rc = DMA_CHUNK_ROWS
    slot = i % 2
    cur_valid = i * tg < nrows_ref[0]
    nxt_valid = (i + 1) * tg < nrows_ref[0]

    @pl.when(i == 0)
    def _():
        def body(c, carry):
            _issue_rows(hp_hbm, src0_ref, pl.multiple_of(c * rc, rc), rc, buf_ref.at[0], sem.at[0])
            return carry
        lax.fori_loop(0, tg // rc, body, 0)

    @pl.when(cur_valid)
    def _():
        _all_rows_wait(hp_hbm, buf_ref.at[slot], sem.at[slot])

    def unpack(r0):
        rows = pl.ds(r0, rc)
        w = _load_rows(buf_ref, slot, r0, rc)
        o_ref[rows, :half] = _unpack_lo(w).astype(BF16)
        o_ref[rows, half:] = _unpack_hi(w).astype(BF16)

    @pl.when(jnp.logical_and(cur_valid, nxt_valid))
    def _():
        def body(c, carry):
            r0 = pl.multiple_of(c * rc, rc)
            unpack(r0)
            _issue_rows(hp_hbm, srcn_ref, r0, rc, buf_ref.at[1 - slot], sem.at[1 - slot])
            return carry
        lax.fori_loop(0, tg // rc, body, 0)

    @pl.when(jnp.logical_and(cur_valid, jnp.logical_not(nxt_valid)))
    def _():
        def body(c, carry):
            unpack(pl.multiple_of(c * rc, rc))
            return carry
        lax.fori_loop(0, tg // rc, body, 0)

    @pl.when(jnp.logical_not(cur_valid))
    def _():
        o_ref[...] = jnp.zeros_like(o_ref)


def _gather_rows(hp, src_tok, n_valid_rows):
    n_rows = src_tok.shape[0]
    half = hp.shape[1]
    tg = GATHER_ROWS
    n_t = n_rows // tg
    src_tiles = src_tok.reshape(n_t, 1, tg)
    return pl.pallas_call(
        _gather_kernel,
        grid_spec=pltpu.PrefetchScalarGridSpec(
            num_scalar_prefetch=1,
            grid=(n_t,),
            in_specs=[pl.BlockSpec((None, 1, tg), lambda i, n: (0, 0, 0), memory_space=pltpu.SMEM),
                      pl.BlockSpec((None, 1, tg), lambda i, n: (jnp.minimum(i + 1, n_t - 1), 0, 0),
                                   memory_space=pltpu.SMEM),
                      pl.BlockSpec(memory_space=pl.ANY)],
            out_specs=pl.BlockSpec((tg, 2 * half), lambda i, n: (i, 0)),
            scratch_shapes=[pltpu.VMEM((2, tg // SUBLANES, SUBLANES, half), U32),
                            pltpu.SemaphoreType.DMA((2,))]),
        out_shape=jax.ShapeDtypeStruct((n_rows, 2 * half), BF16),
        compiler_params=_cparams("arbitrary"),
        name="moe_gather",
    )(n_valid_rows, src_tiles, src_tiles, hp.reshape(hp.shape[0] // SUBLANES, SUBLANES, half))


def _cast_rows_to_bf16(src_ref, dst_ref):
    n_rows, n_cols = src_ref.shape
    step = max(CAST_CHUNK_ELEMS // n_cols, 16)

    def body(c, carry):
        rows = pl.ds(pl.multiple_of(c * step, step), step)
        dst_ref[rows, :] = src_ref[rows, :].astype(BF16)
        return carry

    lax.fori_loop(0, n_rows // step, body, 0)


def _stream_expert_weights(te_ref, nx_ref, valid, layer, tn, w_hbms, stage_refs, bf16_refs, sem):
    j, i = pl.program_id(0), pl.program_id(1)

    def copies(e, jj):
        cols = pl.ds(pl.multiple_of(jj * tn, tn), tn)
        return [pltpu.make_async_copy(w.at[layer, e, :, cols], st, sem.at[k])
                for k, (w, st) in enumerate(zip(w_hbms, stage_refs))]

    @pl.when(jnp.logical_and(j == 0, i == 0))
    def _():
        for c in copies(te_ref[0], 0):
            c.start()

    first = jnp.logical_and(valid, jnp.logical_or(i == 0, te_ref[i] != te_ref[jnp.maximum(i - 1, 0)]))

    @pl.when(first)
    def _():
        for c in copies(te_ref[i], j):
            c.wait()
        for st, wb in zip(stage_refs, bf16_refs):
            _cast_rows_to_bf16(st, wb)
        nxt = nx_ref[i]

        @pl.when(nxt >= 0)
        def _():
            for c in copies(nxt, j):
                c.start()

        @pl.when(jnp.logical_and(nxt < 0, j + 1 < pl.num_programs(0)))
        def _():
            for c in copies(te_ref[0], j + 1):
                c.start()


def _moe_a_kernel(te_ref, nx_ref, nv_ref, x_ref, wg_hbm, wu_hbm, o_ref,
                  wgs_ref, wus_ref, wgb_ref, wub_ref, sem, *, layer):
    valid = pl.program_id(1) < nv_ref[0]
    _stream_expert_weights(te_ref, nx_ref, valid, layer, o_ref.shape[1],
                           (wg_hbm, wu_hbm), (wgs_ref, wus_ref), (wgb_ref, wub_ref), sem)

    @pl.when(valid)
    def _():
        x = x_ref[...]
        gate = _dot(x, wgb_ref[...])
        up = _dot(x, wub_ref[...])
        o_ref[...] = (gate * jax.nn.sigmoid(gate) * up).astype(BF16)

    @pl.when(jnp.logical_not(valid))
    def _():
        o_ref[...] = jnp.zeros_like(o_ref)


def _moe_b_kernel(te_ref, nx_ref, nv_ref, a_ref, wd_hbm, o_ref, wds_ref, wdb_ref, sem, *, layer):
    valid = pl.program_id(1) < nv_ref[0]
    _stream_expert_weights(te_ref, nx_ref, valid, layer, wdb_ref.shape[1],
                           (wd_hbm,), (wds_ref,), (wdb_ref,), sem)

    @pl.when(valid)
    def _():
        y = _dot(a_ref[...], wdb_ref[...])
        half = y.shape[1] // 2
        o_ref[...] = _pack_bf16_pair(y[:, :half], y[:, half:])

    @pl.when(jnp.logical_not(valid))
    def _():
        o_ref[...] = jnp.zeros_like(o_ref)


def _moe_experts(xs, tile_expert, tile_next, n_valid, w_gate, w_up, w_down, layer):
    n_rows, d = xs.shape
    f = w_gate.shape[3]
    n_tiles = n_rows // MOE_TM
    row_idx = lambda j, i, te, nx, nv: (jnp.minimum(i, nv[0] - 1), 0)
    hbm = pl.BlockSpec(memory_space=pl.ANY)
    a = pl.pallas_call(
        functools.partial(_moe_a_kernel, layer=layer),
        grid_spec=pltpu.PrefetchScalarGridSpec(
            num_scalar_prefetch=3,
            grid=(f // MOE_A_TN, n_tiles),
            in_specs=[pl.BlockSpec((MOE_TM, d), row_idx), hbm, hbm],
            out_specs=pl.BlockSpec((MOE_TM, MOE_A_TN), lambda j, i, te, nx, nv: (i, j)),
            scratch_shapes=[pltpu.VMEM((d, MOE_A_TN), F32)] * 2 + [pltpu.VMEM((d, MOE_A_TN), BF16)] * 2
            + [pltpu.SemaphoreType.DMA((2,))]),
        out_shape=jax.ShapeDtypeStruct((n_rows, f), BF16),
        compiler_params=_cparams("arbitrary", "arbitrary"),
        name="moe_gate_up",
    )(tile_expert, tile_next, n_valid, xs, w_gate, w_up)
    ys = pl.pallas_call(
        functools.partial(_moe_b_kernel, layer=layer),
        grid_spec=pltpu.PrefetchScalarGridSpec(
            num_scalar_prefetch=3,
            grid=(d // MOE_B_TN, n_tiles),
            in_specs=[pl.BlockSpec((MOE_TM, f), row_idx), hbm],
            out_specs=pl.BlockSpec((MOE_TM, MOE_B_TN // 2), lambda j, i, te, nx, nv: (i, j)),
            scratch_shapes=[pltpu.VMEM((f, MOE_B_TN), F32), pltpu.VMEM((f, MOE_B_TN), BF16),
                            pltpu.SemaphoreType.DMA((1,))]),
        out_shape=jax.ShapeDtypeStruct((n_rows, d // 2), U32),
        compiler_params=_cparams("arbitrary", "arbitrary"),
        name="moe_down",
    )(tile_expert, tile_next, n_valid, a, w_down)
    return ys


def _combine_kernel(dest0_ref, destn_ref, ys_hbm, x_ref, ew_ref, gate_ref, *rest, with_norm):
    if with_norm:
        g_ref, sh_ref, sc_ref, xo_ref, ho_ref, buf_ref, sem = rest
    else:
        xo_ref, buf_ref, sem = rest
    i = pl.program_id(0)
    slot = i % 2
    tc, d = x_ref.shape
    n = TOP_K * tc
    rc, cw = COMBINE_CHUNK_ROWS, COMBINE_CHUNK_COLS
    n_chunks = tc // rc
    per_chunk = n // n_chunks
    q = MOE_B_TN // 2

    @pl.when(i == 0)
    def _():
        def body(c, carry):
            r0 = pl.multiple_of(c * per_chunk, per_chunk)
            _issue_rows(ys_hbm, dest0_ref, r0, per_chunk, buf_ref.at[0], sem.at[0])
            return carry
        lax.fori_loop(0, n_chunks, body, 0)

    _all_rows_wait(ys_hbm, buf_ref.at[slot], sem.at[slot])

    def chunk(ci, carry, prefetch):
        r0 = pl.multiple_of(ci * rc, rc)
        rows, rows2 = pl.ds(r0, rc), pl.ds(r0 + tc, rc)
        a1, a2 = ew_ref[rows, 0:1], ew_ref[rows, 1:2]
        ss = jnp.zeros((rc, cw), F32)
        for j in range(d // (2 * q)):
            for cc in range(q // cw):
                pcols = slice(j * q + cc * cw, j * q + (cc + 1) * cw)
                w1 = _load_rows(buf_ref, slot, r0, rc, pcols)
                w2 = _load_rows(buf_ref, slot, r0 + tc, rc, pcols)
                for part, unpack in ((0, _unpack_lo), (1, _unpack_hi)):
                    c0 = j * 2 * q + part * q + cc * cw
                    cols = slice(c0, c0 + cw)
                    xo = x_ref[rows, cols] + gate_ref[:, cols] * (a1 * unpack(w1) + a2 * unpack(w2))
                    xo_ref[rows, cols] = xo
                    if with_norm:
                        ss = ss + xo * xo
        if with_norm:
            inv = lax.rsqrt(jnp.sum(ss, axis=-1, keepdims=True) / d + NORM_EPS)
            for cc in range(d // cw):
                cols = slice(cc * cw, (cc + 1) * cw)
                y = xo_ref[rows, cols] * inv * g_ref[:, cols]
                ho_ref[rows, cols] = (y * (1.0 + sc_ref[:, cols]) + sh_ref[:, cols]).astype(BF16)
        if prefetch:
            p0 = pl.multiple_of(ci * per_chunk, per_chunk)
            _issue_rows(ys_hbm, destn_ref, p0, per_chunk, buf_ref.at[1 - slot], sem.at[1 - slot])
        return carry

    has_next = i + 1 < pl.num_programs(0)

    @pl.when(has_next)
    def _():
        lax.fori_loop(0, n_chunks, functools.partial(chunk, prefetch=True), 0, unroll=2)

    @pl.when(jnp.logical_not(has_next))
    def _():
        lax.fori_loop(0, n_chunks, functools.partial(chunk, prefetch=False), 0, unroll=2)


def _combine(x2, ys, dest, ew, mods, layer, seq, norm=None):
    t, d = x2.shape
    tc = COMBINE_ROWS
    n_t = t // tc
    dest_tiles = dest.reshape(TOP_K, n_t, tc).transpose(1, 0, 2).reshape(n_t, 1, TOP_K * tc)
    row = lambda i: (i * tc) // seq
    mspec = lambda lay, chunk: pl.BlockSpec((None, None, None, 1, d), _mod_spec(lay, chunk, row))
    in_specs = [pl.BlockSpec((None, 1, TOP_K * tc), lambda i: (0, 0, 0), memory_space=pltpu.SMEM),
                pl.BlockSpec((None, 1, TOP_K * tc), lambda i: (jnp.minimum(i + 1, n_t - 1), 0, 0),
                             memory_space=pltpu.SMEM),
                pl.BlockSpec(memory_space=pl.ANY),
                pl.BlockSpec((tc, d), lambda i: (i, 0)),
                pl.BlockSpec((tc, TOP_K), lambda i: (i, 0)),
                mspec(layer, 5)]
    args = [dest_tiles, dest_tiles, ys.reshape(ys.shape[0] // SUBLANES, SUBLANES, d // 2), x2, ew.T, mods]
    out_shapes = [jax.ShapeDtypeStruct((t, d), F32)]
    out_specs = [pl.BlockSpec((tc, d), lambda i: (i, 0))]
    if norm is not None:
        next_layer, g = norm
        in_specs += [pl.BlockSpec((1, d), lambda i: (0, 0)), mspec(next_layer, 0), mspec(next_layer, 1)]
        args += [g.reshape(1, d), mods, mods]
        out_shapes.append(jax.ShapeDtypeStruct((t, d), BF16))
        out_specs.append(pl.BlockSpec((tc, d), lambda i: (i, 0)))
    return pl.pallas_call(
        functools.partial(_combine_kernel, with_norm=norm is not None),
        grid=(n_t,),
        in_specs=in_specs,
        out_specs=out_specs,
        out_shape=out_shapes,
        scratch_shapes=[pltpu.VMEM((2, TOP_K * tc // SUBLANES, SUBLANES, d // 2), U32),
                        pltpu.SemaphoreType.DMA((2,))],
        compiler_params=_cparams("arbitrary"),
        name="moe_combine",
    )(*args)


def _moe_layer(x2, norm_g, mods, layer, seq, router_w, router_b, w_gate, w_up, w_down, next_norm):
    t = x2.shape[0]
    hp, eidx, ew, rank, cnt = _prenorm_router(x2, norm_g, mods, layer, seq, router_w, router_b)
    dest, src_tok, tile_expert, tile_next, n_valid = _routing_plan(eidx, rank, cnt, t)
    xs = _gather_rows(hp, src_tok, n_valid * MOE_TM)
    ys = _moe_experts(xs, tile_expert, tile_next, n_valid, w_gate, w_up, w_down, layer)
    return _combine(x2, ys, dest, ew, mods, layer, seq, next_norm)


def _sgu_kernel(z_ref, ws_ref, bs_ref, g_ref, b_ref, o_ref):
    width = o_ref.shape[1]
    gdim = width // SGU_GROUPS
    for c in range(z_ref.shape[0] // SGU_CHUNK):
        rows = slice(c * SGU_CHUNK, (c + 1) * SGU_CHUNK)
        v = z_ref[rows, width:].astype(F32)
        mu = jnp.mean(v, axis=-1, keepdims=True)
        vc = v - mu
        var = jnp.mean(vc * vc, axis=-1, keepdims=True)
        vn = (vc * lax.rsqrt(var + NORM_EPS) * g_ref[...] + b_ref[...]).astype(BF16)
        for g in range(SGU_GROUPS):
            cols = slice(g * gdim, (g + 1) * gdim)
            s = _dot(ws_ref[g].astype(BF16), vn[:, cols]) + bs_ref[:, g:g + 1]
            o_ref[rows, cols] = (z_ref[rows, cols].astype(F32) * s).astype(BF16)


def _sgu_gate(z, w_s, b_s, ln_g, ln_b):
    t, two_w = z.shape
    width = two_w // 2
    tm = ROW_TILE
    return pl.pallas_call(
        _sgu_kernel,
        grid=(t // tm,),
        in_specs=[pl.BlockSpec((tm, two_w), lambda i: (i, 0)),
                  pl.BlockSpec((SGU_GROUPS, SGU_CHUNK, SGU_CHUNK), lambda i: (0, 0, 0)),
                  pl.BlockSpec((SGU_CHUNK, SGU_GROUPS), lambda i: (0, 0)),
                  pl.BlockSpec((1, width), lambda i: (0, 0)),
                  pl.BlockSpec((1, width), lambda i: (0, 0))],
        out_specs=pl.BlockSpec((tm, width), lambda i: (i, 0)),
        out_shape=jax.ShapeDtypeStruct((t, width), BF16),
        compiler_params=_cparams("arbitrary"),
        name="sgu_gate",
    )(z, w_s, b_s.T, ln_g.reshape(1, width), ln_b.reshape(1, width))


def kernel(x, c, ctx, c_ctx, ada_w, ada_b, norm1_g, norm2_g, na_w_qkv, na_q_g, na_k_g, na_rpb, na_w_o,
           sgu_w_uv, sgu_b_uv, sgu_ln_g, sgu_ln_b, sgu_w_s, sgu_b_s, sgu_w_out, sgu_b_out,
           router_w, router_b, moe_w_gate, moe_w_up, moe_w_down):
    batch, seq, d = x.shape
    ctx_len = ctx.shape[1]
    t = batch * seq
    x2 = x.reshape(t, d)
    ctx2 = ctx.reshape(batch * ctx_len, d)
    tm, tn = MM_TM, MM_TN

    mod_rows = 16
    cpad = jnp.concatenate([c, c_ctx[None, :], jnp.zeros((mod_rows - batch - 1, d), F32)], axis=0)
    mods = _ada_mods(cpad, ada_w, ada_b).reshape(ada_w.shape[0], mod_rows, 6, 1, d)
    gate_spec = lambda layer, chunk: pl.BlockSpec(
        (None, None, None, 1, tn), lambda j, i: (layer, (i * tm) // seq, chunk, 0, j))
    res_spec = pl.BlockSpec((tm, tn), lambda j, i: (i, j))
    bias_spec = pl.BlockSpec((1, tn), lambda j, i: (0, j))

    h_all = _prenorm0(x2, ctx2, norm1_g[0], mods, 0, seq, batch)
    q_rot, q_pl, k_rot, v = _qkv(h_all, na_w_qkv[0], na_q_g[0], na_k_g[0], t, seq, d)
    att = _attention(q_rot, q_pl, k_rot, v, na_rpb[0], batch, seq, ctx_len)
    (x2,) = _ws_matmul(att, na_w_o[0], 0, d, _ep_residual, [x2, mods], [res_spec, gate_spec(0, 2)],
                       [jax.ShapeDtypeStruct((t, d), F32)], [res_spec], "attn_out")
    x2, h = _moe_layer(x2, norm2_g[0], mods, 0, seq, router_w, router_b,
                       moe_w_gate, moe_w_up, moe_w_down, (1, norm1_g[1]))

    width = sgu_w_uv.shape[2] // 2
    (z,) = _ws_matmul(h, sgu_w_uv[0], 0, 2 * width, _ep_bias_gelu, [sgu_b_uv[0].reshape(1, -1)], [bias_spec],
                      [jax.ShapeDtypeStruct((t, 2 * width), BF16)], [res_spec], "sgu_uv")
    gated = _sgu_gate(z, sgu_w_s[0], sgu_b_s[0], sgu_ln_g[0], sgu_ln_b[0])
    (x2,) = _ws_matmul(gated, sgu_w_out[0], 0, d, _ep_bias_residual,
                       [x2, mods, sgu_b_out[0].reshape(1, -1)], [res_spec, gate_spec(1, 2), bias_spec],
                       [jax.ShapeDtypeStruct((t, d), F32)], [res_spec], "sgu_out")
    (x2,) = _moe_layer(x2, norm2_g[1], mods, 1, seq, router_w, router_b,
                       moe_w_gate, moe_w_up, moe_w_down, None)
    return x2.reshape(batch, seq, d)
```

```python
import functools

import jax
import jax.numpy as jnp
from jax import lax
from jax.experimental import pallas as pl
from jax.experimental.pallas import tpu as pltpu

F32, BF16, I32, U32 = jnp.float32, jnp.bfloat16, jnp.int32, jnp.uint32

GRID_W = 64
NORM_EPS = 1e-6
NA_HEAD_DIM = 128
NA_KH = 8
NA_KW = 16
LOG2_E = 1.4426950408889634
NA_QSCALE = NA_HEAD_DIM ** -0.5 * LOG2_E
ROPE_BASE = 10000.0
ROPE_FREQS = NA_HEAD_DIM // 4
SGU_CHUNK = 128
SGU_GROUPS = 16
N_EXPERTS = 16
N_GROUPS = 4
EXPERTS_PER_GROUP = N_EXPERTS // N_GROUPS
TOP_K = 2
MASK_VALUE = -1e30

LANES = 128
SUBLANES = 8
VMEM_LIMIT_BYTES = 56 * 1024 * 1024
MM_TM = 1024
MM_TN = 512
MM_TN_WIDE = 1024
MM_RC = 128
ROW_TILE = 512
MOE_TM = 512
MOE_A_TN = 512
MOE_B_TN = 4096
CAST_CHUNK_ELEMS = 128 * 1024
GATHER_ROWS = 512
COMBINE_ROWS = 256
COMBINE_CHUNK_ROWS = 16
COMBINE_CHUNK_COLS = 512
NORM_CHUNK_ROWS = 16
NORM_CHUNK_COLS = 512
DMA_CHUNK_ROWS = 16
ATTN_CTX_CHUNK = 256
ATTN_DEPTH = 4


def _cparams(*sem):
    return pltpu.CompilerParams(dimension_semantics=sem, vmem_limit_bytes=VMEM_LIMIT_BYTES)


def _dot(a, b):
    return jnp.dot(a, b, preferred_element_type=F32)


def _dot_nt(a, b, precision=None):
    return lax.dot_general(a, b, (((1,), (1,)), ((), ())), precision=precision,
                           preferred_element_type=F32)


def _rms(x, g):
    return x * lax.rsqrt(jnp.mean(x * x, axis=-1, keepdims=True) + NORM_EPS) * g


def _pack_bf16_pair(lo, hi):
    lo_b = lax.bitcast_convert_type(lo.astype(BF16).astype(F32), U32) >> 16
    hi_b = lax.bitcast_convert_type(hi.astype(BF16).astype(F32), U32) & jnp.uint32(0xFFFF0000)
    return lo_b | hi_b


def _unpack_lo(w):
    return lax.bitcast_convert_type(w << 16, F32)


def _unpack_hi(w):
    return lax.bitcast_convert_type(w & jnp.uint32(0xFFFF0000), F32)


def _ada_kernel(c_ref, w_ref, b_ref, o_ref):
    c = c_ref[...]
    a = (c * jax.nn.sigmoid(c)).astype(BF16)
    o_ref[...] = _dot(a, w_ref[...].astype(BF16)) + b_ref[...]


def _ada_mods(cpad, ada_w, ada_b):
    n_layers, d, n = ada_w.shape
    rows = cpad.shape[0]
    tn = MM_TN
    return pl.pallas_call(
        _ada_kernel,
        grid=(n_layers, n // tn),
        in_specs=[pl.BlockSpec((rows, d), lambda l, j: (0, 0)),
                  pl.BlockSpec((None, d, tn), lambda l, j: (l, 0, j)),
                  pl.BlockSpec((None, 1, tn), lambda l, j: (l, 0, j))],
        out_specs=pl.BlockSpec((None, rows, tn), lambda l, j: (l, 0, j)),
        out_shape=jax.ShapeDtypeStruct((n_layers, rows, n), F32),
        compiler_params=_cparams("arbitrary", "arbitrary"),
        name="ada_mods",
    )(cpad, ada_w, ada_b.reshape(n_layers, 1, n))


def _mod_spec(layer, chunk, row_fn):
    def idx(*g):
        return (layer, row_fn(*g), chunk, 0, 0)
    return idx


def _prenorm0_kernel(x_ref, ctx_ref, g_ref, sh_ref, sc_ref, o_ref, *, n_lat):
    i = pl.program_id(0)

    def emit(v):
        o_ref[...] = (_rms(v, g_ref[...]) * (1.0 + sc_ref[...]) + sh_ref[...]).astype(BF16)

    @pl.when(i < n_lat)
    def _():
        emit(x_ref[...])

    @pl.when(i >= n_lat)
    def _():
        emit(ctx_ref[...])


def _prenorm0(x2, ctx2, g, mods, layer, seq, ctx_row):
    t, d = x2.shape
    tc = ctx2.shape[0]
    tm = ROW_TILE
    n_lat, n_ctx = t // tm, tc // tm
    row = lambda i: jnp.where(i < n_lat, (i * tm) // seq, ctx_row)
    mspec = lambda chunk: pl.BlockSpec((None, None, None, 1, d), _mod_spec(layer, chunk, row))
    return pl.pallas_call(
        functools.partial(_prenorm0_kernel, n_lat=n_lat),
        grid=(n_lat + n_ctx,),
        in_specs=[pl.BlockSpec((tm, d), lambda i: (jnp.minimum(i, n_lat - 1), 0)),
                  pl.BlockSpec((tm, d), lambda i: (jnp.maximum(i - n_lat, 0), 0)),
                  pl.BlockSpec((1, d), lambda i: (0, 0)),
                  mspec(0), mspec(1)],
        out_specs=pl.BlockSpec((tm, d), lambda i: (i, 0)),
        out_shape=jax.ShapeDtypeStruct((t + tc, d), BF16),
        compiler_params=_cparams("arbitrary"),
        name="prenorm0",
    )(x2, ctx2, g.reshape(1, d), mods, mods)


def _ws_body(x_ref, w_hbm, *refs, n_extra, n_out, epilogue, tm, rc, jb):
    extra, outs = refs[:n_extra], refs[n_extra:n_extra + n_out]
    ws_ref, wb_ref, sem = refs[n_extra + n_out:]
    j, i = pl.program_id(0), pl.program_id(1)
    tn = wb_ref.shape[1]

    def copy(jj):
        cols = pl.ds(pl.multiple_of((jj + jb) * tn, tn), tn)
        return pltpu.make_async_copy(w_hbm.at[:, cols], ws_ref, sem)

    @pl.when(jnp.logical_and(j == 0, i == 0))
    def _():
        copy(0).start()

    @pl.when(i == 0)
    def _():
        copy(j).wait()
        _cast_rows_to_bf16(ws_ref, wb_ref)

        @pl.when(j + 1 < pl.num_programs(0))
        def _():
            copy(j + 1).start()

    for c in range(tm // rc):
        rows = slice(c * rc, (c + 1) * rc)
        epilogue(_dot(x_ref[rows, :], wb_ref[...]), rows, extra, outs)


def _ws_matmul(x, w, col0, n_cols, epilogue, extras, extra_specs, out_shapes, out_specs, name,
               tm=MM_TM, tn=MM_TN, m=None):
    k = x.shape[1]
    m = x.shape[0] if m is None else m
    assert m % tm == 0 and n_cols % tn == 0 and col0 % tn == 0
    body = functools.partial(_ws_body, n_extra=len(extras), n_out=len(out_shapes),
                             epilogue=epilogue, tm=tm, rc=MM_RC, jb=col0 // tn)
    return pl.pallas_call(
        body,
        grid=(n_cols // tn, m // tm),
        in_specs=[pl.BlockSpec((tm, k), lambda j, i: (i, 0)),
                  pl.BlockSpec(memory_space=pl.ANY)] + list(extra_specs),
        out_specs=out_specs,
        out_shape=out_shapes,
        scratch_shapes=[pltpu.VMEM((k, tn), F32), pltpu.VMEM((k, tn), BF16), pltpu.SemaphoreType.DMA(())],
        compiler_params=_cparams("arbitrary", "arbitrary"),
        name=name,
    )(x, w, *extras)


def _swap32(y):
    lane = lax.broadcasted_iota(I32, y.shape, 1)
    return jnp.where((lane & 32) != 0, pltpu.roll(y, 32, 1), pltpu.roll(y, 96, 1))


def _ep_q(acc, rows, extra, outs):
    g_ref, cos_ref, sin_ref = extra
    qrot_ref, qpl_ref = outs
    cos, sin = cos_ref[rows, :], sin_ref[rows, :]
    for h in range(acc.shape[1] // NA_HEAD_DIM):
        cols = slice(h * NA_HEAD_DIM, (h + 1) * NA_HEAD_DIM)
        y = _rms(acc[:, cols], g_ref[...]) * NA_QSCALE
        qpl_ref[h, rows, :] = y.astype(BF16)
        qrot_ref[h, rows, :] = (y * cos + _swap32(y) * sin).astype(BF16)


def _ep_k(acc, rows, extra, outs):
    g_ref, cos_ref, sin_ref = extra
    (krot_ref,) = outs
    cos, sin = cos_ref[rows, :], sin_ref[rows, :]
    for h in range(acc.shape[1] // NA_HEAD_DIM):
        cols = slice(h * NA_HEAD_DIM, (h + 1) * NA_HEAD_DIM)
        y = _rms(acc[:, cols], g_ref[...])
        krot_ref[h, rows, :] = (y * cos + _swap32(y) * sin).astype(BF16)


def _ep_v(acc, rows, extra, outs):
    for h in range(acc.shape[1] // NA_HEAD_DIM):
        outs[0][h, rows, :] = acc[:, h * NA_HEAD_DIM:(h + 1) * NA_HEAD_DIM].astype(BF16)


def _ep_residual(acc, rows, extra, outs):
    x_ref, gate_ref = extra
    outs[0][rows, :] = x_ref[rows, :] + gate_ref[...] * acc


def _ep_bias_residual(acc, rows, extra, outs):
    x_ref, gate_ref, b_ref = extra
    outs[0][rows, :] = x_ref[rows, :] + gate_ref[...] * (acc + b_ref[...])


def _ep_bias_gelu(acc, rows, extra, outs):
    (b_ref,) = extra
    a = acc + b_ref[...]
    outs[0][rows, :] = (0.5 * a * (1.0 + lax.erf(a * (2.0 ** -0.5)))).astype(BF16)


def _rope_tables(seq, extra_rows):
    t = jnp.arange(seq, dtype=I32)
    pos = jnp.stack([t // GRID_W, t % GRID_W], axis=-1).astype(F32)
    inv_freq = ROPE_BASE ** (-jnp.arange(ROPE_FREQS, dtype=F32) / ROPE_FREQS)
    ang = pos[:, :, None] * inv_freq
    cos, sin = jnp.cos(ang), jnp.sin(ang)
    cos = jnp.stack([cos, cos], axis=2).reshape(seq, NA_HEAD_DIM)
    sin = jnp.stack([-sin, sin], axis=2).reshape(seq, NA_HEAD_DIM)
    cos = jnp.concatenate([cos, jnp.ones((extra_rows, NA_HEAD_DIM), F32)], axis=0)
    sin = jnp.concatenate([sin, jnp.zeros((extra_rows, NA_HEAD_DIM), F32)], axis=0)
    return cos, sin


def _qkv(h_all, w_qkv, q_g, k_g, t, seq, d):
    tm, tn = MM_TM, MM_TN_WIDE
    hd = NA_HEAD_DIM
    m_all = h_all.shape[0]
    n_lat, per_seq = t // tm, seq // tm
    cos, sin = _rope_tables(seq, tm)
    g_spec = pl.BlockSpec((1, hd), lambda j, i: (0, 0))
    tab_idx = lambda j, i: (jnp.where(i < n_lat, i % per_seq, per_seq), 0)
    tab_spec = pl.BlockSpec((tm, hd), tab_idx)
    out_spec = pl.BlockSpec((tn // hd, tm, hd), lambda j, i: (j, i, 0))
    q_rot, q_pl = _ws_matmul(
        h_all, w_qkv, 0, d, _ep_q, [q_g.reshape(1, -1), cos, sin], [g_spec, tab_spec, tab_spec],
        [jax.ShapeDtypeStruct((d // hd, t, hd), BF16)] * 2, [out_spec, out_spec], "qkv_q", tn=tn, m=t)
    (k_rot,) = _ws_matmul(
        h_all, w_qkv, d, d, _ep_k, [k_g.reshape(1, -1), cos, sin], [g_spec, tab_spec, tab_spec],
        [jax.ShapeDtypeStruct((d // hd, m_all, hd), BF16)], [out_spec], "qkv_k", tn=tn)
    (v,) = _ws_matmul(
        h_all, w_qkv, 2 * d, d, _ep_v, [], [],
        [jax.ShapeDtypeStruct((d // hd, m_all, hd), BF16)], [out_spec], "qkv_v", tn=tn)
    return q_rot, q_pl, k_rot, v


def _bias_tables(rpb):
    n_heads, n_dr, _ = rpb.shape
    q = jnp.arange(GRID_W, dtype=I32)[:, None]
    kc = jnp.arange(GRID_W, dtype=I32)[None, :]
    dc = jnp.clip(kc - q + NA_KW - 1, 0, 2 * NA_KW - 2)
    cs = jnp.clip(q - NA_KW // 2, 0, GRID_W - NA_KW)
    in_win = (kc >= cs) & (kc < cs + NA_KW)
    c = jnp.where(in_win[None, None], rpb[:, :, dc] * LOG2_E, MASK_VALUE)
    c = c.transpose(0, 2, 1, 3).reshape(n_heads, GRID_W, n_dr * GRID_W).astype(F32)
    width = (n_dr + 2) * GRID_W
    c = jnp.pad(c, ((0, 0), (0, 0), (0, width - n_dr * GRID_W)))
    return c[:, :, :width - GRID_W], c[:, :, GRID_W:]


def _pipelined(n, start, finish, depth):
    pending = [start(i) for i in range(min(depth, n))]
    for i in range(n):
        s = pending.pop(0)
        if i + depth < n:
            pending.append(start(i + depth))
        finish(i, s)


def _attn_kernel(q_ref, qp_ref, k_ref, v_ref, kc_ref, vc_ref, c0_ref, c1_ref, o_ref,
                 mc_ref, lc_ref, oc_ref, *, rows):
    win = NA_KH * GRID_W
    half = NA_KH // 2
    seq = rows * GRID_W
    hd = q_ref.shape[1]

    def ctx_scores(c):
        return _dot_nt(qp_ref[c * ATTN_CTX_CHUNK:(c + 1) * ATTN_CTX_CHUNK, :], kc_ref[...])

    def ctx_finish(c, s):
        sl = slice(c * ATTN_CTX_CHUNK, (c + 1) * ATTN_CTX_CHUNK)
        m = jnp.max(s, axis=-1, keepdims=True)
        p = jnp.exp2(s - m)
        mc_ref[sl, :] = jnp.broadcast_to(m, (ATTN_CTX_CHUNK, hd))
        lc_ref[sl, :] = jnp.broadcast_to(jnp.sum(p, axis=-1, keepdims=True), (ATTN_CTX_CHUNK, hd))
        oc_ref[sl, :] = _dot(p.astype(BF16), vc_ref[...])

    _pipelined(seq // ATTN_CTX_CHUNK, ctx_scores, ctx_finish, 2)

    def key_start(r):
        return min(max(r - half, 0), rows - NA_KH)

    def loc_scores(r):
        k0 = key_start(r) * GRID_W
        off = (NA_KH - 1 - (r - key_start(r))) * GRID_W
        if off % LANES == 0:
            bias = c0_ref[:, off:off + win]
        else:
            bias = c1_ref[:, off - GRID_W:off - GRID_W + win]
        return _dot_nt(q_ref[r * GRID_W:(r + 1) * GRID_W, :], k_ref[k0:k0 + win, :]) + bias

    def loc_finish(r, s):
        sl = slice(r * GRID_W, (r + 1) * GRID_W)
        k0 = key_start(r) * GRID_W
        m_loc = jnp.max(s, axis=-1, keepdims=True)
        p = jnp.exp2(s - m_loc)
        l_loc = jnp.sum(p, axis=-1, keepdims=True)
        o_loc = _dot(p.astype(BF16), v_ref[k0:k0 + win, :])
        m_ctx = mc_ref[sl, :]
        m = jnp.maximum(m_loc, m_ctx)
        a = jnp.exp2(m_loc - m)
        b = jnp.exp2(m_ctx - m)
        denom = a * l_loc + b * lc_ref[sl, :]
        o_ref[sl, :] = ((a * o_loc + b * oc_ref[sl, :]) / denom).astype(BF16)

    _pipelined(rows, loc_scores, loc_finish, ATTN_DEPTH)


def _attention(q_rot, q_pl, k_rot, v, rpb, batch, seq, ctx_len):
    n_heads, t, hd = q_rot.shape
    rows = seq // GRID_W
    c0, c1 = _bias_tables(rpb)
    ctx_blk0 = t // ctx_len
    lat = pl.BlockSpec((None, seq, hd), lambda h, b: (h, b, 0))
    cx = pl.BlockSpec((None, ctx_len, hd), lambda h, b: (h, ctx_blk0 + b, 0))
    tab = pl.BlockSpec((None, GRID_W, c0.shape[2]), lambda h, b: (h, 0, 0))
    return pl.pallas_call(
        functools.partial(_attn_kernel, rows=rows),
        grid=(n_heads, batch),
        in_specs=[lat, lat, lat, lat, cx, cx, tab, tab],
        out_specs=pl.BlockSpec((seq, hd), lambda h, b: (b, h)),
        out_shape=jax.ShapeDtypeStruct((t, n_heads * hd), BF16),
        scratch_shapes=[pltpu.VMEM((seq, hd), F32)] * 3,
        compiler_params=_cparams("arbitrary", "arbitrary"),
        name="attention",
    )(q_rot, q_pl, k_rot, v, k_rot, v, c0, c1)


def _router_logits(wrt_ref, h1_ref, h2_ref):
    w = wrt_ref[...]
    w1 = w.astype(BF16)
    w2 = (w - w1.astype(F32)).astype(BF16)
    a = _dot_nt(jnp.concatenate([w1, w2], axis=0), h1_ref[...])
    b = _dot_nt(w1, h2_ref[...])
    return a[:N_EXPERTS] + (a[N_EXPERTS:] + b)


def _route(logits, rb_ref, carry_ref, eidx_ref, ew_ref, rank_ref, cnt_ref):
    tm = logits.shape[1]
    scores = jax.nn.sigmoid(logits)
    sel = scores + rb_ref[...]

    def top2(vals):
        def first_max(vs):
            m = functools.reduce(jnp.maximum, vs)
            idx = jnp.full(m.shape, len(vs) - 1, I32)
            for k in range(len(vs) - 2, -1, -1):
                idx = jnp.where(vs[k] == m, k, idx)
            return m, idx
        m1, i1 = first_max(vals)
        m2, i2 = first_max([jnp.where(i1 == k, -jnp.inf, v) for k, v in enumerate(vals)])
        return m1, i1, m2, i2

    grp = []
    for g in range(N_GROUPS):
        vals = [sel[g * EXPERTS_PER_GROUP + k:g * EXPERTS_PER_GROUP + k + 1, :]
                for k in range(EXPERTS_PER_GROUP)]
        grp.append(top2(vals))
    gsum = [m1 + m2 for m1, _, m2, _ in grp]
    gmax = functools.reduce(jnp.maximum, gsum)
    g_idx = jnp.full(gmax.shape, N_GROUPS - 1, I32)
    for g in range(N_GROUPS - 2, -1, -1):
        g_idx = jnp.where(gsum[g] == gmax, g, g_idx)
    i1 = grp[N_GROUPS - 1][1]
    i2 = grp[N_GROUPS - 1][3]
    for g in range(N_GROUPS - 2, -1, -1):
        i1 = jnp.where(g_idx == g, grp[g][1], i1)
        i2 = jnp.where(g_idx == g, grp[g][3], i2)
    e1 = g_idx * EXPERTS_PER_GROUP + i1
    e2 = g_idx * EXPERTS_PER_GROUP + i2
    e_iota = lax.broadcasted_iota(I32, (N_EXPERTS, tm), 0)
    hit1, hit2 = e_iota == e1, e_iota == e2
    s1 = jnp.sum(jnp.where(hit1, scores, 0.0), axis=0, keepdims=True)
    s2 = jnp.sum(jnp.where(hit2, scores, 0.0), axis=0, keepdims=True)
    tot = s1 + s2
    eidx_ref[0:1, :] = e1
    eidx_ref[1:2, :] = e2
    ew_ref[0:1, :] = s1 / tot
    ew_ref[1:2, :] = s2 / tot
    onehot = (hit1 | hit2).astype(BF16)
    upper = (lax.broadcasted_iota(I32, (tm, tm), 0) < lax.broadcasted_iota(I32, (tm, tm), 1)).astype(BF16)
    before = _dot(onehot, upper) + carry_ref[...]
    rank_ref[0:1, :] = jnp.sum(jnp.where(hit1, before, 0.0), axis=0, keepdims=True).astype(I32)
    rank_ref[1:2, :] = jnp.sum(jnp.where(hit2, before, 0.0), axis=0, keepdims=True).astype(I32)
    carry_ref[...] += jnp.sum(onehot.astype(F32), axis=1, keepdims=True)
    cnt_ref[...] = jnp.broadcast_to(carry_ref[...], cnt_ref.shape)


def _prenorm_router_kernel(x_ref, g_ref, sh_ref, sc_ref, wrt_ref, rb_ref,
                           hp_ref, eidx_ref, ew_ref, rank_ref, cnt_ref, carry_ref, h1_ref, h2_ref, gs_ref):
    @pl.when(pl.program_id(0) == 0)
    def _():
        carry_ref[...] = jnp.zeros_like(carry_ref)

    tm, d = x_ref.shape
    half = d // 2
    rc, cw = NORM_CHUNK_ROWS, NORM_CHUNK_COLS
    gs_ref[...] = g_ref[...] * (1.0 + sc_ref[...])

    def chunk(ci, carry):
        rows = pl.ds(pl.multiple_of(ci * rc, rc), rc)
        ss = jnp.zeros((rc, cw), F32)
        for cc in range(d // cw):
            xv = x_ref[rows, cc * cw:(cc + 1) * cw]
            ss = ss + xv * xv
        inv = lax.rsqrt(jnp.sum(ss, axis=-1, keepdims=True) / d + NORM_EPS)

        def modulated_bits(cols):
            h = x_ref[rows, cols] * inv * gs_ref[:, cols] + sh_ref[:, cols]
            h1 = h.astype(BF16)
            hb = h1.astype(F32)
            h1_ref[rows, cols] = h1
            h2_ref[rows, cols] = (h - hb).astype(BF16)
            return lax.bitcast_convert_type(hb, U32)

        for cc in range(half // cw):
            lo = modulated_bits(slice(cc * cw, (cc + 1) * cw))
            hi = modulated_bits(slice(half + cc * cw, half + (cc + 1) * cw))
            hp_ref[rows, cc * cw:(cc + 1) * cw] = (lo >> 16) | hi
        return carry

    lax.fori_loop(0, tm // rc, chunk, 0, unroll=4)
    _route(_router_logits(wrt_ref, h1_ref, h2_ref), rb_ref, carry_ref, eidx_ref, ew_ref, rank_ref, cnt_ref)


def _router_out(t, d, tm):
    shapes = [jax.ShapeDtypeStruct((t, d // 2), U32),
              jax.ShapeDtypeStruct((TOP_K, t), I32),
              jax.ShapeDtypeStruct((TOP_K, t), F32),
              jax.ShapeDtypeStruct((TOP_K, t), I32),
              jax.ShapeDtypeStruct((N_EXPERTS, LANES), F32)]
    specs = [pl.BlockSpec((tm, d // 2), lambda i: (i, 0)),
             pl.BlockSpec((TOP_K, tm), lambda i: (0, i)),
             pl.BlockSpec((TOP_K, tm), lambda i: (0, i)),
             pl.BlockSpec((TOP_K, tm), lambda i: (0, i)),
             pl.BlockSpec((N_EXPERTS, LANES), lambda i: (0, 0))]
    return shapes, specs


def _prenorm_router(x2, g, mods, layer, seq, router_w, router_b):
    t, d = x2.shape
    tm = ROW_TILE
    row = lambda i: (i * tm) // seq
    mspec = lambda chunk: pl.BlockSpec((None, None, None, 1, d), _mod_spec(layer, chunk, row))
    shapes, specs = _router_out(t, d, tm)
    return pl.pallas_call(
        _prenorm_router_kernel,
        grid=(t // tm,),
        in_specs=[pl.BlockSpec((tm, d), lambda i: (i, 0)),
                  pl.BlockSpec((1, d), lambda i: (0, 0)),
                  mspec(3), mspec(4),
                  pl.BlockSpec((N_EXPERTS, d), lambda i: (0, 0)),
                  pl.BlockSpec((N_EXPERTS, 1), lambda i: (0, 0))],
        out_specs=specs,
        out_shape=shapes,
        scratch_shapes=[pltpu.VMEM((N_EXPERTS, 1), F32), pltpu.VMEM((tm, d), BF16), pltpu.VMEM((tm, d), BF16),
                        pltpu.VMEM((1, d), F32)],
        compiler_params=_cparams("arbitrary"),
        name="prenorm_router",
    )(x2, g.reshape(1, d), mods, mods, router_w.T, router_b.reshape(N_EXPERTS, 1))


def _routing_plan(eidx, rank, cnt, t):
    assert GATHER_ROWS % MOE_TM == 0
    max_rows = (TOP_K * t + N_EXPERTS * (MOE_TM - 1)) // MOE_TM * MOE_TM
    n_rows = -(-max_rows // GATHER_ROWS) * GATHER_ROWS
    n_tiles = n_rows // MOE_TM
    counts = cnt[:, 0].astype(I32)
    padded = ((counts + MOE_TM - 1) // MOE_TM) * MOE_TM
    ends = jnp.cumsum(padded)
    offs = ends - padded
    hit = eidx[None] == jnp.arange(N_EXPERTS, dtype=I32)[:, None, None]
    dest = jnp.sum(jnp.where(hit, offs[:, None, None], 0), axis=0) + rank
    n_valid = (ends[-1] // MOE_TM).astype(I32)
    tile_start = jnp.arange(n_tiles, dtype=I32) * MOE_TM
    tile_expert = jnp.sum((tile_start[:, None] >= ends[None, :]).astype(I32), axis=1)
    last = jnp.minimum(jnp.maximum(n_valid - 1, 0), n_tiles - 1)
    tile_expert = jnp.where(jnp.arange(n_tiles) < n_valid, tile_expert, tile_expert[last])
    tile_expert = jnp.minimum(tile_expert, N_EXPERTS - 1).astype(I32)
    e_ids = jnp.arange(N_EXPERTS, dtype=I32)
    owner = jnp.where(padded > 0, e_ids, N_EXPERTS)
    next_owner = jnp.concatenate([lax.cummin(owner[::-1])[::-1][1:], jnp.full((1,), N_EXPERTS, I32)])
    next_owner = jnp.where(next_owner >= N_EXPERTS, -1, next_owner)
    tile_next = jnp.sum(jnp.where(tile_expert[:, None] == e_ids[None, :], next_owner[None, :], 0), axis=1).astype(I32)
    tok = jnp.tile(jnp.arange(t, dtype=I32), TOP_K)
    src_tok = (jnp.arange(n_rows, dtype=I32) % t).at[dest.reshape(-1)].set(tok)
    return dest, src_tok, tile_expert, tile_next, n_valid.reshape(1)


def _row_copy(src_tiles, row, dst_tiles, tile, sub, sem):
    src = src_tiles.at[lax.shift_right_logical(row, 3), pl.ds(row & (SUBLANES - 1), 1), :]
    return pltpu.make_async_copy(src, dst_tiles.at[tile, pl.ds(sub, 1), :], sem)


def _all_rows_wait(src_tiles, dst_tiles, sem):
    pltpu.make_async_copy(src_tiles.at[pl.ds(0, dst_tiles.shape[0])], dst_tiles, sem).wait()


def _issue_rows(src_tiles, idx_ref, r0, count, dst_tiles, sem):
    tile0 = lax.shift_right_logical(r0, 3)
    for k in range(count):
        _row_copy(src_tiles, idx_ref[0, r0 + k], dst_tiles, tile0 + k // SUBLANES, k % SUBLANES,
                  sem).start(priority=k % 2)


def _load_rows(buf_ref, slot, r0, count, cols=slice(None)):
    w = buf_ref[slot, pl.ds(lax.shift_right_logical(r0, 3), count // SUBLANES), :, cols]
    return w.reshape(count, w.shape[-1])


def _gather_kernel(nrows_ref, src0_ref, srcn_ref, hp_hbm, o_ref, buf_ref, sem):
    i = pl.program_id(0)
    tg, half = o_ref.shape[0], buf_ref.shape[-1]
    rc = DMA_CHUNK_ROWS
    slot = i % 2
    cur_valid = i * tg < nrows_ref[0]
    nxt_valid = (i + 1) * tg < nrows_ref[0]

    @pl.when(i == 0)
    def _():
        def body(c, carry):
            _issue_rows(hp_hbm, src0_ref, pl.multiple_of(c * rc, rc), rc, buf_ref.at[0], sem.at[0])
            return carry
        lax.fori_loop(0, tg // rc, body, 0)

    @pl.when(cur_valid)
    def _():
        _all_rows_wait(hp_hbm, buf_ref.at[slot], sem.at[slot])

    def unpack(r0):
        rows = pl.ds(r0, rc)
        w = _load_rows(buf_ref, slot, r0, rc)
        o_ref[rows, :half] = _unpack_lo(w).astype(BF16)
        o_ref[rows, half:] = _unpack_hi(w).astype(BF16)

    @pl.when(jnp.logical_and(cur_valid, nxt_valid))
    def _():
        def body(c, carry):
            r0 = pl.multiple_of(c * rc, rc)
            unpack(r0)
            _issue_rows(hp_hbm, srcn_ref, r0, rc, buf_ref.at[1 - slot], sem.at[1 - slot])
            return carry
        lax.fori_loop(0, tg // rc, body, 0)

    @pl.when(jnp.logical_and(cur_valid, jnp.logical_not(nxt_valid)))
    def _():
        def body(c, carry):
            unpack(pl.multiple_of(c * rc, rc))
            return carry
        lax.fori_loop(0, tg // rc, body, 0)

    @pl.when(jnp.logical_not(cur_valid))
    def _():
        o_ref[...] = jnp.zeros_like(o_ref)


def _gather_rows(hp, src_tok, n_valid_rows):
    n_rows = src_tok.shape[0]
    half = hp.shape[1]
    tg = GATHER_ROWS
    n_t = n_rows // tg
    src_tiles = src_tok.reshape(n_t, 1, tg)
    return pl.pallas_call(
        _gather_kernel,
        grid_spec=pltpu.PrefetchScalarGridSpec(
            num_scalar_prefetch=1,
            grid=(n_t,),
            in_specs=[pl.BlockSpec((None, 1, tg), lambda i, n: (0, 0, 0), memory_space=pltpu.SMEM),
                      pl.BlockSpec((None, 1, tg), lambda i, n: (jnp.minimum(i + 1, n_t - 1), 0, 0),
                                   memory_space=pltpu.SMEM),
                      pl.BlockSpec(memory_space=pl.ANY)],
            out_specs=pl.BlockSpec((tg, 2 * half), lambda i, n: (i, 0)),
            scratch_shapes=[pltpu.VMEM((2, tg // SUBLANES, SUBLANES, half), U32),
                            pltpu.SemaphoreType.DMA((2,))]),
        out_shape=jax.ShapeDtypeStruct((n_rows, 2 * half), BF16),
        compiler_params=_cparams("arbitrary"),
        name="moe_gather",
    )(n_valid_rows, src_tiles, src_tiles, hp.reshape(hp.shape[0] // SUBLANES, SUBLANES, half))


def _cast_rows_to_bf16(src_ref, dst_ref):
    n_rows, n_cols = src_ref.shape
    step = max(CAST_CHUNK_ELEMS // n_cols, 16)

    def body(c, carry):
        rows = pl.ds(pl.multiple_of(c * step, step), step)
        dst_ref[rows, :] = src_ref[rows, :].astype(BF16)
        return carry

    lax.fori_loop(0, n_rows // step, body, 0)


def _stream_expert_weights(te_ref, nx_ref, valid, layer, tn, w_hbms, stage_refs, bf16_refs, sem):
    j, i = pl.program_id(0), pl.program_id(1)

    def copies(e, jj):
        cols = pl.ds(pl.multiple_of(jj * tn, tn), tn)
        return [pltpu.make_async_copy(w.at[layer, e, :, cols], st, sem.at[k])
                for k, (w, st) in enumerate(zip(w_hbms, stage_refs))]

    @pl.when(jnp.logical_and(j == 0, i == 0))
    def _():
        for c in copies(te_ref[0], 0):
            c.start()

    first = jnp.logical_and(valid, jnp.logical_or(i == 0, te_ref[i] != te_ref[jnp.maximum(i - 1, 0)]))

    @pl.when(first)
    def _():
        for c in copies(te_ref[i], j):
            c.wait()
        for st, wb in zip(stage_refs, bf16_refs):
            _cast_rows_to_bf16(st, wb)
        nxt = nx_ref[i]

        @pl.when(nxt >= 0)
        def _():
            for c in copies(nxt, j):
                c.start()

        @pl.when(jnp.logical_and(nxt < 0, j + 1 < pl.num_programs(0)))
        def _():
            for c in copies(te_ref[0], j + 1):
                c.start()


def _moe_a_kernel(te_ref, nx_ref, nv_ref, x_ref, wg_hbm, wu_hbm, o_ref,
                  wgs_ref, wus_ref, wgb_ref, wub_ref, sem, *, layer):
    valid = pl.program_id(1) < nv_ref[0]
    _stream_expert_weights(te_ref, nx_ref, valid, layer, o_ref.shape[1],
                           (wg_hbm, wu_hbm), (wgs_ref, wus_ref), (wgb_ref, wub_ref), sem)

    @pl.when(valid)
    def _():
        x = x_ref[...]
        gate = _dot(x, wgb_ref[...])
        up = _dot(x, wub_ref[...])
        o_ref[...] = (gate * jax.nn.sigmoid(gate) * up).astype(BF16)

    @pl.when(jnp.logical_not(valid))
    def _():
        o_ref[...] = jnp.zeros_like(o_ref)


def _moe_b_kernel(te_ref, nx_ref, nv_ref, a_ref, wd_hbm, o_ref, wds_ref, wdb_ref, sem, *, layer):
    valid = pl.program_id(1) < nv_ref[0]
    _stream_expert_weights(te_ref, nx_ref, valid, layer, wdb_ref.shape[1],
                           (wd_hbm,), (wds_ref,), (wdb_ref,), sem)

    @pl.when(valid)
    def _():
        y = _dot(a_ref[...], wdb_ref[...])
        half = y.shape[1] // 2
        o_ref[...] = _pack_bf16_pair(y[:, :half], y[:, half:])

    @pl.when(jnp.logical_not(valid))
    def _():
        o_ref[...] = jnp.zeros_like(o_ref)


def _moe_experts(xs, tile_expert, tile_next, n_valid, w_gate, w_up, w_down, layer):
    n_rows, d = xs.shape
    f = w_gate.shape[3]
    n_tiles = n_rows // MOE_TM
    row_idx = lambda j, i, te, nx, nv: (jnp.minimum(i, nv[0] - 1), 0)
    hbm = pl.BlockSpec(memory_space=pl.ANY)
    a = pl.pallas_call(
        functools.partial(_moe_a_kernel, layer=layer),
        grid_spec=pltpu.PrefetchScalarGridSpec(
            num_scalar_prefetch=3,
            grid=(f // MOE_A_TN, n_tiles),
            in_specs=[pl.BlockSpec((MOE_TM, d), row_idx), hbm, hbm],
            out_specs=pl.BlockSpec((MOE_TM, MOE_A_TN), lambda j, i, te, nx, nv: (i, j)),
            scratch_shapes=[pltpu.VMEM((d, MOE_A_TN), F32)] * 2 + [pltpu.VMEM((d, MOE_A_TN), BF16)] * 2
            + [pltpu.SemaphoreType.DMA((2,))]),
        out_shape=jax.ShapeDtypeStruct((n_rows, f), BF16),
        compiler_params=_cparams("arbitrary", "arbitrary"),
        name="moe_gate_up",
    )(tile_expert, tile_next, n_valid, xs, w_gate, w_up)
    ys = pl.pallas_call(
        functools.partial(_moe_b_kernel, layer=layer),
        grid_spec=pltpu.PrefetchScalarGridSpec(
            num_scalar_prefetch=3,
            grid=(d // MOE_B_TN, n_tiles),
            in_specs=[pl.BlockSpec((MOE_TM, f), row_idx), hbm],
            out_specs=pl.BlockSpec((MOE_TM, MOE_B_TN // 2), lambda j, i, te, nx, nv: (i, j)),
            scratch_shapes=[pltpu.VMEM((f, MOE_B_TN), F32), pltpu.VMEM((f, MOE_B_TN), BF16),
                            pltpu.SemaphoreType.DMA((1,))]),
        out_shape=jax.ShapeDtypeStruct((n_rows, d // 2), U32),
        compiler_params=_cparams("arbitrary", "arbitrary"),
        name="moe_down",
    )(tile_expert, tile_next, n_valid, a, w_down)
    return ys


def _combine_kernel(dest0_ref, destn_ref, ys_hbm, x_ref, ew_ref, gate_ref, *rest, with_norm):
    if with_norm:
        g_ref, sh_ref, sc_ref, xo_ref, ho_ref, buf_ref, sem = rest
    else:
        xo_ref, buf_ref, sem = rest
    i = pl.program_id(0)
    slot = i % 2
    tc, d = x_ref.shape
    n = TOP_K * tc
    rc, cw = COMBINE_CHUNK_ROWS, COMBINE_CHUNK_COLS
    n_chunks = tc // rc
    per_chunk = n // n_chunks
    q = MOE_B_TN // 2

    @pl.when(i == 0)
    def _():
        def body(c, carry):
            r0 = pl.multiple_of(c * per_chunk, per_chunk)
            _issue_rows(ys_hbm, dest0_ref, r0, per_chunk, buf_ref.at[0], sem.at[0])
            return carry
        lax.fori_loop(0, n_chunks, body, 0)

    _all_rows_wait(ys_hbm, buf_ref.at[slot], sem.at[slot])

    def chunk(ci, carry, prefetch):
        r0 = pl.multiple_of(ci * rc, rc)
        rows, rows2 = pl.ds(r0, rc), pl.ds(r0 + tc, rc)
        a1, a2 = ew_ref[rows, 0:1], ew_ref[rows, 1:2]
        ss = jnp.zeros((rc, cw), F32)
        for j in range(d // (2 * q)):
            for cc in range(q // cw):
                pcols = slice(j * q + cc * cw, j * q + (cc + 1) * cw)
                w1 = _load_rows(buf_ref, slot, r0, rc, pcols)
                w2 = _load_rows(buf_ref, slot, r0 + tc, rc, pcols)
                for part, unpack in ((0, _unpack_lo), (1, _unpack_hi)):
                    c0 = j * 2 * q + part * q + cc * cw
                    cols = slice(c0, c0 + cw)
                    xo = x_ref[rows, cols] + gate_ref[:, cols] * (a1 * unpack(w1) + a2 * unpack(w2))
                    xo_ref[rows, cols] = xo
                    if with_norm:
                        ss = ss + xo * xo
        if with_norm:
            inv = lax.rsqrt(jnp.sum(ss, axis=-1, keepdims=True) / d + NORM_EPS)
            for cc in range(d // cw):
                cols = slice(cc * cw, (cc + 1) * cw)
                y = xo_ref[rows, cols] * inv * g_ref[:, cols]
                ho_ref[rows, cols] = (y * (1.0 + sc_ref[:, cols]) + sh_ref[:, cols]).astype(BF16)
        if prefetch:
            p0 = pl.multiple_of(ci * per_chunk, per_chunk)
            _issue_rows(ys_hbm, destn_ref, p0, per_chunk, buf_ref.at[1 - slot], sem.at[1 - slot])
        return carry

    has_next = i + 1 < pl.num_programs(0)

    @pl.when(has_next)
    def _():
        lax.fori_loop(0, n_chunks, functools.partial(chunk, prefetch=True), 0, unroll=2)

    @pl.when(jnp.logical_not(has_next))
    def _():
        lax.fori_loop(0, n_chunks, functools.partial(chunk, prefetch=False), 0, unroll=2)


def _combine(x2, ys, dest, ew, mods, layer, seq, norm=None):
    t, d = x2.shape
    tc = COMBINE_ROWS
    n_t = t // tc
    dest_tiles = dest.reshape(TOP_K, n_t, tc).transpose(1, 0, 2).reshape(n_t, 1, TOP_K * tc)
    row = lambda i: (i * tc) // seq
    mspec = lambda lay, chunk: pl.BlockSpec((None, None, None, 1, d), _mod_spec(lay, chunk, row))
    in_specs = [pl.BlockSpec((None, 1, TOP_K * tc), lambda i: (0, 0, 0), memory_space=pltpu.SMEM),
                pl.BlockSpec((None, 1, TOP_K * tc), lambda i: (jnp.minimum(i + 1, n_t - 1), 0, 0),
                             memory_space=pltpu.SMEM),
                pl.BlockSpec(memory_space=pl.ANY),
                pl.BlockSpec((tc, d), lambda i: (i, 0)),
                pl.BlockSpec((tc, TOP_K), lambda i: (i, 0)),
                mspec(layer, 5)]
    args = [dest_tiles, dest_tiles, ys.reshape(ys.shape[0] // SUBLANES, SUBLANES, d // 2), x2, ew.T, mods]
    out_shapes = [jax.ShapeDtypeStruct((t, d), F32)]
    out_specs = [pl.BlockSpec((tc, d), lambda i: (i, 0))]
    if norm is not None:
        next_layer, g = norm
        in_specs += [pl.BlockSpec((1, d), lambda i: (0, 0)), mspec(next_layer, 0), mspec(next_layer, 1)]
        args += [g.reshape(1, d), mods, mods]
        out_shapes.append(jax.ShapeDtypeStruct((t, d), BF16))
        out_specs.append(pl.BlockSpec((tc, d), lambda i: (i, 0)))
    return pl.pallas_call(
        functools.partial(_combine_kernel, with_norm=norm is not None),
        grid=(n_t,),
        in_specs=in_specs,
        out_specs=out_specs,
        out_shape=out_shapes,
        scratch_shapes=[pltpu.VMEM((2, TOP_K * tc // SUBLANES, SUBLANES, d // 2), U32),
                        pltpu.SemaphoreType.DMA((2,))],
        compiler_params=_cparams("arbitrary"),
        name="moe_combine",
    )(*args)


def _moe_layer(x2, norm_g, mods, layer, seq, router_w, router_b, w_gate, w_up, w_down, next_norm):
    t = x2.shape[0]
    hp, eidx, ew, rank, cnt = _prenorm_router(x2, norm_g, mods, layer, seq, router_w, router_b)
    dest, src_tok, tile_expert, tile_next, n_valid = _routing_plan(eidx, rank, cnt, t)
    xs = _gather_rows(hp, src_tok, n_valid * MOE_TM)
    ys = _moe_experts(xs, tile_expert, tile_next, n_valid, w_gate, w_up, w_down, layer)
    return _combine(x2, ys, dest, ew, mods, layer, seq, next_norm)


def _sgu_kernel(z_ref, ws_ref, bs_ref, g_ref, b_ref, o_ref):
    width = o_ref.shape[1]
    gdim = width // SGU_GROUPS
    for c in range(z_ref.shape[0] // SGU_CHUNK):
        rows = slice(c * SGU_CHUNK, (c + 1) * SGU_CHUNK)
        v = z_ref[rows, width:].astype(F32)
        mu = jnp.mean(v, axis=-1, keepdims=True)
        vc = v - mu
        var = jnp.mean(vc * vc, axis=-1, keepdims=True)
        vn = (vc * lax.rsqrt(var + NORM_EPS) * g_ref[...] + b_ref[...]).astype(BF16)
        for g in range(SGU_GROUPS):
            cols = slice(g * gdim, (g + 1) * gdim)
            s = _dot(ws_ref[g].astype(BF16), vn[:, cols]) + bs_ref[:, g:g + 1]
            o_ref[rows, cols] = (z_ref[rows, cols].astype(F32) * s).astype(BF16)


def _sgu_gate(z, w_s, b_s, ln_g, ln_b):
    t, two_w = z.shape
    width = two_w // 2
    tm = ROW_TILE
    return pl.pallas_call(
        _sgu_kernel,
        grid=(t // tm,),
        in_specs=[pl.BlockSpec((tm, two_w), lambda i: (i, 0)),
                  pl.BlockSpec((SGU_GROUPS, SGU_CHUNK, SGU_CHUNK), lambda i: (0, 0, 0)),
                  pl.BlockSpec((SGU_CHUNK, SGU_GROUPS), lambda i: (0, 0)),
                  pl.BlockSpec((1, width), lambda i: (0, 0)),
                  pl.BlockSpec((1, width), lambda i: (0, 0))],
        out_specs=pl.BlockSpec((tm, width), lambda i: (i, 0)),
        out_shape=jax.ShapeDtypeStruct((t, width), BF16),
        compiler_params=_cparams("arbitrary"),
        name="sgu_gate",
    )(z, w_s, b_s.T, ln_g.reshape(1, width), ln_b.reshape(1, width))


def kernel(x, c, ctx, c_ctx, ada_w, ada_b, norm1_g, norm2_g, na_w_qkv, na_q_g, na_k_g, na_rpb, na_w_o,
           sgu_w_uv, sgu_b_uv, sgu_ln_g, sgu_ln_b, sgu_w_s, sgu_b_s, sgu_w_out, sgu_b_out,
           router_w, router_b, moe_w_gate, moe_w_up, moe_w_down):
    batch, seq, d = x.shape
    ctx_len = ctx.shape[1]
    t = batch * seq
    x2 = x.reshape(t, d)
    ctx2 = ctx.reshape(batch * ctx_len, d)
    tm, tn = MM_TM, MM_TN

    mod_rows = 16
    cpad = jnp.concatenate([c, c_ctx[None, :], jnp.zeros((mod_rows - batch - 1, d), F32)], axis=0)
    mods = _ada_mods(cpad, ada_w, ada_b).reshape(ada_w.shape[0], mod_rows, 6, 1, d)
    gate_spec = lambda layer, chunk: pl.BlockSpec(
        (None, None, None, 1, tn), lambda j, i: (layer, (i * tm) // seq, chunk, 0, j))
    res_spec = pl.BlockSpec((tm, tn), lambda j, i: (i, j))
    bias_spec = pl.BlockSpec((1, tn), lambda j, i: (0, j))

    h_all = _prenorm0(x2, ctx2, norm1_g[0], mods, 0, seq, batch)
    q_rot, q_pl, k_rot, v = _qkv(h_all, na_w_qkv[0], na_q_g[0], na_k_g[0], t, seq, d)
    att = _attention(q_rot, q_pl, k_rot, v, na_rpb[0], batch, seq, ctx_len)
    (x2,) = _ws_matmul(att, na_w_o[0], 0, d, _ep_residual, [x2, mods], [res_spec, gate_spec(0, 2)],
                       [jax.ShapeDtypeStruct((t, d), F32)], [res_spec], "attn_out")
    x2, h = _moe_layer(x2, norm2_g[0], mods, 0, seq, router_w, router_b,
                       moe_w_gate, moe_w_up, moe_w_down, (1, norm1_g[1]))

    width = sgu_w_uv.shape[2] // 2
    wide = MM_TN_WIDE
    (z,) = _ws_matmul(h, sgu_w_uv[0], 0, 2 * width, _ep_bias_gelu, [sgu_b_uv[0].reshape(1, -1)],
                      [pl.BlockSpec((1, wide), lambda j, i: (0, j))],
                      [jax.ShapeDtypeStruct((t, 2 * width), BF16)],
                      [pl.BlockSpec((tm, wide), lambda j, i: (i, j))], "sgu_uv", tn=wide)
    gated = _sgu_gate(z, sgu_w_s[0], sgu_b_s[0], sgu_ln_g[0], sgu_ln_b[0])
    (x2,) = _ws_matmul(gated, sgu_w_out[0], 0, d, _ep_bias_residual,
                       [x2, mods, sgu_b_out[0].reshape(1, -1)], [res_spec, gate_spec(1, 2), bias_spec],
                       [jax.ShapeDtypeStruct((t, d), F32)], [res_spec], "sgu_out")
    (x2,) = _moe_layer(x2, norm2_g[1], mods, 1, seq, router_w, router_b,
                       moe_w_gate, moe_w_up, moe_w_down, None)
    return x2.reshape(batch, seq, d)
```

```python
import functools

import jax
import jax.numpy as jnp
from jax import lax
from jax.experimental import pallas as pl
from jax.experimental.pallas import tpu as pltpu

F32, BF16, I32, U32 = jnp.float32, jnp.bfloat16, jnp.int32, jnp.uint32

GRID_W = 64
NORM_EPS = 1e-6
NA_HEAD_DIM = 128
NA_KH = 8
NA_KW = 16
LOG2_E = 1.4426950408889634
NA_QSCALE = NA_HEAD_DIM ** -0.5 * LOG2_E
ROPE_BASE = 10000.0
ROPE_FREQS = NA_HEAD_DIM // 4
SGU_CHUNK = 128
SGU_GROUPS = 16
N_EXPERTS = 16
N_GROUPS = 4
EXPERTS_PER_GROUP = N_EXPERTS // N_GROUPS
TOP_K = 2
MASK_VALUE = -1e30

LANES = 128
SUBLANES = 8
VMEM_LIMIT_BYTES = 56 * 1024 * 1024
MM_TM = 1024
MM_TN = 512
MM_TN_WIDE = 1024
MM_RC = 128
ROW_TILE = 512
MOE_TM = 512
MOE_A_TN = 512
MOE_B_TN = 4096
CAST_CHUNK_ELEMS = 128 * 1024
GATHER_ROWS = 512
COMBINE_ROWS = 256
COMBINE_CHUNK_ROWS = 16
COMBINE_CHUNK_COLS = 512
NORM_CHUNK_ROWS = 16
NORM_CHUNK_COLS = 512
DMA_CHUNK_ROWS = 16
ATTN_CTX_CHUNK = 256
ATTN_DEPTH = 4


def _cparams(*sem):
    return pltpu.CompilerParams(dimension_semantics=sem, vmem_limit_bytes=VMEM_LIMIT_BYTES)


def _dot(a, b):
    return jnp.dot(a, b, preferred_element_type=F32)


def _dot_nt(a, b, precision=None):
    return lax.dot_general(a, b, (((1,), (1,)), ((), ())), precision=precision,
                           preferred_element_type=F32)


def _rms(x, g):
    return x * lax.rsqrt(jnp.mean(x * x, axis=-1, keepdims=True) + NORM_EPS) * g


def _pack_bf16_pair(lo, hi):
    lo_b = lax.bitcast_convert_type(lo.astype(BF16).astype(F32), U32) >> 16
    hi_b = lax.bitcast_convert_type(hi.astype(BF16).astype(F32), U32) & jnp.uint32(0xFFFF0000)
    return lo_b | hi_b


def _unpack_lo(w):
    return lax.bitcast_convert_type(w << 16, F32)


def _unpack_hi(w):
    return lax.bitcast_convert_type(w & jnp.uint32(0xFFFF0000), F32)


def _ada_kernel(c_ref, w_ref, b_ref, o_ref):
    c = c_ref[...]
    a = (c * jax.nn.sigmoid(c)).astype(BF16)
    o_ref[...] = _dot(a, w_ref[...].astype(BF16)) + b_ref[...]


def _ada_mods(cpad, ada_w, ada_b):
    n_layers, d, n = ada_w.shape
    rows = cpad.shape[0]
    tn = MM_TN
    return pl.pallas_call(
        _ada_kernel,
        grid=(n_layers, n // tn),
        in_specs=[pl.BlockSpec((rows, d), lambda l, j: (0, 0)),
                  pl.BlockSpec((None, d, tn), lambda l, j: (l, 0, j)),
                  pl.BlockSpec((None, 1, tn), lambda l, j: (l, 0, j))],
        out_specs=pl.BlockSpec((None, rows, tn), lambda l, j: (l, 0, j)),
        out_shape=jax.ShapeDtypeStruct((n_layers, rows, n), F32),
        compiler_params=_cparams("arbitrary", "arbitrary"),
        name="ada_mods",
    )(cpad, ada_w, ada_b.reshape(n_layers, 1, n))


def _mod_spec(layer, chunk, row_fn):
    def idx(*g):
        return (layer, row_fn(*g), chunk, 0, 0)
    return idx


def _prenorm0_kernel(x_ref, ctx_ref, g_ref, sh_ref, sc_ref, o_ref, *, n_lat):
    i = pl.program_id(0)

    def emit(v):
        o_ref[...] = (_rms(v, g_ref[...]) * (1.0 + sc_ref[...]) + sh_ref[...]).astype(BF16)

    @pl.when(i < n_lat)
    def _():
        emit(x_ref[...])

    @pl.when(i >= n_lat)
    def _():
        emit(ctx_ref[...])


def _prenorm0(x2, ctx2, g, mods, layer, seq, ctx_row):
    t, d = x2.shape
    tc = ctx2.shape[0]
    tm = ROW_TILE
    n_lat, n_ctx = t // tm, tc // tm
    row = lambda i: jnp.where(i < n_lat, (i * tm) // seq, ctx_row)
    mspec = lambda chunk: pl.BlockSpec((None, None, None, 1, d), _mod_spec(layer, chunk, row))
    return pl.pallas_call(
        functools.partial(_prenorm0_kernel, n_lat=n_lat),
        grid=(n_lat + n_ctx,),
        in_specs=[pl.BlockSpec((tm, d), lambda i: (jnp.minimum(i, n_lat - 1), 0)),
                  pl.BlockSpec((tm, d), lambda i: (jnp.maximum(i - n_lat, 0), 0)),
                  pl.BlockSpec((1, d), lambda i: (0, 0)),
                  mspec(0), mspec(1)],
        out_specs=pl.BlockSpec((tm, d), lambda i: (i, 0)),
        out_shape=jax.ShapeDtypeStruct((t + tc, d), BF16),
        compiler_params=_cparams("arbitrary"),
        name="prenorm0",
    )(x2, ctx2, g.reshape(1, d), mods, mods)


def _ws_body(x_ref, w_hbm, *refs, n_extra, n_out, epilogue, tm, rc, jb):
    extra, outs = refs[:n_extra], refs[n_extra:n_extra + n_out]
    ws_ref, wb_ref, sem = refs[n_extra + n_out:]
    j, i = pl.program_id(0), pl.program_id(1)
    tn = wb_ref.shape[1]

    def copy(jj):
        cols = pl.ds(pl.multiple_of((jj + jb) * tn, tn), tn)
        return pltpu.make_async_copy(w_hbm.at[:, cols], ws_ref, sem)

    @pl.when(jnp.logical_and(j == 0, i == 0))
    def _():
        copy(0).start()

    @pl.when(i == 0)
    def _():
        copy(j).wait()
        _cast_rows_to_bf16(ws_ref, wb_ref)

        @pl.when(j + 1 < pl.num_programs(0))
        def _():
            copy(j + 1).start()

    for c in range(tm // rc):
        rows = slice(c * rc, (c + 1) * rc)
        epilogue(_dot(x_ref[rows, :], wb_ref[...]), rows, extra, outs)


def _ws_matmul(x, w, col0, n_cols, epilogue, extras, extra_specs, out_shapes, out_specs, name,
               tm=MM_TM, tn=MM_TN, m=None):
    k = x.shape[1]
    m = x.shape[0] if m is None else m
    assert m % tm == 0 and n_cols % tn == 0 and col0 % tn == 0
    body = functools.partial(_ws_body, n_extra=len(extras), n_out=len(out_shapes),
                             epilogue=epilogue, tm=tm, rc=MM_RC, jb=col0 // tn)
    return pl.pallas_call(
        body,
        grid=(n_cols // tn, m // tm),
        in_specs=[pl.BlockSpec((tm, k), lambda j, i: (i, 0)),
                  pl.BlockSpec(memory_space=pl.ANY)] + list(extra_specs),
        out_specs=out_specs,
        out_shape=out_shapes,
        scratch_shapes=[pltpu.VMEM((k, tn), F32), pltpu.VMEM((k, tn), BF16), pltpu.SemaphoreType.DMA(())],
        compiler_params=_cparams("arbitrary", "arbitrary"),
        name=name,
    )(x, w, *extras)


def _swap32(y):
    lane = lax.broadcasted_iota(I32, y.shape, 1)
    return jnp.where((lane & 32) != 0, pltpu.roll(y, 32, 1), pltpu.roll(y, 96, 1))


def _ep_q(acc, rows, extra, outs):
    g_ref, cos_ref, sin_ref = extra
    qrot_ref, qpl_ref = outs
    cos, sin = cos_ref[rows, :], sin_ref[rows, :]
    for h in range(acc.shape[1] // NA_HEAD_DIM):
        cols = slice(h * NA_HEAD_DIM, (h + 1) * NA_HEAD_DIM)
        y = _rms(acc[:, cols], g_ref[...]) * NA_QSCALE
        qpl_ref[h, rows, :] = y.astype(BF16)
        qrot_ref[h, rows, :] = (y * cos + _swap32(y) * sin).astype(BF16)


def _ep_k(acc, rows, extra, outs):
    g_ref, cos_ref, sin_ref = extra
    (krot_ref,) = outs
    cos, sin = cos_ref[rows, :], sin_ref[rows, :]
    for h in range(acc.shape[1] // NA_HEAD_DIM):
        cols = slice(h * NA_HEAD_DIM, (h + 1) * NA_HEAD_DIM)
        y = _rms(acc[:, cols], g_ref[...])
        krot_ref[h, rows, :] = (y * cos + _swap32(y) * sin).astype(BF16)


def _ep_v(acc, rows, extra, outs):
    for h in range(acc.shape[1] // NA_HEAD_DIM):
        outs[0][h, rows, :] = acc[:, h * NA_HEAD_DIM:(h + 1) * NA_HEAD_DIM].astype(BF16)


def _ep_residual(acc, rows, extra, outs):
    x_ref, gate_ref = extra
    outs[0][rows, :] = x_ref[rows, :] + gate_ref[...] * acc


def _ep_bias_residual(acc, rows, extra, outs):
    x_ref, gate_ref, b_ref = extra
    outs[0][rows, :] = x_ref[rows, :] + gate_ref[...] * (acc + b_ref[...])


def _ep_bias_gelu(acc, rows, extra, outs):
    (b_ref,) = extra
    a = acc + b_ref[...]
    outs[0][rows, :] = (0.5 * a * (1.0 + lax.erf(a * (2.0 ** -0.5)))).astype(BF16)


def _rope_tables(seq, extra_rows):
    t = jnp.arange(seq, dtype=I32)
    pos = jnp.stack([t // GRID_W, t % GRID_W], axis=-1).astype(F32)
    inv_freq = ROPE_BASE ** (-jnp.arange(ROPE_FREQS, dtype=F32) / ROPE_FREQS)
    ang = pos[:, :, None] * inv_freq
    cos, sin = jnp.cos(ang), jnp.sin(ang)
    cos = jnp.stack([cos, cos], axis=2).reshape(seq, NA_HEAD_DIM)
    sin = jnp.stack([-sin, sin], axis=2).reshape(seq, NA_HEAD_DIM)
    cos = jnp.concatenate([cos, jnp.ones((extra_rows, NA_HEAD_DIM), F32)], axis=0)
    sin = jnp.concatenate([sin, jnp.zeros((extra_rows, NA_HEAD_DIM), F32)], axis=0)
    return cos, sin


def _qkv(h_all, w_qkv, q_g, k_g, t, seq, d):
    tm, tn = MM_TM, MM_TN_WIDE
    hd = NA_HEAD_DIM
    m_all = h_all.shape[0]
    n_lat, per_seq = t // tm, seq // tm
    cos, sin = _rope_tables(seq, tm)
    g_spec = pl.BlockSpec((1, hd), lambda j, i: (0, 0))
    tab_idx = lambda j, i: (jnp.where(i < n_lat, i % per_seq, per_seq), 0)
    tab_spec = pl.BlockSpec((tm, hd), tab_idx)
    out_spec = pl.BlockSpec((tn // hd, tm, hd), lambda j, i: (j, i, 0))
    q_rot, q_pl = _ws_matmul(
        h_all, w_qkv, 0, d, _ep_q, [q_g.reshape(1, -1), cos, sin], [g_spec, tab_spec, tab_spec],
        [jax.ShapeDtypeStruct((d // hd, t, hd), BF16)] * 2, [out_spec, out_spec], "qkv_q", tn=tn, m=t)
    (k_rot,) = _ws_matmul(
        h_all, w_qkv, d, d, _ep_k, [k_g.reshape(1, -1), cos, sin], [g_spec, tab_spec, tab_spec],
        [jax.ShapeDtypeStruct((d // hd, m_all, hd), BF16)], [out_spec], "qkv_k", tn=tn)
    (v,) = _ws_matmul(
        h_all, w_qkv, 2 * d, d, _ep_v, [], [],
        [jax.ShapeDtypeStruct((d // hd, m_all, hd), BF16)], [out_spec], "qkv_v", tn=tn)
    return q_rot, q_pl, k_rot, v


def _bias_tables(rpb):
    n_heads, n_dr, _ = rpb.shape
    q = jnp.arange(GRID_W, dtype=I32)[:, None]
    kc = jnp.arange(GRID_W, dtype=I32)[None, :]
    dc = jnp.clip(kc - q + NA_KW - 1, 0, 2 * NA_KW - 2)
    cs = jnp.clip(q - NA_KW // 2, 0, GRID_W - NA_KW)
    in_win = (kc >= cs) & (kc < cs + NA_KW)
    onehot = (dc[None] == jnp.arange(2 * NA_KW - 1, dtype=I32)[:, None, None]).astype(F32)
    picked = jnp.einsum('hdj,jqk->hqdk', rpb.astype(F32), onehot, precision=lax.Precision.HIGHEST)
    c = jnp.where(in_win[None, :, None, :], picked * LOG2_E, MASK_VALUE)
    c = c.reshape(n_heads, GRID_W, n_dr * GRID_W)
    width = (n_dr + 2) * GRID_W
    c = jnp.pad(c, ((0, 0), (0, 0), (0, width - n_dr * GRID_W)))
    return c[:, :, :width - GRID_W], c[:, :, GRID_W:]


def _pipelined(n, start, finish, depth):
    pending = [start(i) for i in range(min(depth, n))]
    for i in range(n):
        s = pending.pop(0)
        if i + depth < n:
            pending.append(start(i + depth))
        finish(i, s)


def _attn_kernel(q_ref, qp_ref, k_ref, v_ref, kc_ref, vc_ref, c0_ref, c1_ref, o_ref,
                 mc_ref, lc_ref, oc_ref, *, rows):
    win = NA_KH * GRID_W
    half = NA_KH // 2
    seq = rows * GRID_W
    hd = q_ref.shape[1]

    def ctx_scores(c):
        return _dot_nt(qp_ref[c * ATTN_CTX_CHUNK:(c + 1) * ATTN_CTX_CHUNK, :], kc_ref[...])

    def ctx_finish(c, s):
        sl = slice(c * ATTN_CTX_CHUNK, (c + 1) * ATTN_CTX_CHUNK)
        m = jnp.max(s, axis=-1, keepdims=True)
        p = jnp.exp2(s - m)
        mc_ref[sl, :] = jnp.broadcast_to(m, (ATTN_CTX_CHUNK, hd))
        lc_ref[sl, :] = jnp.broadcast_to(jnp.sum(p, axis=-1, keepdims=True), (ATTN_CTX_CHUNK, hd))
        oc_ref[sl, :] = _dot(p.astype(BF16), vc_ref[...])

    _pipelined(seq // ATTN_CTX_CHUNK, ctx_scores, ctx_finish, 2)

    def key_start(r):
        return min(max(r - half, 0), rows - NA_KH)

    def loc_scores(r):
        k0 = key_start(r) * GRID_W
        off = (NA_KH - 1 - (r - key_start(r))) * GRID_W
        if off % LANES == 0:
            bias = c0_ref[:, off:off + win]
        else:
            bias = c1_ref[:, off - GRID_W:off - GRID_W + win]
        return _dot_nt(q_ref[r * GRID_W:(r + 1) * GRID_W, :], k_ref[k0:k0 + win, :]) + bias

    def loc_finish(r, s):
        sl = slice(r * GRID_W, (r + 1) * GRID_W)
        k0 = key_start(r) * GRID_W
        m_loc = jnp.max(s, axis=-1, keepdims=True)
        p = jnp.exp2(s - m_loc)
        l_loc = jnp.sum(p, axis=-1, keepdims=True)
        o_loc = _dot(p.astype(BF16), v_ref[k0:k0 + win, :])
        m_ctx = mc_ref[sl, :]
        m = jnp.maximum(m_loc, m_ctx)
        a = jnp.exp2(m_loc - m)
        b = jnp.exp2(m_ctx - m)
        denom = a * l_loc + b * lc_ref[sl, :]
        o_ref[sl, :] = ((a * o_loc + b * oc_ref[sl, :]) / denom).astype(BF16)

    _pipelined(rows, loc_scores, loc_finish, ATTN_DEPTH)


def _attention(q_rot, q_pl, k_rot, v, rpb, batch, seq, ctx_len):
    n_heads, t, hd = q_rot.shape
    rows = seq // GRID_W
    c0, c1 = _bias_tables(rpb)
    ctx_blk0 = t // ctx_len
    lat = pl.BlockSpec((None, seq, hd), lambda h, b: (h, b, 0))
    cx = pl.BlockSpec((None, ctx_len, hd), lambda h, b: (h, ctx_blk0 + b, 0))
    tab = pl.BlockSpec((None, GRID_W, c0.shape[2]), lambda h, b: (h, 0, 0))
    return pl.pallas_call(
        functools.partial(_attn_kernel, rows=rows),
        grid=(n_heads, batch),
        in_specs=[lat, lat, lat, lat, cx, cx, tab, tab],
        out_specs=pl.BlockSpec((seq, hd), lambda h, b: (b, h)),
        out_shape=jax.ShapeDtypeStruct((t, n_heads * hd), BF16),
        scratch_shapes=[pltpu.VMEM((seq, hd), F32)] * 3,
        compiler_params=_cparams("arbitrary", "arbitrary"),
        name="attention",
    )(q_rot, q_pl, k_rot, v, k_rot, v, c0, c1)


def _router_logits(wrt_ref, h1_ref, h2_ref):
    w = wrt_ref[...]
    w1 = w.astype(BF16)
    w2 = (w - w1.astype(F32)).astype(BF16)
    a = _dot_nt(jnp.concatenate([w1, w2], axis=0), h1_ref[...])
    b = _dot_nt(w1, h2_ref[...])
    return a[:N_EXPERTS] + (a[N_EXPERTS:] + b)


def _route(logits, rb_ref, carry_ref, eidx_ref, ew_ref, rank_ref, cnt_ref):
    tm = logits.shape[1]
    scores = jax.nn.sigmoid(logits)
    sel = scores + rb_ref[...]

    def top2(vals):
        def first_max(vs):
            m = functools.reduce(jnp.maximum, vs)
            idx = jnp.full(m.shape, len(vs) - 1, I32)
            for k in range(len(vs) - 2, -1, -1):
                idx = jnp.where(vs[k] == m, k, idx)
            return m, idx
        m1, i1 = first_max(vals)
        m2, i2 = first_max([jnp.where(i1 == k, -jnp.inf, v) for k, v in enumerate(vals)])
        return m1, i1, m2, i2

    grp = []
    for g in range(N_GROUPS):
        vals = [sel[g * EXPERTS_PER_GROUP + k:g * EXPERTS_PER_GROUP + k + 1, :]
                for k in range(EXPERTS_PER_GROUP)]
        grp.append(top2(vals))
    gsum = [m1 + m2 for m1, _, m2, _ in grp]
    gmax = functools.reduce(jnp.maximum, gsum)
    g_idx = jnp.full(gmax.shape, N_GROUPS - 1, I32)
    for g in range(N_GROUPS - 2, -1, -1):
        g_idx = jnp.where(gsum[g] == gmax, g, g_idx)
    i1 = grp[N_GROUPS - 1][1]
    i2 = grp[N_GROUPS - 1][3]
    for g in range(N_GROUPS - 2, -1, -1):
        i1 = jnp.where(g_idx == g, grp[g][1], i1)
        i2 = jnp.where(g_idx == g, grp[g][3], i2)
    e1 = g_idx * EXPERTS_PER_GROUP + i1
    e2 = g_idx * EXPERTS_PER_GROUP + i2
    e_iota = lax.broadcasted_iota(I32, (N_EXPERTS, tm), 0)
    hit1, hit2 = e_iota == e1, e_iota == e2
    s1 = jnp.sum(jnp.where(hit1, scores, 0.0), axis=0, keepdims=True)
    s2 = jnp.sum(jnp.where(hit2, scores, 0.0), axis=0, keepdims=True)
    tot = s1 + s2
    eidx_ref[0:1, :] = e1
    eidx_ref[1:2, :] = e2
    ew_ref[0:1, :] = s1 / tot
    ew_ref[1:2, :] = s2 / tot
    onehot = (hit1 | hit2).astype(BF16)
    upper = (lax.broadcasted_iota(I32, (tm, tm), 0) < lax.broadcasted_iota(I32, (tm, tm), 1)).astype(BF16)
    before = _dot(onehot, upper) + carry_ref[...]
    rank_ref[0:1, :] = jnp.sum(jnp.where(hit1, before, 0.0), axis=0, keepdims=True).astype(I32)
    rank_ref[1:2, :] = jnp.sum(jnp.where(hit2, before, 0.0), axis=0, keepdims=True).astype(I32)
    carry_ref[...] += jnp.sum(onehot.astype(F32), axis=1, keepdims=True)
    cnt_ref[...] = jnp.broadcast_to(carry_ref[...], cnt_ref.shape)


def _prenorm_router_kernel(x_ref, g_ref, sh_ref, sc_ref, wrt_ref, rb_ref,
                           hp_ref, eidx_ref, ew_ref, rank_ref, cnt_ref, carry_ref, h1_ref, h2_ref, gs_ref):
    @pl.when(pl.program_id(0) == 0)
    def _():
        carry_ref[...] = jnp.zeros_like(carry_ref)

    tm, d = x_ref.shape
    half = d // 2
    rc, cw = NORM_CHUNK_ROWS, NORM_CHUNK_COLS
    gs_ref[...] = g_ref[...] * (1.0 + sc_ref[...])

    def chunk(ci, carry):
        rows = pl.ds(pl.multiple_of(ci * rc, rc), rc)
        ss = jnp.zeros((rc, cw), F32)
        for cc in range(d // cw):
            xv = x_ref[rows, cc * cw:(cc + 1) * cw]
            ss = ss + xv * xv
        inv = lax.rsqrt(jnp.sum(ss, axis=-1, keepdims=True) / d + NORM_EPS)

        def modulated_bits(cols):
            h = x_ref[rows, cols] * inv * gs_ref[:, cols] + sh_ref[:, cols]
            h1 = h.astype(BF16)
            hb = h1.astype(F32)
            h1_ref[rows, cols] = h1
            h2_ref[rows, cols] = (h - hb).astype(BF16)
            return lax.bitcast_convert_type(hb, U32)

        for cc in range(half // cw):
            lo = modulated_bits(slice(cc * cw, (cc + 1) * cw))
            hi = modulated_bits(slice(half + cc * cw, half + (cc + 1) * cw))
            hp_ref[rows, cc * cw:(cc + 1) * cw] = (lo >> 16) | hi
        return carry

    lax.fori_loop(0, tm // rc, chunk, 0, unroll=4)
    _route(_router_logits(wrt_ref, h1_ref, h2_ref), rb_ref, carry_ref, eidx_ref, ew_ref, rank_ref, cnt_ref)


def _router_out(t, d, tm):
    shapes = [jax.ShapeDtypeStruct((t, d // 2), U32),
              jax.ShapeDtypeStruct((TOP_K, t), I32),
              jax.ShapeDtypeStruct((TOP_K, t), F32),
              jax.ShapeDtypeStruct((TOP_K, t), I32),
              jax.ShapeDtypeStruct((N_EXPERTS, LANES), F32)]
    specs = [pl.BlockSpec((tm, d // 2), lambda i: (i, 0)),
             pl.BlockSpec((TOP_K, tm), lambda i: (0, i)),
             pl.BlockSpec((TOP_K, tm), lambda i: (0, i)),
             pl.BlockSpec((TOP_K, tm), lambda i: (0, i)),
             pl.BlockSpec((N_EXPERTS, LANES), lambda i: (0, 0))]
    return shapes, specs


def _prenorm_router(x2, g, mods, layer, seq, router_w, router_b):
    t, d = x2.shape
    tm = ROW_TILE
    row = lambda i: (i * tm) // seq
    mspec = lambda chunk: pl.BlockSpec((None, None, None, 1, d), _mod_spec(layer, chunk, row))
    shapes, specs = _router_out(t, d, tm)
    return pl.pallas_call(
        _prenorm_router_kernel,
        grid=(t // tm,),
        in_specs=[pl.BlockSpec((tm, d), lambda i: (i, 0)),
                  pl.BlockSpec((1, d), lambda i: (0, 0)),
                  mspec(3), mspec(4),
                  pl.BlockSpec((N_EXPERTS, d), lambda i: (0, 0)),
                  pl.BlockSpec((N_EXPERTS, 1), lambda i: (0, 0))],
        out_specs=specs,
        out_shape=shapes,
        scratch_shapes=[pltpu.VMEM((N_EXPERTS, 1), F32), pltpu.VMEM((tm, d), BF16), pltpu.VMEM((tm, d), BF16),
                        pltpu.VMEM((1, d), F32)],
        compiler_params=_cparams("arbitrary"),
        name="prenorm_router",
    )(x2, g.reshape(1, d), mods, mods, router_w.T, router_b.reshape(N_EXPERTS, 1))


def _routing_plan(eidx, rank, cnt, t):
    assert GATHER_ROWS % MOE_TM == 0
    max_rows = (TOP_K * t + N_EXPERTS * (MOE_TM - 1)) // MOE_TM * MOE_TM
    n_rows = -(-max_rows // GATHER_ROWS) * GATHER_ROWS
    n_tiles = n_rows // MOE_TM
    counts = cnt[:, 0].astype(I32)
    padded = ((counts + MOE_TM - 1) // MOE_TM) * MOE_TM
    ends = jnp.cumsum(padded)
    offs = ends - padded
    hit = eidx[None] == jnp.arange(N_EXPERTS, dtype=I32)[:, None, None]
    dest = jnp.sum(jnp.where(hit, offs[:, None, None], 0), axis=0) + rank
    n_valid = (ends[-1] // MOE_TM).astype(I32)
    tile_start = jnp.arange(n_tiles, dtype=I32) * MOE_TM
    tile_expert = jnp.sum((tile_start[:, None] >= ends[None, :]).astype(I32), axis=1)
    last = jnp.minimum(jnp.maximum(n_valid - 1, 0), n_tiles - 1)
    tile_expert = jnp.where(jnp.arange(n_tiles) < n_valid, tile_expert, tile_expert[last])
    tile_expert = jnp.minimum(tile_expert, N_EXPERTS - 1).astype(I32)
    e_ids = jnp.arange(N_EXPERTS, dtype=I32)
    owner = jnp.where(padded > 0, e_ids, N_EXPERTS)
    next_owner = jnp.concatenate([lax.cummin(owner[::-1])[::-1][1:], jnp.full((1,), N_EXPERTS, I32)])
    next_owner = jnp.where(next_owner >= N_EXPERTS, -1, next_owner)
    tile_next = jnp.sum(jnp.where(tile_expert[:, None] == e_ids[None, :], next_owner[None, :], 0), axis=1).astype(I32)
    tok = jnp.tile(jnp.arange(t, dtype=I32), TOP_K)
    src_tok = (jnp.arange(n_rows, dtype=I32) % t).at[dest.reshape(-1)].set(tok)
    return dest, src_tok, tile_expert, tile_next, n_valid.reshape(1)


def _row_copy(src_tiles, row, dst_tiles, tile, sub, sem):
    src = src_tiles.at[lax.shift_right_logical(row, 3), pl.ds(row & (SUBLANES - 1), 1), :]
    return pltpu.make_async_copy(src, dst_tiles.at[tile, pl.ds(sub, 1), :], sem)


def _all_rows_wait(src_tiles, dst_tiles, sem):
    pltpu.make_async_copy(src_tiles.at[pl.ds(0, dst_tiles.shape[0])], dst_tiles, sem).wait()


def _issue_rows(src_tiles, idx_ref, r0, count, dst_tiles, sem):
    tile0 = lax.shift_right_logical(r0, 3)
    for k in range(count):
        _row_copy(src_tiles, idx_ref[0, r0 + k], dst_tiles, tile0 + k // SUBLANES, k % SUBLANES,
                  sem).start(priority=k % 2)


def _load_rows(buf_ref, slot, r0, count, cols=slice(None)):
    w = buf_ref[slot, pl.ds(lax.shift_right_logical(r0, 3), count // SUBLANES), :, cols]
    return w.reshape(count, w.shape[-1])


def _gather_kernel(nrows_ref, src0_ref, srcn_ref, hp_hbm, o_ref, buf_ref, sem):
    i = pl.program_id(0)
    tg, half = o_ref.shape[0], buf_ref.shape[-1]
    rc = DMA_CHUNK_ROWS
    slot = i % 2
    cur_valid = i * tg < nrows_ref[0]
    nxt_valid = (i + 1) * tg < nrows_ref[0]

    @pl.when(i == 0)
    def _():
        def body(c, carry):
            _issue_rows(hp_hbm, src0_ref, pl.multiple_of(c * rc, rc), rc, buf_ref.at[0], sem.at[0])
            return carry
        lax.fori_loop(0, tg // rc, body, 0)

    @pl.when(cur_valid)
    def _():
        _all_rows_wait(hp_hbm, buf_ref.at[slot], sem.at[slot])

    def unpack(r0):
        rows = pl.ds(r0, rc)
        w = _load_rows(buf_ref, slot, r0, rc)
        o_ref[rows, :half] = _unpack_lo(w).astype(BF16)
        o_ref[rows, half:] = _unpack_hi(w).astype(BF16)

    @pl.when(jnp.logical_and(cur_valid, nxt_valid))
    def _():
        def body(c, carry):
            r0 = pl.multiple_of(c * rc, rc)
            unpack(r0)
            _issue_rows(hp_hbm, srcn_ref, r0, rc, buf_ref.at[1 - slot], sem.at[1 - slot])
            return carry
        lax.fori_loop(0, tg // rc, body, 0)

    @pl.when(jnp.logical_and(cur_valid, jnp.logical_not(nxt_valid)))
    def _():
        def body(c, carry):
            unpack(pl.multiple_of(c * rc, rc))
            return carry
        lax.fori_loop(0, tg // rc, body, 0)

    @pl.when(jnp.logical_not(cur_valid))
    def _():
        o_ref[...] = jnp.zeros_like(o_ref)


def _gather_rows(hp, src_tok, n_valid_rows):
    n_rows = src_tok.shape[0]
    half = hp.shape[1]
    tg = GATHER_ROWS
    n_t = n_rows // tg
    src_tiles = src_tok.reshape(n_t, 1, tg)
    return pl.pallas_call(
        _gather_kernel,
        grid_spec=pltpu.PrefetchScalarGridSpec(
            num_scalar_prefetch=1,
            grid=(n_t,),
            in_specs=[pl.BlockSpec((None, 1, tg), lambda i, n: (0, 0, 0), memory_space=pltpu.SMEM),
                      pl.BlockSpec((None, 1, tg), lambda i, n: (jnp.minimum(i + 1, n_t - 1), 0, 0),
                                   memory_space=pltpu.SMEM),
                      pl.BlockSpec(memory_space=pl.ANY)],
            out_specs=pl.BlockSpec((tg, 2 * half), lambda i, n: (i, 0)),
            scratch_shapes=[pltpu.VMEM((2, tg // SUBLANES, SUBLANES, half), U32),
                            pltpu.SemaphoreType.DMA((2,))]),
        out_shape=jax.ShapeDtypeStruct((n_rows, 2 * half), BF16),
        compiler_params=_cparams("arbitrary"),
        name="moe_gather",
    )(n_valid_rows, src_tiles, src_tiles, hp.reshape(hp.shape[0] // SUBLANES, SUBLANES, half))


def _cast_rows_to_bf16(src_ref, dst_ref):
    n_rows, n_cols = src_ref.shape
    step = max(CAST_CHUNK_ELEMS // n_cols, 16)

    def body(c, carry):
        rows = pl.ds(pl.multiple_of(c * step, step), step)
        dst_ref[rows, :] = src_ref[rows, :].astype(BF16)
        return carry

    lax.fori_loop(0, n_rows // step, body, 0)


def _stream_expert_weights(te_ref, nx_ref, valid, layer, tn, w_hbms, stage_refs, bf16_refs, sem):
    j, i = pl.program_id(0), pl.program_id(1)

    def copies(e, jj):
        cols = pl.ds(pl.multiple_of(jj * tn, tn), tn)
        return [pltpu.make_async_copy(w.at[layer, e, :, cols], st, sem.at[k])
                for k, (w, st) in enumerate(zip(w_hbms, stage_refs))]

    @pl.when(jnp.logical_and(j == 0, i == 0))
    def _():
        for c in copies(te_ref[0], 0):
            c.start()

    first = jnp.logical_and(valid, jnp.logical_or(i == 0, te_ref[i] != te_ref[jnp.maximum(i - 1, 0)]))

    @pl.when(first)
    def _():
        for c in copies(te_ref[i], j):
            c.wait()
        for st, wb in zip(stage_refs, bf16_refs):
            _cast_rows_to_bf16(st, wb)
        nxt = nx_ref[i]

        @pl.when(nxt >= 0)
        def _():
            for c in copies(nxt, j):
                c.start()

        @pl.when(jnp.logical_and(nxt < 0, j + 1 < pl.num_programs(0)))
        def _():
            for c in copies(te_ref[0], j + 1):
                c.start()


def _moe_a_kernel(te_ref, nx_ref, nv_ref, x_ref, wg_hbm, wu_hbm, o_ref,
                  wgs_ref, wus_ref, wgb_ref, wub_ref, sem, *, layer):
    valid = pl.program_id(1) < nv_ref[0]
    _stream_expert_weights(te_ref, nx_ref, valid, layer, o_ref.shape[1],
                           (wg_hbm, wu_hbm), (wgs_ref, wus_ref), (wgb_ref, wub_ref), sem)

    @pl.when(valid)
    def _():
        x = x_ref[...]
        gate = _dot(x, wgb_ref[...])
        up = _dot(x, wub_ref[...])
        o_ref[...] = (gate * jax.nn.sigmoid(gate) * up).astype(BF16)

    @pl.when(jnp.logical_not(valid))
    def _():
        o_ref[...] = jnp.zeros_like(o_ref)


def _moe_b_kernel(te_ref, nx_ref, nv_ref, a_ref, wd_hbm, o_ref, wds_ref, wdb_ref, sem, *, layer):
    valid = pl.program_id(1) < nv_ref[0]
    _stream_expert_weights(te_ref, nx_ref, valid, layer, wdb_ref.shape[1],
                           (wd_hbm,), (wds_ref,), (wdb_ref,), sem)

    @pl.when(valid)
    def _():
        y = _dot(a_ref[...], wdb_ref[...])
        half = y.shape[1] // 2
        o_ref[...] = _pack_bf16_pair(y[:, :half], y[:, half:])

    @pl.when(jnp.logical_not(valid))
    def _():
        o_ref[...] = jnp.zeros_like(o_ref)


def _moe_experts(xs, tile_expert, tile_next, n_valid, w_gate, w_up, w_down, layer):
    n_rows, d = xs.shape
    f = w_gate.shape[3]
    n_tiles = n_rows // MOE_TM
    row_idx = lambda j, i, te, nx, nv: (jnp.minimum(i, nv[0] - 1), 0)
    hbm = pl.BlockSpec(memory_space=pl.ANY)
    a = pl.pallas_call(
        functools.partial(_moe_a_kernel, layer=layer),
        grid_spec=pltpu.PrefetchScalarGridSpec(
            num_scalar_prefetch=3,
            grid=(f // MOE_A_TN, n_tiles),
            in_specs=[pl.BlockSpec((MOE_TM, d), row_idx), hbm, hbm],
            out_specs=pl.BlockSpec((MOE_TM, MOE_A_TN), lambda j, i, te, nx, nv: (i, j)),
            scratch_shapes=[pltpu.VMEM((d, MOE_A_TN), F32)] * 2 + [pltpu.VMEM((d, MOE_A_TN), BF16)] * 2
            + [pltpu.SemaphoreType.DMA((2,))]),
        out_shape=jax.ShapeDtypeStruct((n_rows, f), BF16),
        compiler_params=_cparams("arbitrary", "arbitrary"),
        name="moe_gate_up",
    )(tile_expert, tile_next, n_valid, xs, w_gate, w_up)
    ys = pl.pallas_call(
        functools.partial(_moe_b_kernel, layer=layer),
        grid_spec=pltpu.PrefetchScalarGridSpec(
            num_scalar_prefetch=3,
            grid=(d // MOE_B_TN, n_tiles),
            in_specs=[pl.BlockSpec((MOE_TM, f), row_idx), hbm],
            out_specs=pl.BlockSpec((MOE_TM, MOE_B_TN // 2), lambda j, i, te, nx, nv: (i, j)),
            scratch_shapes=[pltpu.VMEM((f, MOE_B_TN), F32), pltpu.VMEM((f, MOE_B_TN), BF16),
                            pltpu.SemaphoreType.DMA((1,))]),
        out_shape=jax.ShapeDtypeStruct((n_rows, d // 2), U32),
        compiler_params=_cparams("arbitrary", "arbitrary"),
        name="moe_down",
    )(tile_expert, tile_next, n_valid, a, w_down)
    return ys


def _combine_kernel(dest0_ref, destn_ref, ys_hbm, x_ref, ew_ref, gate_ref, *rest, with_norm):
    if with_norm:
        g_ref, sh_ref, sc_ref, xo_ref, ho_ref, buf_ref, sem = rest
    else:
        xo_ref, buf_ref, sem = rest
    i = pl.program_id(0)
    slot = i % 2
    tc, d = x_ref.shape
    n = TOP_K * tc
    rc, cw = COMBINE_CHUNK_ROWS, COMBINE_CHUNK_COLS
    n_chunks = tc // rc
    per_chunk = n // n_chunks
    q = MOE_B_TN // 2

    @pl.when(i == 0)
    def _():
        def body(c, carry):
            r0 = pl.multiple_of(c * per_chunk, per_chunk)
            _issue_rows(ys_hbm, dest0_ref, r0, per_chunk, buf_ref.at[0], sem.at[0])
            return carry
        lax.fori_loop(0, n_chunks, body, 0)

    _all_rows_wait(ys_hbm, buf_ref.at[slot], sem.at[slot])

    def chunk(ci, carry, prefetch):
        r0 = pl.multiple_of(ci * rc, rc)
        rows, rows2 = pl.ds(r0, rc), pl.ds(r0 + tc, rc)
        a1, a2 = ew_ref[rows, 0:1], ew_ref[rows, 1:2]
        ss = jnp.zeros((rc, cw), F32)
        for j in range(d // (2 * q)):
            for cc in range(q // cw):
                pcols = slice(j * q + cc * cw, j * q + (cc + 1) * cw)
                w1 = _load_rows(buf_ref, slot, r0, rc, pcols)
                w2 = _load_rows(buf_ref, slot, r0 + tc, rc, pcols)
                for part, unpack in ((0, _unpack_lo), (1, _unpack_hi)):
                    c0 = j * 2 * q + part * q + cc * cw
                    cols = slice(c0, c0 + cw)
                    xo = x_ref[rows, cols] + gate_ref[:, cols] * (a1 * unpack(w1) + a2 * unpack(w2))
                    xo_ref[rows, cols] = xo
                    if with_norm:
                        ss = ss + xo * xo
        if with_norm:
            inv = lax.rsqrt(jnp.sum(ss, axis=-1, keepdims=True) / d + NORM_EPS)
            for cc in range(d // cw):
                cols = slice(cc * cw, (cc + 1) * cw)
                y = xo_ref[rows, cols] * inv * g_ref[:, cols]
                ho_ref[rows, cols] = (y * (1.0 + sc_ref[:, cols]) + sh_ref[:, cols]).astype(BF16)
        if prefetch:
            p0 = pl.multiple_of(ci * per_chunk, per_chunk)
            _issue_rows(ys_hbm, destn_ref, p0, per_chunk, buf_ref.at[1 - slot], sem.at[1 - slot])
        return carry

    has_next = i + 1 < pl.num_programs(0)

    @pl.when(has_next)
    def _():
        lax.fori_loop(0, n_chunks, functools.partial(chunk, prefetch=True), 0, unroll=2)

    @pl.when(jnp.logical_not(has_next))
    def _():
        lax.fori_loop(0, n_chunks, functools.partial(chunk, prefetch=False), 0, unroll=2)


def _combine(x2, ys, dest, ew, mods, layer, seq, norm=None):
    t, d = x2.shape
    tc = COMBINE_ROWS
    n_t = t // tc
    dest_tiles = dest.reshape(TOP_K, n_t, tc).transpose(1, 0, 2).reshape(n_t, 1, TOP_K * tc)
    row = lambda i: (i * tc) // seq
    mspec = lambda lay, chunk: pl.BlockSpec((None, None, None, 1, d), _mod_spec(lay, chunk, row))
    in_specs = [pl.BlockSpec((None, 1, TOP_K * tc), lambda i: (0, 0, 0), memory_space=pltpu.SMEM),
                pl.BlockSpec((None, 1, TOP_K * tc), lambda i: (jnp.minimum(i + 1, n_t - 1), 0, 0),
                             memory_space=pltpu.SMEM),
                pl.BlockSpec(memory_space=pl.ANY),
                pl.BlockSpec((tc, d), lambda i: (i, 0)),
                pl.BlockSpec((tc, TOP_K), lambda i: (i, 0)),
                mspec(layer, 5)]
    args = [dest_tiles, dest_tiles, ys.reshape(ys.shape[0] // SUBLANES, SUBLANES, d // 2), x2, ew.T, mods]
    out_shapes = [jax.ShapeDtypeStruct((t, d), F32)]
    out_specs = [pl.BlockSpec((tc, d), lambda i: (i, 0))]
    if norm is not None:
        next_layer, g = norm
        in_specs += [pl.BlockSpec((1, d), lambda i: (0, 0)), mspec(next_layer, 0), mspec(next_layer, 1)]
        args += [g.reshape(1, d), mods, mods]
        out_shapes.append(jax.ShapeDtypeStruct((t, d), BF16))
        out_specs.append(pl.BlockSpec((tc, d), lambda i: (i, 0)))
    return pl.pallas_call(
        functools.partial(_combine_kernel, with_norm=norm is not None),
        grid=(n_t,),
        in_specs=in_specs,
        out_specs=out_specs,
        out_shape=out_shapes,
        scratch_shapes=[pltpu.VMEM((2, TOP_K * tc // SUBLANES, SUBLANES, d // 2), U32),
                        pltpu.SemaphoreType.DMA((2,))],
        compiler_params=_cparams("arbitrary"),
        name="moe_combine",
    )(*args)


def _moe_layer(x2, norm_g, mods, layer, seq, router_w, router_b, w_gate, w_up, w_down, next_norm):
    t = x2.shape[0]
    hp, eidx, ew, rank, cnt = _prenorm_router(x2, norm_g, mods, layer, seq, router_w, router_b)
    dest, src_tok, tile_expert, tile_next, n_valid = _routing_plan(eidx, rank, cnt, t)
    xs = _gather_rows(hp, src_tok, n_valid * MOE_TM)
    ys = _moe_experts(xs, tile_expert, tile_next, n_valid, w_gate, w_up, w_down, layer)
    return _combine(x2, ys, dest, ew, mods, layer, seq, next_norm)


def _sgu_kernel(z_ref, ws_ref, bs_ref, g_ref, b_ref, o_ref):
    width = o_ref.shape[1]
    gdim = width // SGU_GROUPS
    for c in range(z_ref.shape[0] // SGU_CHUNK):
        rows = slice(c * SGU_CHUNK, (c + 1) * SGU_CHUNK)
        v = z_ref[rows, width:].astype(F32)
        mu = jnp.mean(v, axis=-1, keepdims=True)
        vc = v - mu
        var = jnp.mean(vc * vc, axis=-1, keepdims=True)
        vn = (vc * lax.rsqrt(var + NORM_EPS) * g_ref[...] + b_ref[...]).astype(BF16)
        for g in range(SGU_GROUPS):
            cols = slice(g * gdim, (g + 1) * gdim)
            s = _dot(ws_ref[g].astype(BF16), vn[:, cols]) + bs_ref[:, g:g + 1]
            o_ref[rows, cols] = (z_ref[rows, cols].astype(F32) * s).astype(BF16)


def _sgu_gate(z, w_s, b_s, ln_g, ln_b):
    t, two_w = z.shape
    width = two_w // 2
    tm = ROW_TILE
    return pl.pallas_call(
        _sgu_kernel,
        grid=(t // tm,),
        in_specs=[pl.BlockSpec((tm, two_w), lambda i: (i, 0)),
                  pl.BlockSpec((SGU_GROUPS, SGU_CHUNK, SGU_CHUNK), lambda i: (0, 0, 0)),
                  pl.BlockSpec((SGU_CHUNK, SGU_GROUPS), lambda i: (0, 0)),
                  pl.BlockSpec((1, width), lambda i: (0, 0)),
                  pl.BlockSpec((1, width), lambda i: (0, 0))],
        out_specs=pl.BlockSpec((tm, width), lambda i: (i, 0)),
        out_shape=jax.ShapeDtypeStruct((t, width), BF16),
        compiler_params=_cparams("arbitrary"),
        name="sgu_gate",
    )(z, w_s, b_s.T, ln_g.reshape(1, width), ln_b.reshape(1, width))


def kernel(x, c, ctx, c_ctx, ada_w, ada_b, norm1_g, norm2_g, na_w_qkv, na_q_g, na_k_g, na_rpb, na_w_o,
           sgu_w_uv, sgu_b_uv, sgu_ln_g, sgu_ln_b, sgu_w_s, sgu_b_s, sgu_w_out, sgu_b_out,
           router_w, router_b, moe_w_gate, moe_w_up, moe_w_down):
    batch, seq, d = x.shape
    ctx_len = ctx.shape[1]
    t = batch * seq
    x2 = x.reshape(t, d)
    ctx2 = ctx.reshape(batch * ctx_len, d)
    tm, tn = MM_TM, MM_TN

    mod_rows = 16
    cpad = jnp.concatenate([c, c_ctx[None, :], jnp.zeros((mod_rows - batch - 1, d), F32)], axis=0)
    mods = _ada_mods(cpad, ada_w, ada_b).reshape(ada_w.shape[0], mod_rows, 6, 1, d)
    gate_spec = lambda layer, chunk: pl.BlockSpec(
        (None, None, None, 1, tn), lambda j, i: (layer, (i * tm) // seq, chunk, 0, j))
    res_spec = pl.BlockSpec((tm, tn), lambda j, i: (i, j))
    bias_spec = pl.BlockSpec((1, tn), lambda j, i: (0, j))

    h_all = _prenorm0(x2, ctx2, norm1_g[0], mods, 0, seq, batch)
    q_rot, q_pl, k_rot, v = _qkv(h_all, na_w_qkv[0], na_q_g[0], na_k_g[0], t, seq, d)
    att = _attention(q_rot, q_pl, k_rot, v, na_rpb[0], batch, seq, ctx_len)
    (x2,) = _ws_matmul(att, na_w_o[0], 0, d, _ep_residual, [x2, mods], [res_spec, gate_spec(0, 2)],
                       [jax.ShapeDtypeStruct((t, d), F32)], [res_spec], "attn_out")
    x2, h = _moe_layer(x2, norm2_g[0], mods, 0, seq, router_w, router_b,
                       moe_w_gate, moe_w_up, moe_w_down, (1, norm1_g[1]))

    width = sgu_w_uv.shape[2] // 2
    wide = MM_TN_WIDE
    (z,) = _ws_matmul(h, sgu_w_uv[0], 0, 2 * width, _ep_bias_gelu, [sgu_b_uv[0].reshape(1, -1)],
                      [pl.BlockSpec((1, wide), lambda j, i: (0, j))],
                      [jax.ShapeDtypeStruct((t, 2 * width), BF16)],
                      [pl.BlockSpec((tm, wide), lambda j, i: (i, j))], "sgu_uv", tn=wide)
    gated = _sgu_gate(z, sgu_w_s[0], sgu_b_s[0], sgu_ln_g[0], sgu_ln_b[0])
    (x2,) = _ws_matmul(gated, sgu_w_out[0], 0, d, _ep_bias_residual,
                       [x2, mods, sgu_b_out[0].reshape(1, -1)], [res_spec, gate_spec(1, 2), bias_spec],
                       [jax.ShapeDtypeStruct((t, d), F32)], [res_spec], "sgu_out")
    (x2,) = _moe_layer(x2, norm2_g[1], mods, 1, seq, router_w, router_b,
                       moe_w_gate, moe_w_up, moe_w_down, None)
    return x2.reshape(batch, seq, d)
```
